```python
import math
import jax, jax.numpy as jnp
from jax import lax
import numpy as np

D_MODEL = 4096
BATCH = 1
SEQ = 8192
DEPTH = 2

N_META = 16
CHUNK = 128
NORM_EPS = 1e-6
NEG = -1e30

SSD_HEADS = 32
SSD_HEAD_DIM = 64
SSD_INNER = SSD_HEADS * SSD_HEAD_DIM
SSD_GROUPS = 4
SSD_HPG = SSD_HEADS // SSD_GROUPS
SSD_STATE = 128
SSD_CONV = 5
SSD_CONV_CH = SSD_INNER + 2 * SSD_GROUPS * SSD_STATE

ATT_Q_HEADS = 16
ATT_KV_HEADS = 4
ATT_HEAD_DIM = 128
ATT_WINDOW = 128
ATT_BLOCK = 128

ML_HEADS = 6
ML_QK_DIM = 256
ML_V_DIM = 512
ML_WIDTH = ML_HEADS * ML_V_DIM

FN_GROUPS = 4
FN_GROUP_DIM = 256
FN_WIDTH = FN_GROUPS * FN_GROUP_DIM

N_EXPERTS = 16
EC_FACTOR = 2
EXPERT_FF = 1536

AB_IN_SIZES = (SSD_INNER, SSD_CONV_CH, 2 * SSD_HEADS, ATT_Q_HEADS * ATT_HEAD_DIM, ATT_KV_HEADS * ATT_HEAD_DIM, ATT_KV_HEADS * ATT_HEAD_DIM)
AB_IN_WIDTH = sum(AB_IN_SIZES)
AB_MIX_WIDTH = SSD_INNER + ATT_Q_HEADS * ATT_HEAD_DIM
CD_IN_SIZES = (ML_HEADS * ML_QK_DIM, ML_HEADS * ML_QK_DIM, ML_WIDTH, ML_WIDTH, 2 * ML_HEADS, 2 * ML_HEADS, FN_WIDTH)
CD_IN_WIDTH = sum(CD_IN_SIZES)
CD_MIX_WIDTH = ML_WIDTH + FN_WIDTH

kernel_name = 'hybrid_ssd_swa_mlstm_fnet_ecmoe_encoder'


def split_cols(t, sizes):
    return jnp.split(t, [int(c) for c in np.cumsum(sizes)[:-1]], axis=-1)


def rms_norm(x, w):
    xf = x.astype(jnp.float32)
    y = xf * lax.rsqrt(jnp.mean(xf * xf, axis=-1, keepdims=True) + NORM_EPS)
    return (y * w.astype(jnp.float32)).astype(x.dtype)


def pad_front(t):
    return jnp.pad(t, [(0, 0), (CHUNK - N_META, 0)] + [(0, 0)] * (t.ndim - 2))


def flip_seq(t):
    return jnp.flip(t, axis=1)


def alibi_slopes(n):
    return 2.0 ** (-8.0 * jnp.arange(1, n + 1, dtype=jnp.float32) / n)


def centred_dwconv(u, w, b):
    pad = (SSD_CONV - 1) // 2
    out = lax.conv_general_dilated(u, w[:, None, :].astype(u.dtype), window_strides=(1,), padding=[(pad, pad)],
                                   dimension_numbers=('NWC', 'WIO', 'NWC'), feature_group_count=u.shape[-1])
    return out + b.astype(u.dtype)


def ssd_scan(x, dt, a_coef, bmat, cmat):
    bsz, lp, ng, nr, hp = x.shape
    nc = lp // CHUNK
    a = (dt * a_coef).reshape(bsz, nc, CHUNK, ng, nr).transpose(0, 3, 4, 1, 2)
    a_cum = jnp.cumsum(a, axis=-1)
    xdt = (x * dt[..., None]).reshape(bsz, nc, CHUNK, ng, nr, hp)
    bc = bmat.reshape(bsz, nc, CHUNK, ng, -1)
    cc = cmat.reshape(bsz, nc, CHUNK, ng, -1)
    tri = jnp.tril(jnp.ones((CHUNK, CHUNK), dtype=bool))
    decay_in = jnp.exp(jnp.where(tri, a_cum[..., :, None] - a_cum[..., None, :], NEG))
    cb = jnp.einsum('bclgn,bcsgn->bgcls', cc, bc)
    y_diag = jnp.einsum('bgcls,bgrcls,bcsgrp->bclgrp', cb, decay_in, xdt)
    decay_out = jnp.exp(a_cum[..., -1:] - a_cum)
    states = jnp.einsum('bcsgn,bgrcs,bcsgrp->cbgrpn', bc, decay_out, xdt)
    chunk_decay = jnp.exp(a_cum[..., -1]).transpose(3, 0, 1, 2)

    def step(h, inp):
        dec, st = inp
        return dec[..., None, None] * h + st, h

    _, prev = lax.scan(step, jnp.zeros_like(states[0]), (chunk_decay, states))
    y_off = jnp.einsum('bclgn,cbgrpn,bgrcl->bclgrp', cc, prev, jnp.exp(a_cum))
    return (y_diag + y_off).reshape(bsz, lp, ng, nr, hp)


def ssd_mixer(z, xbc, dt_raw, conv_w, conv_b, a_log, dt_bias, d_skip, norm_w):
    bsz, n_tok, _ = z.shape
    xbc = jax.nn.silu(centred_dwconv(xbc, conv_w, conv_b).astype(jnp.float32))
    xs, bmat, cmat = split_cols(xbc, (SSD_INNER, SSD_GROUPS * SSD_STATE, SSD_GROUPS * SSD_STATE))
    xs = xs.reshape(bsz, n_tok, SSD_GROUPS, SSD_HPG, SSD_HEAD_DIM)
    bmat = bmat.reshape(bsz, n_tok, SSD_GROUPS, SSD_STATE)
    cmat = cmat.reshape(bsz, n_tok, SSD_GROUPS, SSD_STATE)
    dt = jax.nn.softplus(dt_raw.astype(jnp.float32) + dt_bias.astype(jnp.float32))
    dt = dt.reshape(bsz, n_tok, 2, SSD_GROUPS, SSD_HPG)
    a_coef = (-jnp.exp(a_log.astype(jnp.float32))).reshape(2, SSD_GROUPS, SSD_HPG)
    xs_p, b_p, c_p, dt_p = pad_front(xs), pad_front(bmat), pad_front(cmat), pad_front(dt)
    y_f = ssd_scan(xs_p, dt_p[:, :, 0], a_coef[0], b_p, c_p)
    y_b = flip_seq(ssd_scan(flip_seq(xs_p), flip_seq(dt_p[:, :, 1]), a_coef[1], flip_seq(b_p), flip_seq(c_p)))
    y = (y_f + y_b)[:, CHUNK - N_META:] + d_skip.astype(jnp.float32).reshape(SSD_GROUPS, SSD_HPG, 1) * xs
    gate = jax.nn.silu(z.astype(jnp.float32)).reshape(bsz, n_tok, SSD_GROUPS, -1)
    y = y.reshape(bsz, n_tok, SSD_GROUPS, -1) * gate
    y = y * lax.rsqrt(jnp.mean(y * y, axis=-1, keepdims=True) + NORM_EPS)
    y = y.reshape(bsz, n_tok, SSD_INNER) * norm_w.astype(jnp.float32)
    return y.astype(z.dtype)


def window_attention(q, k, v, sink):
    bsz, n_tok, _, hd = q.shape
    grp = ATT_Q_HEADS // ATT_KV_HEADS
    n = n_tok - N_META
    nb = n // ATT_BLOCK
    scale = hd ** -0.5
    q = q.reshape(bsz, n_tok, ATT_KV_HEADS, grp, hd)
    slopes = alibi_slopes(ATT_Q_HEADS).reshape(ATT_KV_HEADS, grp)
    sink_h = sink.astype(jnp.float32).reshape(ATT_KV_HEADS, grp)
    km, vm = k[:, :N_META], v[:, :N_META]
    qb = q[:, N_META:].reshape(bsz, nb, ATT_BLOCK, ATT_KV_HEADS, grp, hd)

    def band(t):
        tp = jnp.pad(t[:, N_META:], ((0, 0), (ATT_BLOCK, ATT_BLOCK), (0, 0), (0, 0)))
        tp = tp.reshape(bsz, nb + 2, ATT_BLOCK, ATT_KV_HEADS, hd)
        return jnp.concatenate([tp[:, :-2], tp[:, 1:-1], tp[:, 2:]], axis=2)

    kb, vb = band(k), band(v)
    qi = jnp.arange(ATT_BLOCK)[:, None]
    kj = jnp.arange(3 * ATT_BLOCK)[None, :]
    dist = jnp.abs(kj - ATT_BLOCK - qi)
    key_pos = (jnp.arange(nb) * ATT_BLOCK - ATT_BLOCK)[:, None, None] + kj[None]
    valid = (dist <= ATT_WINDOW)[None] & (key_pos >= 0) & (key_pos < n)
    s_band = jnp.einsum('bnqhgd,bnkhd->bhgnqk', qb, kb).astype(jnp.float32) * scale
    s_band = jnp.where(valid, s_band - slopes[:, :, None, None, None] * dist, NEG)
    s_meta = jnp.einsum('bnqhgd,bmhd->bhgnqm', qb, km).astype(jnp.float32) * scale
    sink_col = jnp.broadcast_to(sink_h.reshape(1, ATT_KV_HEADS, grp, 1, 1, 1), s_meta.shape[:-1] + (1,))
    p = jax.nn.softmax(jnp.concatenate([s_band, s_meta, sink_col], axis=-1), axis=-1).astype(v.dtype)
    o = (jnp.einsum('bhgnqk,bnkhd->bnqhgd', p[..., :3 * ATT_BLOCK], vb)
         + jnp.einsum('bhgnqm,bmhd->bnqhgd', p[..., 3 * ATT_BLOCK:-1], vm))
    o_real = o.reshape(bsz, n, ATT_Q_HEADS * hd)
    span = N_META + ATT_WINDOW
    dist_m = jnp.abs(jnp.arange(span)[None, :] - jnp.arange(N_META)[:, None])
    s_m = jnp.einsum('bqhgd,bkhd->bhgqk', q[:, :N_META], k[:, :span]).astype(jnp.float32) * scale
    s_m = jnp.where(dist_m <= ATT_WINDOW, s_m, NEG)
    sink_m = jnp.broadcast_to(sink_h.reshape(1, ATT_KV_HEADS, grp, 1, 1), s_m.shape[:-1] + (1,))
    p_m = jax.nn.softmax(jnp.concatenate([s_m, sink_m], axis=-1), axis=-1).astype(v.dtype)
    o_m = jnp.einsum('bhgqk,bkhd->bqhgd', p_m[..., :-1], v[:, :span]).reshape(bsz, N_META, ATT_Q_HEADS * hd)
    return jnp.concatenate([o_m, o_real], axis=1)


def mlstm_scan(q, k, v, log_i, log_f):
    bsz, lp, nh, _ = q.shape
    dv = v.shape[-1]
    nc = lp // CHUNK

    def chunked(t):
        return t.reshape(bsz, nc, CHUNK, nh, -1).transpose(0, 3, 1, 2, 4)

    qc, kc, vc = chunked(q), chunked(k), chunked(v)
    li = log_i.reshape(bsz, nc, CHUNK, nh).transpose(0, 3, 1, 2)
    lf = log_f.reshape(bsz, nc, CHUNK, nh).transpose(0, 3, 1, 2)
    bl = jnp.cumsum(lf, axis=-1)
    g = bl[..., -1]
    tri = jnp.tril(jnp.ones((CHUNK, CHUNK), dtype=bool))
    dmat = jnp.where(tri, bl[..., :, None] - bl[..., None, :] + li[..., None, :], NEG)
    a = g[..., None] - bl + li
    m_loc = jnp.max(a, axis=-1)
    wa = jnp.exp(a - m_loc[..., None])
    c_loc = jnp.einsum('bhcs,bhcsk,bhcsv->cbhkv', wa, kc, vc)
    n_loc = jnp.einsum('bhcs,bhcsk->cbhk', wa, kc)

    def step(carry, inp):
        c_st, n_st, m_st = carry
        g_c, ml_c, cl_c, nl_c = inp
        m_new = jnp.maximum(g_c + m_st, ml_c)
        s_old = jnp.exp(g_c + m_st - m_new)
        s_new = jnp.exp(ml_c - m_new)
        new = (s_old[..., None, None] * c_st + s_new[..., None, None] * cl_c,
               s_old[..., None] * n_st + s_new[..., None] * nl_c, m_new)
        return new, (c_st, n_st, m_st)

    init = (jnp.zeros_like(c_loc[0]), jnp.zeros_like(n_loc[0]), jnp.zeros_like(m_loc[:, :, 0]))
    xs = (g.transpose(2, 0, 1), m_loc.transpose(2, 0, 1), c_loc, n_loc)
    _, (c_prev, n_prev, m_prev) = lax.scan(step, init, xs)
    m_inter = bl + m_prev.transpose(1, 2, 0)[..., None]
    m_t = jnp.maximum(jnp.max(dmat, axis=-1), m_inter)
    p = jnp.exp(dmat - m_t[..., None]) * jnp.einsum('bhcqk,bhcsk->bhcqs', qc, kc)
    w_inter = jnp.exp(m_inter - m_t)
    num = jnp.einsum('bhcqs,bhcsv->bhcqv', p, vc) + w_inter[..., None] * jnp.einsum('bhcqk,cbhkv->bhcqv', qc, c_prev)
    den = jnp.sum(p, axis=-1) + w_inter * jnp.einsum('bhcqk,cbhk->bhcq', qc, n_prev)
    h = num / jnp.maximum(jnp.abs(den), jnp.exp(-m_t))[..., None]
    return h.transpose(0, 2, 3, 1, 4).reshape(bsz, lp, nh, dv)


def mlstm_mixer(q, k, v, o_pre, i_pre, f_pre, i_bias, f_bias, head_norm):
    bsz, n_tok, _ = q.shape
    q = q.astype(jnp.float32).reshape(bsz, n_tok, ML_HEADS, ML_QK_DIM)
    k = k.astype(jnp.float32).reshape(bsz, n_tok, ML_HEADS, ML_QK_DIM) * (ML_QK_DIM ** -0.5)
    v = v.astype(jnp.float32).reshape(bsz, n_tok, ML_HEADS, ML_V_DIM)
    log_i = i_pre.astype(jnp.float32).reshape(bsz, n_tok, 2, ML_HEADS) + i_bias.astype(jnp.float32)
    log_f = jax.nn.log_sigmoid(f_pre.astype(jnp.float32).reshape(bsz, n_tok, 2, ML_HEADS) + f_bias.astype(jnp.float32))
    q_p, k_p, v_p, li_p, lf_p = pad_front(q), pad_front(k), pad_front(v), pad_front(log_i), pad_front(log_f)
    h_f = mlstm_scan(q_p, k_p, v_p, li_p[:, :, 0], lf_p[:, :, 0])
    h_b = flip_seq(mlstm_scan(flip_seq(q_p), flip_seq(k_p), flip_seq(v_p), flip_seq(li_p[:, :, 1]), flip_seq(lf_p[:, :, 1])))
    h = (h_f + h_b)[:, CHUNK - N_META:]
    h = h * lax.rsqrt(jnp.mean(h * h, axis=-1, keepdims=True) + NORM_EPS)
    h = h.reshape(bsz, n_tok, ML_WIDTH) * head_norm.astype(jnp.float32)
    return (jax.nn.sigmoid(o_pre.astype(jnp.float32)) * h).astype(o_pre.dtype)


def fourier_mixer(u):
    bsz, n_tok, _ = u.shape
    uf = u.astype(jnp.float32).reshape(bsz, n_tok, FN_GROUPS, FN_GROUP_DIM)
    y = jnp.real(jnp.fft.fft2(uf, axes=(1, 3), norm='ortho'))
    return y.reshape(bsz, n_tok, FN_WIDTH).astype(u.dtype)


def ab_mixer(u, w_in, conv_w, conv_b, a_log, dt_bias, d_skip, ssd_norm, sink, w_out):
    bsz, n_tok, _ = u.shape
    z, xbc, dt_raw, q, k, v = split_cols(u @ w_in, AB_IN_SIZES)
    y_ssd = ssd_mixer(z, xbc, dt_raw.reshape(bsz, n_tok, 2, SSD_HEADS), conv_w, conv_b, a_log, dt_bias, d_skip, ssd_norm)
    y_att = window_attention(q.reshape(bsz, n_tok, ATT_Q_HEADS, ATT_HEAD_DIM),
                             k.reshape(bsz, n_tok, ATT_KV_HEADS, ATT_HEAD_DIM),
                             v.reshape(bsz, n_tok, ATT_KV_HEADS, ATT_HEAD_DIM), sink)
    return jnp.concatenate([y_ssd, y_att.astype(u.dtype)], axis=-1) @ w_out


def cd_mixer(u, w_in, i_bias, f_bias, head_norm, w_out):
    q, k, v, o_pre, i_pre, f_pre, u_fn = split_cols(u @ w_in, CD_IN_SIZES)
    y_ml = mlstm_mixer(q, k, v, o_pre, i_pre, f_pre, i_bias, f_bias, head_norm)
    y_fn = fourier_mixer(u_fn)
    return jnp.concatenate([y_ml, y_fn], axis=-1) @ w_out


def expert_choice_ffn(u, w_router, w_gate, w_up, w_down):
    bsz, n_tok, d = u.shape
    cap = EC_FACTOR * n_tok // N_EXPERTS
    aff = jax.nn.softmax((u @ w_router).astype(jnp.float32), axis=-1)
    gate, idx = lax.top_k(aff.transpose(0, 2, 1), cap)
    xs = jax.vmap(lambda xb, ib: xb[ib])(u, idx)
    hdn = jax.nn.silu(jnp.einsum('becd,edf->becf', xs, w_gate)) * jnp.einsum('becd,edf->becf', xs, w_up)
    out = jnp.einsum('becf,efd->becd', hdn, w_down)
    out = out * gate[..., None].astype(out.dtype)
    y = jax.vmap(lambda ib, ob: jnp.zeros((n_tok, d), ob.dtype).at[ib].add(ob))(idx, out)
    return y.astype(u.dtype)


def setup_inputs(seed: int = 0) -> dict:
    key = jax.random.key(seed)
    ks = jax.random.split(key, 24)
    n_even = (DEPTH + 1) // 2
    n_odd = DEPTH // 2
    f32 = jnp.float32

    def nrm(k, shape, scale):
        return jax.random.normal(k, shape, f32) * scale

    dt0 = jnp.exp(jax.random.uniform(ks[8], (n_even, 2, SSD_HEADS), f32, math.log(1e-3), math.log(1e-1)))
    return {
        'x': nrm(ks[0], (BATCH, SEQ, D_MODEL), 1.0),
        'meta_tokens': nrm(ks[1], (N_META, D_MODEL), 1.0),
        'norm_mix': 1.0 + nrm(ks[2], (DEPTH, D_MODEL), 0.02),
        'ab_w_in': nrm(ks[3], (n_even, D_MODEL, AB_IN_WIDTH), D_MODEL ** -0.5),
        'ab_conv_w': nrm(ks[4], (n_even, SSD_CONV, SSD_CONV_CH), SSD_CONV ** -0.5),
        'ab_conv_b': nrm(ks[5], (n_even, SSD_CONV_CH), 0.02),
        'ab_a_log': jnp.log(jax.random.uniform(ks[6], (n_even, 2, SSD_HEADS), f32, 1.0, 16.0)),
        'ab_dt_bias': dt0 + jnp.log(-jnp.expm1(-dt0)),
        'ab_d_skip': 1.0 + nrm(ks[7], (n_even, SSD_HEADS), 0.1),
        'ab_ssd_norm': 1.0 + nrm(ks[9], (n_even, SSD_INNER), 0.02),
        'ab_sink': nrm(ks[10], (n_even, ATT_Q_HEADS), 0.5),
        'ab_w_out': nrm(ks[11], (n_even, AB_MIX_WIDTH, D_MODEL), AB_MIX_WIDTH ** -0.5),
        'cd_w_in': nrm(ks[12], (n_odd, D_MODEL, CD_IN_WIDTH), D_MODEL ** -0.5),
        'cd_i_bias': nrm(ks[13], (n_odd, 2, ML_HEADS), 0.1),
        'cd_f_bias': jax.random.uniform(ks[14], (n_odd, 2, ML_HEADS), f32, 3.0, 6.0),
        'cd_head_norm': 1.0 + nrm(ks[15], (n_odd, ML_WIDTH), 0.02),
        'cd_w_out': nrm(ks[16], (n_odd, CD_MIX_WIDTH, D_MODEL), CD_MIX_WIDTH ** -0.5),
        'norm_ffn': 1.0 + nrm(ks[17], (DEPTH, D_MODEL), 0.02),
        'moe_router': nrm(ks[18], (DEPTH, D_MODEL, N_EXPERTS), D_MODEL ** -0.5),
        'moe_w_gate': nrm(ks[19], (DEPTH, N_EXPERTS, D_MODEL, EXPERT_FF), D_MODEL ** -0.5),
        'moe_w_up': nrm(ks[20], (DEPTH, N_EXPERTS, D_MODEL, EXPERT_FF), D_MODEL ** -0.5),
        'moe_w_down': nrm(ks[21], (DEPTH, N_EXPERTS, EXPERT_FF, D_MODEL), EXPERT_FF ** -0.5),
        'final_norm': 1.0 + nrm(ks[22], (D_MODEL,), 0.02),
    }


def reference(x, meta_tokens, norm_mix, ab_w_in, ab_conv_w, ab_conv_b, ab_a_log, ab_dt_bias, ab_d_skip,
              ab_ssd_norm, ab_sink, ab_w_out, cd_w_in, cd_i_bias, cd_f_bias, cd_head_norm, cd_w_out,
              norm_ffn, moe_router, moe_w_gate, moe_w_up, moe_w_down, final_norm):
    bsz = x.shape[0]
    meta = jnp.broadcast_to(meta_tokens.astype(x.dtype)[None], (bsz, N_META, x.shape[-1]))
    h = jnp.concatenate([meta, x], axis=1)
    for layer in range(DEPTH):
        j = layer // 2
        u = rms_norm(h, norm_mix[layer])
        if layer % 2 == 0:
            mix = ab_mixer(u, ab_w_in[j], ab_conv_w[j], ab_conv_b[j], ab_a_log[j], ab_dt_bias[j], ab_d_skip[j],
                           ab_ssd_norm[j], ab_sink[j], ab_w_out[j])
        else:
            mix = cd_mixer(u, cd_w_in[j], cd_i_bias[j], cd_f_bias[j], cd_head_norm[j], cd_w_out[j])
        h = h + mix.astype(h.dtype)
        u = rms_norm(h, norm_ffn[layer])
        h = h + expert_choice_ffn(u, moe_router[layer], moe_w_gate[layer], moe_w_up[layer], moe_w_down[layer])
    return rms_norm(h[:, N_META:], final_norm)
```

```python
import functools
import math

import numpy as np
import jax
import jax.numpy as jnp
from jax import lax
from jax.experimental import pallas as pl
from jax.experimental.pallas import tpu as pltpu

F32 = jnp.float32
BF16 = jnp.bfloat16
I32 = jnp.int32
HIGHEST = lax.Precision.HIGHEST

D_MODEL = 4096
N_META = 16
CHUNK = 128
PAD = CHUNK - N_META
NORM_EPS = 1e-6
NEG = -1e30

SSD_HEADS = 32
SSD_HEAD_DIM = 64
SSD_INNER = SSD_HEADS * SSD_HEAD_DIM
SSD_GROUPS = 4
SSD_STATE = 128
SSD_CONV = 5
SSD_CONV_CH = SSD_INNER + 2 * SSD_GROUPS * SSD_STATE

ATT_Q_HEADS = 16
ATT_KV_HEADS = 4
ATT_HEAD_DIM = 128
ATT_WINDOW = 128
ATT_GRP = ATT_Q_HEADS // ATT_KV_HEADS

ML_HEADS = 6
ML_QK_DIM = 256
ML_V_DIM = 512
ML_WIDTH = ML_HEADS * ML_V_DIM
ML_F_LANE = 8

FN_GROUPS = 4
FN_GROUP_DIM = 256
FN_WIDTH = FN_GROUPS * FN_GROUP_DIM

N_EXPERTS = 16
EC_FACTOR = 2
EXPERT_FF = 1536
FF_TILE = 256
OUT_TILE = 512

VMEM_LIMIT = 56 * 1024 * 1024

SSD_DOT_PASSES = 1
ATT_DOT_PASSES = 1
AB_W_PASSES = 1


def _cparams(sem):
    return pltpu.CompilerParams(dimension_semantics=sem, vmem_limit_bytes=VMEM_LIMIT)


def _row_tile(lp):
    return 640 if lp % 640 == 0 else CHUNK


def _sigmoid(x):
    return 1.0 / (1.0 + jnp.exp(-x))


def _softplus(x):
    return jnp.maximum(x, 0.0) + jnp.log1p(jnp.exp(-jnp.abs(x)))


NN = (((1,), (0,)), ((), ()))
NT = (((1,), (1,)), ((), ()))


def _parts(a, passes):
    hi = a.astype(BF16)
    if passes == 1:
        return (hi,)
    return (hi, (a - hi.astype(F32)).astype(BF16))


def _mm(pa, pb, dims=NN):
    def dot(x, y):
        return lax.dot_general(x, y, dims, preferred_element_type=F32)

    acc = dot(pa[0], pb[0])
    if len(pa) > 1:
        acc = acc + dot(pa[1], pb[0])
    if len(pb) > 1:
        acc = acc + dot(pa[0], pb[1])
    return acc


def _rmsnorm_kernel(h_ref, w_ref, o_ref):
    x = h_ref[...]
    y = x * lax.rsqrt(jnp.mean(x * x, axis=-1, keepdims=True) + NORM_EPS) * w_ref[...]
    o_ref[...] = y.astype(o_ref.dtype)


def rmsnorm(h, w, out_dtype):
    lp, d = h.shape
    tm = _row_tile(lp)
    return pl.pallas_call(
        _rmsnorm_kernel,
        grid=(lp // tm,),
        in_specs=[pl.BlockSpec((tm, d), lambda i: (i, 0)), pl.BlockSpec((1, d), lambda i: (0, 0))],
        out_specs=pl.BlockSpec((tm, d), lambda i: (i, 0)),
        out_shape=jax.ShapeDtypeStruct((lp, d), out_dtype),
        compiler_params=_cparams(("arbitrary",)),
        name="rmsnorm",
    )(h, w.reshape(1, d))


def _rmsnorm_router_kernel(h_ref, w_ref, wrt_ref, u_ref, lg_ref):
    x = h_ref[...]
    y = x * lax.rsqrt(jnp.mean(x * x, axis=-1, keepdims=True) + NORM_EPS) * w_ref[...]
    u_ref[...] = y
    lg_ref[...] = lax.dot_general(wrt_ref[...], y, (((1,), (1,)), ((), ())), precision=HIGHEST,
                                  preferred_element_type=F32)


def rmsnorm_router(h, w, w_router):
    lp, d = h.shape
    tm = _row_tile(lp)
    return pl.pallas_call(
        _rmsnorm_router_kernel,
        grid=(lp // tm,),
        in_specs=[pl.BlockSpec((tm, d), lambda i: (i, 0)), pl.BlockSpec((1, d), lambda i: (0, 0)),
                  pl.BlockSpec((N_EXPERTS, d), lambda i: (0, 0))],
        out_specs=[pl.BlockSpec((tm, d), lambda i: (i, 0)), pl.BlockSpec((N_EXPERTS, tm), lambda i: (0, i))],
        out_shape=[jax.ShapeDtypeStruct((lp, d), F32), jax.ShapeDtypeStruct((N_EXPERTS, lp), F32)],
        compiler_params=_cparams(("arbitrary",)),
        name="rmsnorm_router",
    )(h, w.reshape(1, d), w_router.T)


def _final_norm_kernel(h_ref, w_ref, o_ref):
    x = h_ref[...]
    o_ref[...] = x * lax.rsqrt(jnp.mean(x * x, axis=-1, keepdims=True) + NORM_EPS) * w_ref[...]


def final_rmsnorm(h, w):
    lp, d = h.shape
    n = lp - CHUNK
    return pl.pallas_call(
        _final_norm_kernel,
        grid=(n // CHUNK,),
        in_specs=[pl.BlockSpec((CHUNK, d), lambda i: (i + 1, 0)), pl.BlockSpec((1, d), lambda i: (0, 0))],
        out_specs=pl.BlockSpec((CHUNK, d), lambda i: (i, 0)),
        out_shape=jax.ShapeDtypeStruct((n, d), F32),
        compiler_params=_cparams(("arbitrary",)),
        name="final_norm",
    )(h, w.reshape(1, d))


def _matmul_kernel(*refs, k_splits, has_res, w_passes):
    n_x = len(k_splits)
    x_refs = refs[:n_x]
    w_ref = refs[n_x]
    r_ref = refs[n_x + 1] if has_res else None
    o_ref = refs[n_x + 1 + int(has_res)]
    wb_ref = refs[n_x + 2 + int(has_res)]

    @pl.when(pl.program_id(1) == 0)
    def _():
        w = w_ref[...]
        hi = w.astype(BF16)
        wb_ref[0] = hi
        if w_passes == 2:
            wb_ref[1] = (w - hi.astype(F32)).astype(BF16)

    acc = None
    k0 = 0
    for x_ref, kk in zip(x_refs, k_splits):
        x = x_ref[...]
        for p in range(w_passes):
            part = jnp.dot(x, wb_ref[p, k0:k0 + kk, :], preferred_element_type=F32)
            acc = part if acc is None else acc + part
        k0 += kk
    if has_res:
        acc = acc + r_ref[...]
    o_ref[...] = acc.astype(o_ref.dtype)


def matmul(xs, w, tn, res=None, out_dtype=F32, w_passes=1):
    lp = xs[0].shape[0]
    k_splits = tuple(int(x.shape[1]) for x in xs)
    k, n = w.shape
    assert sum(k_splits) == k and n % tn == 0
    tm = _row_tile(lp)
    in_specs = [pl.BlockSpec((tm, kk), lambda j, i: (i, 0)) for kk in k_splits]
    in_specs.append(pl.BlockSpec((k, tn), lambda j, i: (0, j)))
    args = list(xs) + [w]
    if res is not None:
        in_specs.append(pl.BlockSpec((tm, tn), lambda j, i: (i, j)))
        args.append(res)
    return pl.pallas_call(
        functools.partial(_matmul_kernel, k_splits=k_splits, has_res=res is not None, w_passes=w_passes),
        grid=(n // tn, lp // tm),
        in_specs=in_specs,
        out_specs=pl.BlockSpec((tm, tn), lambda j, i: (i, j)),
        out_shape=jax.ShapeDtypeStruct((lp, n), out_dtype),
        scratch_shapes=[pltpu.VMEM((w_passes, k, tn), BF16)],
        compiler_params=_cparams(("arbitrary", "arbitrary")),
        name="matmul",
    )(*args)


def _ssd_conv_kernel(cur_ref, prev_ref, next_ref, w_ref, b_ref, o_ref, ext_ref):
    c = pl.program_id(0)
    nch = pl.num_programs(0)
    halo = 8
    ext_ref[0:halo, :] = jnp.where(c > 0, prev_ref[...], 0.0)
    ext_ref[halo:halo + CHUNK, :] = cur_ref[...]
    ext_ref[halo + CHUNK:2 * halo + CHUNK, :] = jnp.where(c < nch - 1, next_ref[...], 0.0)
    w = w_ref[...]
    acc = jnp.zeros(cur_ref.shape, F32) + b_ref[...]
    centre = (SSD_CONV - 1) // 2
    for j in range(SSD_CONV):
        acc = acc + w[j:j + 1, :] * ext_ref[pl.ds(halo - centre + j, CHUNK), :]
    y = acc * _sigmoid(acc)
    row = c * CHUNK + lax.broadcasted_iota(I32, y.shape, 0)
    o_ref[...] = jnp.where(row >= PAD, y, 0.0)


def ssd_conv(zxbc, conv_w, conv_b):
    lp = zxbc.shape[0]
    nch = lp // CHUNK
    tc = 512
    col0 = SSD_INNER // tc
    sub = CHUNK // 8
    return pl.pallas_call(
        _ssd_conv_kernel,
        grid=(nch, SSD_CONV_CH // tc),
        in_specs=[
            pl.BlockSpec((CHUNK, tc), lambda c, j: (c, col0 + j)),
            pl.BlockSpec((8, tc), lambda c, j: (jnp.maximum(c * sub - 1, 0), col0 + j)),
            pl.BlockSpec((8, tc), lambda c, j: (jnp.minimum((c + 1) * sub, nch * sub - 1), col0 + j)),
            pl.BlockSpec((SSD_CONV, tc), lambda c, j: (0, j)),
            pl.BlockSpec((1, tc), lambda c, j: (0, j)),
        ],
        out_specs=pl.BlockSpec((CHUNK, tc), lambda c, j: (c, j)),
        out_shape=jax.ShapeDtypeStruct((lp, SSD_CONV_CH), F32),
        scratch_shapes=[pltpu.VMEM((CHUNK + 16, tc), F32)],
        compiler_params=_cparams(("arbitrary", "arbitrary")),
        name="ssd_conv",
    )(zxbc, zxbc, zxbc, conv_w, conv_b.reshape(1, SSD_CONV_CH))


def _ssd_scan_kernel(x_ref, b_ref, c_ref, dt_ref, dtb_ref, acoef_ref, tri_ref, y_ref, state_ref):
    d = pl.program_id(0)
    c = pl.program_id(1)
    nch = pl.num_programs(1)
    ce = jnp.where(d == 0, c, nch - 1 - c)

    @pl.when(c == 0)
    def _():
        state_ref[...] = jnp.zeros(state_ref.shape, F32)

    tri = tri_ref[0]
    trib = tri > 0.5
    row = ce * CHUNK + lax.broadcasted_iota(I32, (CHUNK, CHUNK), 0)
    dt = jnp.where(row >= PAD, _softplus(dt_ref[...] + dtb_ref[0]), 0.0)
    a = dt * acoef_ref[0]
    a_cum = jnp.dot(tri, a, precision=HIGHEST, preferred_element_type=F32)
    a_cum_t = a_cum.T
    a_tot = jnp.sum(a, axis=0, keepdims=True)
    e_cum = jnp.exp(a_cum)
    d_out = jnp.exp(a_tot - a_cum)
    e_tot = jnp.exp(a_tot)
    lane = lax.broadcasted_iota(I32, (CHUNK, CHUNK), 1)
    first = lane < SSD_HEAD_DIM
    first_row = first[0:1, :]

    def pair_cols(m, h0):
        return jnp.where(first, m[:, h0:h0 + 1], m[:, h0 + 1:h0 + 2])

    hpg = SSD_HEADS // SSD_GROUPS
    np_ = SSD_DOT_PASSES
    for g in range(SSD_GROUPS):
        bg = b_ref[:, g * SSD_STATE:(g + 1) * SSD_STATE]
        cg = _parts(c_ref[:, g * SSD_STATE:(g + 1) * SSD_STATE], np_)
        cb = _mm(cg, _parts(bg, np_), NT)
        bt = _parts(bg.T, np_)
        for pp in range(hpg // 2):
            h0 = g * hpg + 2 * pp
            p_idx = h0 // 2
            lo = h0 * SSD_HEAD_DIM
            x_p = x_ref[:, lo:lo + 2 * SSD_HEAD_DIM]
            xdt = x_p * pair_cols(dt, h0)
            xw = _parts(xdt * pair_cols(d_out, h0), np_)
            xdt_b = _parts(xdt, np_)
            yd = []
            for hh in (h0, h0 + 1):
                diff = a_cum[:, hh:hh + 1] - a_cum_t[hh:hh + 1, :]
                dec = jnp.exp(jnp.where(trib, diff, NEG))
                yd.append(_mm(_parts(cb * dec, np_), xdt_b))
            y_diag = jnp.where(first, yd[0], yd[1])
            st_prev = state_ref[p_idx]
            y_off = _mm(cg, _parts(st_prev, np_)) * pair_cols(e_cum, h0)
            y_ref[0, :, lo:lo + 2 * SSD_HEAD_DIM] = y_diag + y_off
            st_new = _mm(bt, xw)
            cdec = jnp.where(first_row, e_tot[:, h0:h0 + 1], e_tot[:, h0 + 1:h0 + 2])
            state_ref[p_idx] = st_prev * cdec + st_new


def ssd_scan(xbc_act, dt_raw, dt_bias_rows, acoef_rows, tri2):
    lp = xbc_act.shape[0]
    nch = lp // CHUNK

    def chunk(d, c):
        return jnp.where(d == 0, c, nch - 1 - c)

    gs = SSD_GROUPS * SSD_STATE
    return pl.pallas_call(
        _ssd_scan_kernel,
        grid=(2, nch),
        in_specs=[
            pl.BlockSpec((CHUNK, SSD_INNER), lambda d, c: (chunk(d, c), 0)),
            pl.BlockSpec((CHUNK, gs), lambda d, c: (chunk(d, c), SSD_INNER // gs)),
            pl.BlockSpec((CHUNK, gs), lambda d, c: (chunk(d, c), SSD_INNER // gs + 1)),
            pl.BlockSpec((CHUNK, CHUNK), lambda d, c: (chunk(d, c), d)),
            pl.BlockSpec((1, 1, CHUNK), lambda d, c: (d, 0, 0)),
            pl.BlockSpec((1, 1, CHUNK), lambda d, c: (d, 0, 0)),
            pl.BlockSpec((1, CHUNK, CHUNK), lambda d, c: (d, 0, 0)),
        ],
        out_specs=pl.BlockSpec((1, CHUNK, SSD_INNER), lambda d, c: (d, chunk(d, c), 0)),
        out_shape=jax.ShapeDtypeStruct((2, lp, SSD_INNER), F32),
        scratch_shapes=[pltpu.VMEM((SSD_HEADS // 2, SSD_STATE, 2 * SSD_HEAD_DIM), F32)],
        compiler_params=_cparams(("arbitrary", "arbitrary")),
        name="ssd_scan",
    )(xbc_act, xbc_act, xbc_act, dt_raw, dt_bias_rows, acoef_rows, tri2)


def _ssd_final_kernel(yf_ref, yb_ref, x_ref, z_ref, dskip_ref, nw_ref, o_ref):
    y = yf_ref[0] + yb_ref[0] + dskip_ref[...] * x_ref[...]
    z = z_ref[...]
    y = y * (z * _sigmoid(z))
    gw = SSD_INNER // SSD_GROUPS
    for g in range(SSD_GROUPS):
        yg = y[:, g * gw:(g + 1) * gw]
        yg = yg * lax.rsqrt(jnp.mean(yg * yg, axis=-1, keepdims=True) + NORM_EPS)
        o_ref[:, g * gw:(g + 1) * gw] = (yg * nw_ref[:, g * gw:(g + 1) * gw]).astype(o_ref.dtype)


def ssd_final(y2, xbc_act, zxbc, d_skip_row, norm_w):
    lp = xbc_act.shape[0]
    return pl.pallas_call(
        _ssd_final_kernel,
        grid=(lp // CHUNK,),
        in_specs=[
            pl.BlockSpec((1, CHUNK, SSD_INNER), lambda c: (0, c, 0)),
            pl.BlockSpec((1, CHUNK, SSD_INNER), lambda c: (1, c, 0)),
            pl.BlockSpec((CHUNK, SSD_INNER), lambda c: (c, 0)),
            pl.BlockSpec((CHUNK, SSD_INNER), lambda c: (c, 0)),
            pl.BlockSpec((1, SSD_INNER), lambda c: (0, 0)),
            pl.BlockSpec((1, SSD_INNER), lambda c: (0, 0)),
        ],
        out_specs=pl.BlockSpec((CHUNK, SSD_INNER), lambda c: (c, 0)),
        out_shape=jax.ShapeDtypeStruct((lp, SSD_INNER), BF16),
        compiler_params=_cparams(("arbitrary",)),
        name="ssd_final",
    )(y2, y2, xbc_act, zxbc, d_skip_row, norm_w.reshape(1, SSD_INNER))


def _attn_kernel(slope_ref, sink_ref, q_ref, km_ref, kp_ref, ks_ref, kn_ref, vm_ref, vp_ref, vs_ref, vn_ref, o_ref):
    hk = pl.program_id(0)
    b = pl.program_id(1)
    nb = pl.num_programs(1)
    scale = ATT_HEAD_DIM ** -0.5
    np_ = ATT_DOT_PASSES
    kcat = _parts(jnp.concatenate([km_ref[...], kp_ref[...], ks_ref[...], kn_ref[...]], axis=0), np_)
    vcat = _parts(jnp.concatenate([vm_ref[...], vp_ref[...], vs_ref[...], vn_ref[...]], axis=0), np_)

    qpos = b * CHUNK - PAD + lax.broadcasted_iota(I32, (CHUNK, 4 * CHUNK), 0)
    col = lax.broadcasted_iota(I32, (CHUNK, 4 * CHUNK), 1)
    slot = col // CHUNK
    j = col - slot * CHUNK
    kblock = jnp.where(slot == 0, 0, b - 2 + slot)
    kpos = kblock * CHUNK - PAD + j
    dist = jnp.abs(qpos - kpos)
    meta_ok = (slot == 0) & (kpos >= 0)
    real_ok = (slot > 0) & (kblock >= 1) & (kblock < nb) & (kpos >= N_META) & (dist <= ATT_WINDOW)
    valid = meta_ok | real_ok
    alibi = jnp.where(real_ok & (qpos >= N_META), dist.astype(F32), 0.0)

    for g in range(ATT_GRP):
        h = hk * ATT_GRP + g
        qg = _parts(q_ref[:, g * ATT_HEAD_DIM:(g + 1) * ATT_HEAD_DIM], np_)
        s = _mm(qg, kcat, NT) * scale
        s = jnp.where(valid, s - slope_ref[h] * alibi, NEG)
        sink = sink_ref[h]
        m = jnp.maximum(jnp.max(s, axis=-1, keepdims=True), sink)
        p = jnp.exp(s - m)
        denom = jnp.sum(p, axis=-1, keepdims=True) + jnp.exp(sink - m)
        o = _mm(_parts(p, np_), vcat) / denom
        o = jnp.where(qpos[:, 0:ATT_HEAD_DIM] >= 0, o, 0.0)
        o_ref[:, g * ATT_HEAD_DIM:(g + 1) * ATT_HEAD_DIM] = o.astype(o_ref.dtype)


def window_attention(qkv, slopes, sink):
    lp = qkv.shape[0]
    nb = lp // CHUNK
    qw = ATT_GRP * ATT_HEAD_DIM
    kcol = ATT_Q_HEADS * ATT_HEAD_DIM // ATT_HEAD_DIM
    vcol = kcol + ATT_KV_HEADS

    def kv_spec(col0, which):
        def imap(hk, b, *_):
            if which == 0:
                blk = 0
            else:
                blk = jnp.clip(b - 2 + which, 0, nb - 1)
            return (blk, col0 + hk)
        return pl.BlockSpec((CHUNK, ATT_HEAD_DIM), imap)

    return pl.pallas_call(
        _attn_kernel,
        grid_spec=pltpu.PrefetchScalarGridSpec(
            num_scalar_prefetch=2,
            grid=(ATT_KV_HEADS, nb),
            in_specs=[pl.BlockSpec((CHUNK, qw), lambda hk, b, *_: (b, hk))]
            + [kv_spec(kcol, w) for w in range(4)] + [kv_spec(vcol, w) for w in range(4)],
            out_specs=pl.BlockSpec((CHUNK, qw), lambda hk, b, *_: (b, hk)),
        ),
        out_shape=jax.ShapeDtypeStruct((lp, ATT_Q_HEADS * ATT_HEAD_DIM), BF16),
        compiler_params=_cparams(("arbitrary", "arbitrary")),
        name="window_attention",
    )(slopes, sink, *([qkv] * 9))


def _mlstm_scan_kernel(q_ref, k_ref, v_ref, g_ref, gb_ref, tri_ref, h_ref, c_ref, n_ref, m_ref):
    d = pl.program_id(0)
    hd = pl.program_id(1)
    c = pl.program_id(2)
    nch = pl.num_programs(2)
    ce = jnp.where(d == 0, c, nch - 1 - c)

    @pl.when(c == 0)
    def _():
        c_ref[...] = jnp.zeros(c_ref.shape, F32)
        n_ref[...] = jnp.zeros(n_ref.shape, F32)
        m_ref[...] = jnp.zeros(m_ref.shape, F32)

    tri = tri_ref[0]
    trib = tri > 0.5
    row = ce * CHUNK + lax.broadcasted_iota(I32, (CHUNK, CHUNK), 0)
    lane = lax.broadcasted_iota(I32, (CHUNK, CHUNK), 1)
    sub = lax.broadcasted_iota(I32, (CHUNK, CHUNK), 0)
    pre = g_ref[...] + gb_ref[0]
    real = row >= PAD
    li_all = jnp.where(real, pre, 0.0)
    lf_all = jnp.where(real, -_softplus(-pre), 0.0)
    bl_all = jnp.dot(tri, lf_all, precision=HIGHEST, preferred_element_type=F32)

    def col_of(mat, ln):
        return jnp.sum(jnp.where(lane == ln, mat, 0.0), axis=1, keepdims=True)

    def row_of(mat_t, ln):
        return jnp.sum(jnp.where(sub == ln, mat_t, 0.0), axis=0, keepdims=True)

    li_col = col_of(li_all, hd)
    li_row = row_of(li_all.T, hd)
    bl_col = col_of(bl_all, hd + ML_F_LANE)
    bl_row = row_of(bl_all.T, hd + ML_F_LANE)
    g_tot = jnp.sum(col_of(lf_all, hd + ML_F_LANE), axis=0, keepdims=True)

    real_col = real[:, 0:1]
    qf = jnp.where(real_col, q_ref[...], 0.0)
    q = qf.astype(BF16)
    kf = jnp.where(real_col, k_ref[...], 0.0) * (ML_QK_DIM ** -0.5)
    kt = kf.T.astype(BF16)
    v = jnp.where(real_col, v_ref[...], 0.0)
    vb = v.astype(BF16)

    m_prev = m_ref[0:1, 0:1]
    n_prev = n_ref[...]
    c_prev = c_ref[...]

    dmat = jnp.where(trib, bl_col - bl_row + li_row, NEG)
    m_inter = bl_col + m_prev
    m_t = jnp.maximum(jnp.max(dmat, axis=1, keepdims=True), m_inter)
    qk = jnp.dot(q, kt, preferred_element_type=F32)
    p = jnp.exp(dmat - m_t) * qk
    w_inter = jnp.exp(m_inter - m_t)
    num = (jnp.dot(p.astype(BF16), vb, preferred_element_type=F32)
           + w_inter * jnp.dot(q, c_prev.astype(BF16), preferred_element_type=F32))
    qn = jnp.sum(qf * n_prev, axis=1, keepdims=True)
    den = jnp.sum(p, axis=1, keepdims=True) + w_inter * qn
    h_ref[0] = num / jnp.maximum(jnp.abs(den), jnp.exp(-m_t))

    a_col = g_tot - bl_col + li_col
    m_loc = jnp.max(a_col, axis=0, keepdims=True)
    wa = jnp.exp(a_col - m_loc)
    c_loc = jnp.dot(kt, (wa * v).astype(BF16), preferred_element_type=F32)
    n_loc = jnp.sum(wa * kf, axis=0, keepdims=True)
    m_new = jnp.maximum(g_tot + m_prev, m_loc)
    s_old = jnp.exp(g_tot + m_prev - m_new)
    s_new = jnp.exp(m_loc - m_new)
    c_ref[...] = s_old * c_prev + s_new * c_loc
    n_ref[...] = s_old * n_prev + s_new * n_loc
    m_ref[...] = jnp.broadcast_to(m_new, m_ref.shape)


def mlstm_scan(qkvo, gates, gate_bias_rows, tri2):
    lp = qkvo.shape[0]
    nch = lp // CHUNK

    def chunk(d, c):
        return jnp.where(d == 0, c, nch - 1 - c)

    kcol = ML_HEADS
    vcol = 2 * ML_HEADS * ML_QK_DIM // ML_V_DIM
    return pl.pallas_call(
        _mlstm_scan_kernel,
        grid=(2, ML_HEADS, nch),
        in_specs=[
            pl.BlockSpec((CHUNK, ML_QK_DIM), lambda d, h, c: (chunk(d, c), h)),
            pl.BlockSpec((CHUNK, ML_QK_DIM), lambda d, h, c: (chunk(d, c), kcol + h)),
            pl.BlockSpec((CHUNK, ML_V_DIM), lambda d, h, c: (chunk(d, c), vcol + h)),
            pl.BlockSpec((CHUNK, CHUNK), lambda d, h, c: (chunk(d, c), d)),
            pl.BlockSpec((1, 1, CHUNK), lambda d, h, c: (d, 0, 0)),
            pl.BlockSpec((1, CHUNK, CHUNK), lambda d, h, c: (d, 0, 0)),
        ],
        out_specs=pl.BlockSpec((1, CHUNK, ML_V_DIM), lambda d, h, c: (d, chunk(d, c), h)),
        out_shape=jax.ShapeDtypeStruct((2, lp, ML_WIDTH), F32),
        scratch_shapes=[pltpu.VMEM((ML_QK_DIM, ML_V_DIM), F32), pltpu.VMEM((1, ML_QK_DIM), F32),
                        pltpu.VMEM((8, CHUNK), F32)],
        compiler_params=_cparams(("arbitrary", "arbitrary", "arbitrary")),
        name="mlstm_scan",
    )(qkvo, qkvo, qkvo, gates, gate_bias_rows, tri2)


def _mlstm_final_kernel(hf_ref, hb_ref, o_ref_in, nw_ref, y_ref):
    h = hf_ref[0] + hb_ref[0]
    o = o_ref_in[...]
    for hd in range(ML_HEADS):
        sl = slice(hd * ML_V_DIM, (hd + 1) * ML_V_DIM)
        hh = h[:, sl]
        hh = hh * lax.rsqrt(jnp.mean(hh * hh, axis=-1, keepdims=True) + NORM_EPS) * nw_ref[:, sl]
        y_ref[:, sl] = (_sigmoid(o[:, sl]) * hh).astype(y_ref.dtype)


def mlstm_final(h2, qkvo, head_norm):
    lp = qkvo.shape[0]
    ocol = (2 * ML_HEADS * ML_QK_DIM + ML_WIDTH) // ML_WIDTH
    return pl.pallas_call(
        _mlstm_final_kernel,
        grid=(lp // CHUNK,),
        in_specs=[
            pl.BlockSpec((1, CHUNK, ML_WIDTH), lambda c: (0, c, 0)),
            pl.BlockSpec((1, CHUNK, ML_WIDTH), lambda c: (1, c, 0)),
            pl.BlockSpec((CHUNK, ML_WIDTH), lambda c: (c, ocol)),
            pl.BlockSpec((1, ML_WIDTH), lambda c: (0, 0)),
        ],
        out_specs=pl.BlockSpec((CHUNK, ML_WIDTH), lambda c: (c, 0)),
        out_shape=jax.ShapeDtypeStruct((lp, ML_WIDTH), BF16),
        compiler_params=_cparams(("arbitrary",)),
        name="mlstm_final",
    )(h2, h2, qkvo, head_norm.reshape(1, ML_WIDTH))


def _dft_factors(n):
    best = None
    for n1 in range(8, n + 1, 8):
        if n % n1 == 0:
            n2 = n // n1
            if best is None or n1 + n2 < best[0] + best[1]:
                best = (n1, n2)
    assert best is not None
    return best


def _fnet_stage1_kernel(x_ref, c2_ref, s2_ref, twc_ref, tws_ref, tr_ref, ti_ref, *, n2):
    x = x_ref[...]
    rows = lax.broadcasted_iota(I32, x.shape, 0)
    x = jnp.where(rows < n2, x, 0.0)
    cx = jnp.dot(c2_ref[...], x, precision=HIGHEST, preferred_element_type=F32)
    sx = jnp.dot(s2_ref[...], x, precision=HIGHEST, preferred_element_type=F32)
    twc = twc_ref[0]
    tws = tws_ref[0]
    tr_ref[...] = cx * twc - sx * tws
    ti_ref[...] = -(cx * tws + sx * twc)


def _fnet_stage2_kernel(tr_ref, ti_ref, c1_ref, s1_ref, cc_ref, sc_ref, o_ref, *, scale):
    tr = tr_ref[0]
    ti = ti_ref[0]
    c1 = c1_ref[...]
    s1 = s1_ref[...]
    yr = (jnp.dot(c1, tr, precision=HIGHEST, preferred_element_type=F32)
          + jnp.dot(s1, ti, precision=HIGHEST, preferred_element_type=F32))
    yi = (jnp.dot(c1, ti, precision=HIGHEST, preferred_element_type=F32)
          - jnp.dot(s1, tr, precision=HIGHEST, preferred_element_type=F32))
    cc = cc_ref[...]
    sc = sc_ref[...]
    for g in range(FN_GROUPS):
        sl = slice(g * FN_GROUP_DIM, (g + 1) * FN_GROUP_DIM)
        out = (jnp.dot(yr[:, sl], cc, precision=HIGHEST, preferred_element_type=F32)
               + jnp.dot(yi[:, sl], sc, precision=HIGHEST, preferred_element_type=F32))
        o_ref[:, sl] = out * scale


def fourier_mixer(u_fn):
    n, ch = u_fn.shape
    n1, n2 = _dft_factors(n)
    n2p = -(-n2 // 8) * 8
    two_pi = 2.0 * np.pi

    def cs(size, pad):
        idx = np.arange(size)
        ang = two_pi * ((idx[:, None] * idx[None, :]) % size) / size
        cm = np.zeros((pad, pad), np.float32)
        sm = np.zeros((pad, pad), np.float32)
        cm[:size, :size] = np.cos(ang)
        sm[:size, :size] = np.sin(ang)
        return jnp.asarray(cm), jnp.asarray(sm)

    c2, s2 = cs(n2, n2p)
    c1, s1 = cs(n1, n1)
    cc, sc = cs(FN_GROUP_DIM, FN_GROUP_DIM)
    tw_ang = two_pi * ((np.arange(n1)[:, None] * np.arange(n2)[None, :]) % n) / n
    twc = np.zeros((n1, n2p, 1), np.float32)
    tws = np.zeros((n1, n2p, 1), np.float32)
    twc[:, :n2, 0] = np.cos(tw_ang)
    tws[:, :n2, 0] = np.sin(tw_ang)

    x2 = u_fn.reshape(n2, n1 * ch)
    tr, ti = pl.pallas_call(
        functools.partial(_fnet_stage1_kernel, n2=n2),
        grid=(n1,),
        in_specs=[
            pl.BlockSpec((n2p, ch), lambda i: (0, i)),
            pl.BlockSpec((n2p, n2p), lambda i: (0, 0)),
            pl.BlockSpec((n2p, n2p), lambda i: (0, 0)),
            pl.BlockSpec((1, n2p, 1), lambda i: (i, 0, 0)),
            pl.BlockSpec((1, n2p, 1), lambda i: (i, 0, 0)),
        ],
        out_specs=[pl.BlockSpec((n2p, ch), lambda i: (0, i)), pl.BlockSpec((n2p, ch), lambda i: (0, i))],
        out_shape=[jax.ShapeDtypeStruct((n2p, n1 * ch), F32), jax.ShapeDtypeStruct((n2p, n1 * ch), F32)],
        compiler_params=_cparams(("arbitrary",)),
        name="fnet_stage1",
    )(x2, c2, s2, jnp.asarray(twc), jnp.asarray(tws))

    tr3 = tr.reshape(n2p, n1, ch)
    ti3 = ti.reshape(n2p, n1, ch)
    scale = 1.0 / math.sqrt(float(n) * FN_GROUP_DIM)
    out = pl.pallas_call(
        functools.partial(_fnet_stage2_kernel, scale=scale),
        grid=(n2,),
        in_specs=[
            pl.BlockSpec((1, n1, ch), lambda i: (i, 0, 0)),
            pl.BlockSpec((1, n1, ch), lambda i: (i, 0, 0)),
            pl.BlockSpec((n1, n1), lambda i: (0, 0)),
            pl.BlockSpec((n1, n1), lambda i: (0, 0)),
            pl.BlockSpec((FN_GROUP_DIM, FN_GROUP_DIM), lambda i: (0, 0)),
            pl.BlockSpec((FN_GROUP_DIM, FN_GROUP_DIM), lambda i: (0, 0)),
        ],
        out_specs=pl.BlockSpec((n1, ch), lambda i: (0, i)),
        out_shape=jax.ShapeDtypeStruct((n1, n2 * ch), F32),
        compiler_params=_cparams(("arbitrary",)),
        name="fnet_stage2",
    )(tr3, ti3, c1, s1, cc, sc)
    return out.reshape(n, ch)


def _cap_sizes(lp):
    n_tok = lp - PAD
    cap = EC_FACTOR * n_tok // N_EXPERTS
    capp = -(-cap // 16) * 16
    return cap, capp


def _topk_kernel(lg_ref, su_ref, pos_ref, aff_ref, start_ref, idx_ref, rank_scr, pos_scr, *, cap, capp, jb):
    ne, lp = lg_ref.shape
    nch = lp // CHUNK
    lg = lg_ref[...]
    mx = jnp.max(lg, axis=0, keepdims=True)
    ex = jnp.exp(lg - mx)
    aff = ex / jnp.sum(ex, axis=0, keepdims=True)
    lane = lax.broadcasted_iota(I32, (ne, lp), 1)
    real = lane >= PAD
    bits = jnp.where(real, pltpu.bitcast(aff, I32), -1)
    capf = float(cap)

    def bisect(i, thr):
        cand = thr | jnp.left_shift(jnp.int32(1), 30 - i)
        cnt = jnp.sum(jnp.where(bits >= cand, 1.0, 0.0), axis=1, keepdims=True)
        return jnp.where(cnt >= capf, cand, thr)

    thr = lax.fori_loop(0, 31, bisect, jnp.zeros((ne, 1), I32))
    gt = bits > thr
    eq = bits == thr
    need = capf - jnp.sum(jnp.where(gt, 1.0, 0.0), axis=1, keepdims=True)

    su = su_ref[...]

    def excl_cumsum(mask_f32, out_scr, want_starts):
        carry = jnp.zeros((ne, 1), F32)
        starts = jnp.zeros((ne, CHUNK), F32)
        lane_c = lax.broadcasted_iota(I32, (ne, CHUNK), 1)
        for c in range(nch):
            tile = mask_f32[:, c * CHUNK:(c + 1) * CHUNK]
            within = jnp.dot(tile.astype(BF16), su, preferred_element_type=F32)
            out_scr[:, c * CHUNK:(c + 1) * CHUNK] = within + carry
            if want_starts:
                starts = jnp.where(lane_c == c, carry, starts)
            carry = carry + jnp.sum(tile, axis=1, keepdims=True)
        if want_starts:
            starts = jnp.where(lane_c == nch, carry, starts)
        return starts

    excl_cumsum(jnp.where(eq, 1.0, 0.0), rank_scr, False)
    sel = gt | (eq & (rank_scr[...] < need))
    starts = excl_cumsum(jnp.where(sel, 1.0, 0.0), pos_scr, True)
    posf = jnp.where(sel, pos_scr[...], -1.0)
    pos_scr[...] = posf
    pos_ref[...] = posf.astype(I32)
    aff_ref[...] = jnp.where(sel, aff, 0.0)
    start_ref[...] = starts.astype(I32)

    tok = lax.broadcasted_iota(I32, (jb, lp), 1).astype(F32)
    lane_e = lax.broadcasted_iota(I32, (jb, ne), 1)
    for blk in range(capp // jb):
        jcol = (blk * jb + lax.broadcasted_iota(I32, (jb, 1), 0)).astype(F32)

        def per_expert(e, acc):
            prow = pos_scr[pl.ds(e, 1), :]
            hit = jnp.sum(jnp.where(prow == jcol, tok, 0.0), axis=1, keepdims=True)
            return jnp.where(lane_e == e, hit, acc)

        acc = lax.fori_loop(0, ne, per_expert, jnp.zeros((jb, ne), F32))
        idx_ref[blk * jb:(blk + 1) * jb, :] = acc.astype(I32)


def topk_route(logits_t):
    ne, lp = logits_t.shape
    cap, capp = _cap_sizes(lp)
    jb = 80 if capp % 80 == 0 else 16
    su = jnp.asarray(np.triu(np.ones((CHUNK, CHUNK), np.float32), 1), BF16)
    return pl.pallas_call(
        functools.partial(_topk_kernel, cap=cap, capp=capp, jb=jb),
        out_shape=[jax.ShapeDtypeStruct((ne, lp), I32), jax.ShapeDtypeStruct((ne, lp), F32),
                   jax.ShapeDtypeStruct((ne, CHUNK), I32), jax.ShapeDtypeStruct((capp, ne), I32)],
        scratch_shapes=[pltpu.VMEM((ne, lp), F32), pltpu.VMEM((ne, lp), F32)],
        compiler_params=pltpu.CompilerParams(vmem_limit_bytes=VMEM_LIMIT),
        name="topk_route",
    )(logits_t, su)


def _gather_kernel(idx_ref, u_hbm, o_ref, stage_ref, sem):
    e = pl.program_id(0)
    capp = stage_ref.shape[0]

    def row_copy(j):
        t = idx_ref[e, j]
        return pltpu.make_async_copy(u_hbm.at[pl.ds(t, 1), :], stage_ref.at[pl.ds(j, 1), :], sem)

    def issue(j, carry):
        row_copy(j).start()
        return carry

    def drain(j, carry):
        row_copy(j).wait()
        return carry

    lax.fori_loop(0, capp, issue, 0)
    lax.fori_loop(0, capp, drain, 0)
    o_ref[0] = stage_ref[...].astype(o_ref.dtype)


def gather_tokens(idx, u):
    ne, capp = idx.shape
    lp, d = u.shape
    return pl.pallas_call(
        _gather_kernel,
        grid_spec=pltpu.PrefetchScalarGridSpec(
            num_scalar_prefetch=1,
            grid=(ne,),
            in_specs=[pl.BlockSpec(memory_space=pl.ANY)],
            out_specs=pl.BlockSpec((1, capp, d), lambda e, *_: (e, 0, 0)),
            scratch_shapes=[pltpu.VMEM((capp, d), F32), pltpu.SemaphoreType.DMA],
        ),
        out_shape=jax.ShapeDtypeStruct((ne, capp, d), BF16),
        compiler_params=_cparams(("arbitrary",)),
        name="moe_gather",
    )(idx, u)


def _ffn_kernel(xs_ref, wg_ref, wu_ref, wd_ref, o_ref, hdn_ref, *, n_ff):
    s = pl.program_id(1)

    @pl.when(s < n_ff)
    def _():
        x = xs_ref[0]
        g = jnp.dot(x, wg_ref[0, 0].astype(BF16), preferred_element_type=F32)
        u = jnp.dot(x, wu_ref[0, 0].astype(BF16), preferred_element_type=F32)
        hdn_ref[s] = (g * _sigmoid(g) * u).astype(BF16)

    @pl.when(s >= n_ff)
    def _():
        wd = wd_ref[0, 0].astype(BF16)
        acc = None
        for j in range(n_ff):
            part = jnp.dot(hdn_ref[j], wd[j * FF_TILE:(j + 1) * FF_TILE, :], preferred_element_type=F32)
            acc = part if acc is None else acc + part
        o_ref[0] = acc


def expert_ffn(xs, w_gate, w_up, w_down, layer):
    ne, capp, d = xs.shape
    n_ff = EXPERT_FF // FF_TILE
    n_out = d // OUT_TILE
    return pl.pallas_call(
        functools.partial(_ffn_kernel, n_ff=n_ff),
        grid=(ne, n_ff + n_out),
        in_specs=[
            pl.BlockSpec((1, capp, d), lambda e, s: (e, 0, 0)),
            pl.BlockSpec((1, 1, d, FF_TILE), lambda e, s: (layer, e, 0, jnp.minimum(s, n_ff - 1))),
            pl.BlockSpec((1, 1, d, FF_TILE), lambda e, s: (layer, e, 0, jnp.minimum(s, n_ff - 1))),
            pl.BlockSpec((1, 1, EXPERT_FF, OUT_TILE), lambda e, s: (layer, e, 0, jnp.maximum(s - n_ff, 0))),
        ],
        out_specs=pl.BlockSpec((1, capp, OUT_TILE), lambda e, s: (e, 0, jnp.maximum(s - n_ff, 0))),
        out_shape=jax.ShapeDtypeStruct((ne, capp, d), F32),
        scratch_shapes=[pltpu.VMEM((n_ff, capp, FF_TILE), BF16)],
        compiler_params=_cparams(("arbitrary", "arbitrary")),
        name="expert_ffn",
    )(xs, w_gate, w_up, w_down)


def _combine_kernel(start_ref, h_ref, pos_ref, aff_ref, out_hbm, o_ref, g_ref, sem, *, capp, rows):
    b = pl.program_id(0)
    ne = pos_ref.shape[0]

    @pl.when(b == 0)
    def _():
        g_ref[...] = jnp.zeros(g_ref.shape, F32)

    def row_copy(src, dst):
        return pltpu.make_async_copy(out_hbm.at[pl.ds(src, 1), :], g_ref.at[pl.ds(dst, 1), :], sem)

    offs = []
    starts = []
    off = jnp.int32(0)
    for e in range(ne):
        s0 = start_ref[e, b]
        n = start_ref[e, b + 1] - s0
        offs.append(off)
        starts.append(s0)

        def issue(j, carry, e=e, s0=s0, off=off):
            row_copy(e * capp + s0 + j, off + j).start()
            return carry

        lax.fori_loop(0, n, issue, 0)
        off = off + n
    total = off

    def drain(j, carry):
        row_copy(0, j).wait()
        return carry

    lax.fori_loop(0, total, drain, 0)

    pos = pos_ref[...]
    aff = aff_ref[...]
    r_iota = lax.broadcasted_iota(I32, (rows, CHUNK), 0)

    def chunk(ci, acc):
        r0 = pl.multiple_of(ci * rows, rows)
        rr = r_iota + r0
        onehot = jnp.zeros((rows, CHUNK), F32)
        gate = jnp.zeros((rows, CHUNK), F32)
        for e in range(ne):
            tgt = jnp.where(pos[e:e + 1, :] >= 0, pos[e:e + 1, :] - starts[e] + offs[e], -1)
            hit = rr == tgt
            onehot = jnp.where(hit, 1.0, onehot)
            gate = jnp.where(hit, aff[e:e + 1, :], gate)
        gcol = jnp.sum(gate, axis=1, keepdims=True)
        gs = g_ref[pl.ds(r0, rows), :] * gcol
        hi = gs.astype(BF16)
        lo = (gs - hi.astype(F32)).astype(BF16)
        oh = onehot.T.astype(BF16)
        return (acc + jnp.dot(oh, hi, preferred_element_type=F32)
                + jnp.dot(oh, lo, preferred_element_type=F32))

    n_chunks = (total + rows - 1) // rows
    o_ref[...] = lax.fori_loop(0, n_chunks, chunk, h_ref[...])


def moe_combine(h, pos, aff, starts, ffn_out):
    lp, d = h.shape
    ne, capp, _ = ffn_out.shape
    rows = 256
    stage_rows = ne * CHUNK
    return pl.pallas_call(
        functools.partial(_combine_kernel, capp=capp, rows=rows),
        grid_spec=pltpu.PrefetchScalarGridSpec(
            num_scalar_prefetch=1,
            grid=(lp // CHUNK,),
            in_specs=[
                pl.BlockSpec((CHUNK, d), lambda b, *_: (b, 0)),
                pl.BlockSpec((ne, CHUNK), lambda b, *_: (0, b)),
                pl.BlockSpec((ne, CHUNK), lambda b, *_: (0, b)),
                pl.BlockSpec(memory_space=pl.ANY),
            ],
            out_specs=pl.BlockSpec((CHUNK, d), lambda b, *_: (b, 0)),
            scratch_shapes=[pltpu.VMEM((stage_rows, d), F32), pltpu.SemaphoreType.DMA],
        ),
        out_shape=jax.ShapeDtypeStruct((lp, d), F32),
        compiler_params=_cparams(("arbitrary",)),
        name="moe_combine",
    )(starts, h, pos, aff, ffn_out.reshape(ne * capp, d))


def moe_layer(h, norm_w, w_router, w_gate, w_up, w_down, layer):
    u, logits_t = rmsnorm_router(h, norm_w, w_router)
    pos, aff, starts, idx_t = topk_route(logits_t)
    xs = gather_tokens(idx_t.T, u)
    ffn_out = expert_ffn(xs, w_gate, w_up, w_down, layer)
    return moe_combine(h, pos, aff, starts, ffn_out)


def _scan_masks():
    lower = np.tril(np.ones((CHUNK, CHUNK), np.float32))
    return jnp.asarray(np.stack([lower, lower.T]))


def _pad_cols(w, width):
    return jnp.pad(w, ((0, 0), (0, width - w.shape[1])))


def ab_mixer(h, norm_w, w_in, conv_w, conv_b, a_log, dt_bias, d_skip, ssd_norm, sink, w_out):
    u = rmsnorm(h, norm_w, BF16)
    zx_w = SSD_INNER + SSD_CONV_CH
    dt0 = zx_w
    qkv0 = zx_w + 2 * SSD_HEADS
    zxbc = matmul([u], w_in[:, :zx_w], 512, w_passes=AB_W_PASSES)
    w_dt = jnp.concatenate([_pad_cols(w_in[:, dt0:dt0 + SSD_HEADS], CHUNK),
                            _pad_cols(w_in[:, dt0 + SSD_HEADS:qkv0], CHUNK)], axis=1)
    dt_raw = matmul([u], w_dt, 2 * CHUNK, w_passes=AB_W_PASSES)
    qkv = matmul([u], w_in[:, qkv0:], 512, w_passes=AB_W_PASSES)

    xbc_act = ssd_conv(zxbc, conv_w, conv_b)
    dt_bias_rows = _pad_cols(dt_bias, CHUNK).reshape(2, 1, CHUNK)
    acoef_rows = _pad_cols(-jnp.exp(a_log), CHUNK).reshape(2, 1, CHUNK)
    y2 = ssd_scan(xbc_act, dt_raw, dt_bias_rows, acoef_rows, _scan_masks())
    d_skip_row = jnp.repeat(d_skip, SSD_HEAD_DIM).reshape(1, SSD_INNER)
    y_ssd = ssd_final(y2, xbc_act, zxbc, d_skip_row, ssd_norm)

    slopes = 2.0 ** (-8.0 * jnp.arange(1, ATT_Q_HEADS + 1, dtype=F32) / ATT_Q_HEADS)
    y_att = window_attention(qkv, slopes, sink)
    return matmul([y_ssd, y_att], w_out, 512, res=h, w_passes=AB_W_PASSES)


def cd_mixer(h, norm_w, w_in, i_bias, f_bias, head_norm, w_out):
    lp = h.shape[0]
    u = rmsnorm(h, norm_w, BF16)
    qkvo_w = 2 * ML_HEADS * ML_QK_DIM + 2 * ML_WIDTH
    i0 = qkvo_w
    f0 = i0 + 2 * ML_HEADS
    fn0 = f0 + 2 * ML_HEADS
    qkvo = matmul([u], w_in[:, :qkvo_w], 512)

    def gate_block(dirn):
        wi = w_in[:, i0 + dirn * ML_HEADS:i0 + (dirn + 1) * ML_HEADS]
        wf = w_in[:, f0 + dirn * ML_HEADS:f0 + (dirn + 1) * ML_HEADS]
        return _pad_cols(jnp.concatenate([_pad_cols(wi, ML_F_LANE), wf], axis=1), CHUNK)

    def bias_block(dirn):
        row = jnp.concatenate([jnp.pad(i_bias[dirn], (0, ML_F_LANE - ML_HEADS)), f_bias[dirn]])
        return jnp.pad(row, (0, CHUNK - row.shape[0]))

    gates = matmul([u], jnp.concatenate([gate_block(0), gate_block(1)], axis=1), 2 * CHUNK)
    gate_bias_rows = jnp.stack([bias_block(0), bias_block(1)]).reshape(2, 1, CHUNK)
    h2 = mlstm_scan(qkvo, gates, gate_bias_rows, _scan_masks())
    y_ml = mlstm_final(h2, qkvo, head_norm)

    u_fn = matmul([u], w_in[:, fn0:], 512)
    y_fn = fourier_mixer(u_fn[PAD:])
    y_fn = jnp.pad(y_fn, ((PAD, 0), (0, 0))).astype(BF16)
    return matmul([y_ml, y_fn], w_out, 512, res=h)


def kernel(x, meta_tokens, norm_mix, ab_w_in, ab_conv_w, ab_conv_b, ab_a_log, ab_dt_bias, ab_d_skip, ab_ssd_norm,
           ab_sink, ab_w_out, cd_w_in, cd_i_bias, cd_f_bias, cd_head_norm, cd_w_out, norm_ffn, moe_router,
           moe_w_gate, moe_w_up, moe_w_down, final_norm):
    bsz, seq, d = x.shape
    assert bsz == 1 and d == D_MODEL and seq % CHUNK == 0
    depth = norm_mix.shape[0]
    h = jnp.concatenate([jnp.zeros((PAD, d), F32), meta_tokens.astype(F32), x[0]], axis=0)
    for layer in range(depth):
        j = layer // 2
        if layer % 2 == 0:
            h = ab_mixer(h, norm_mix[layer], ab_w_in[j], ab_conv_w[j], ab_conv_b[j], ab_a_log[j], ab_dt_bias[j],
                         ab_d_skip[j], ab_ssd_norm[j], ab_sink[j], ab_w_out[j])
        else:
            h = cd_mixer(h, norm_mix[layer], cd_w_in[j], cd_i_bias[j], cd_f_bias[j], cd_head_norm[j], cd_w_out[j])
        h = moe_layer(h, norm_ffn[layer], moe_router[layer], moe_w_gate, moe_w_up, moe_w_down, layer)
    return final_rmsnorm(h, final_norm)[None]
```

```python
import functools
import math

import numpy as np
import jax
import jax.numpy as jnp
from jax import lax
from jax.experimental import pallas as pl
from jax.experimental.pallas import tpu as pltpu

F32 = jnp.float32
BF16 = jnp.bfloat16
I32 = jnp.int32
HIGHEST = lax.Precision.HIGHEST

D_MODEL = 4096
N_META = 16
CHUNK = 128
PAD = CHUNK - N_META
NORM_EPS = 1e-6
NEG = -1e30

SSD_HEADS = 32
SSD_HEAD_DIM = 64
SSD_INNER = SSD_HEADS * SSD_HEAD_DIM
SSD_GROUPS = 4
SSD_STATE = 128
SSD_CONV = 5
SSD_CONV_CH = SSD_INNER + 2 * SSD_GROUPS * SSD_STATE

ATT_Q_HEADS = 16
ATT_KV_HEADS = 4
ATT_HEAD_DIM = 128
ATT_WINDOW = 128
ATT_GRP = ATT_Q_HEADS // ATT_KV_HEADS

ML_HEADS = 6
ML_QK_DIM = 256
ML_V_DIM = 512
ML_WIDTH = ML_HEADS * ML_V_DIM
ML_F_LANE = 8

FN_GROUPS = 4
FN_GROUP_DIM = 256
FN_WIDTH = FN_GROUPS * FN_GROUP_DIM

N_EXPERTS = 16
EC_FACTOR = 2
EXPERT_FF = 1536
FF_TILE = 256
OUT_TILE = 512

VMEM_LIMIT = 56 * 1024 * 1024


def _cparams(sem):
    return pltpu.CompilerParams(dimension_semantics=sem, vmem_limit_bytes=VMEM_LIMIT)


def _row_tile(lp):
    return 640 if lp % 640 == 0 else CHUNK


def _sigmoid(x):
    return 1.0 / (1.0 + jnp.exp(-x))


def _softplus(x):
    return jnp.maximum(x, 0.0) + jnp.log1p(jnp.exp(-jnp.abs(x)))


NT = (((1,), (1,)), ((), ()))


def _bdot(a, b, dims=None):
    if dims is None:
        return jnp.dot(a, b, preferred_element_type=F32)
    return lax.dot_general(a, b, dims, preferred_element_type=F32)


def _rmsnorm_kernel(h_ref, w_ref, o_ref):
    x = h_ref[...]
    y = x * lax.rsqrt(jnp.mean(x * x, axis=-1, keepdims=True) + NORM_EPS) * w_ref[...]
    o_ref[...] = y.astype(o_ref.dtype)


def rmsnorm(h, w, out_dtype):
    lp, d = h.shape
    tm = _row_tile(lp)
    return pl.pallas_call(
        _rmsnorm_kernel,
        grid=(lp // tm,),
        in_specs=[pl.BlockSpec((tm, d), lambda i: (i, 0)), pl.BlockSpec((1, d), lambda i: (0, 0))],
        out_specs=pl.BlockSpec((tm, d), lambda i: (i, 0)),
        out_shape=jax.ShapeDtypeStruct((lp, d), out_dtype),
        compiler_params=_cparams(("arbitrary",)),
        name="rmsnorm",
    )(h, w.reshape(1, d))


def _rmsnorm_router_kernel(h_ref, w_ref, wrt_ref, u_ref, lg_ref):
    x = h_ref[...]
    y = x * lax.rsqrt(jnp.mean(x * x, axis=-1, keepdims=True) + NORM_EPS) * w_ref[...]
    u_ref[...] = y
    lg_ref[...] = lax.dot_general(wrt_ref[...], y, (((1,), (1,)), ((), ())), precision=HIGHEST,
                                  preferred_element_type=F32)


def rmsnorm_router(h, w, w_router):
    lp, d = h.shape
    tm = _row_tile(lp)
    return pl.pallas_call(
        _rmsnorm_router_kernel,
        grid=(lp // tm,),
        in_specs=[pl.BlockSpec((tm, d), lambda i: (i, 0)), pl.BlockSpec((1, d), lambda i: (0, 0)),
                  pl.BlockSpec((N_EXPERTS, d), lambda i: (0, 0))],
        out_specs=[pl.BlockSpec((tm, d), lambda i: (i, 0)), pl.BlockSpec((N_EXPERTS, tm), lambda i: (0, i))],
        out_shape=[jax.ShapeDtypeStruct((lp, d), F32), jax.ShapeDtypeStruct((N_EXPERTS, lp), F32)],
        compiler_params=_cparams(("arbitrary",)),
        name="rmsnorm_router",
    )(h, w.reshape(1, d), w_router.T)


def _final_norm_kernel(h_ref, w_ref, o_ref):
    x = h_ref[...]
    o_ref[...] = x * lax.rsqrt(jnp.mean(x * x, axis=-1, keepdims=True) + NORM_EPS) * w_ref[...]


def final_rmsnorm(h, w):
    lp, d = h.shape
    n = lp - CHUNK
    return pl.pallas_call(
        _final_norm_kernel,
        grid=(n // CHUNK,),
        in_specs=[pl.BlockSpec((CHUNK, d), lambda i: (i + 1, 0)), pl.BlockSpec((1, d), lambda i: (0, 0))],
        out_specs=pl.BlockSpec((CHUNK, d), lambda i: (i, 0)),
        out_shape=jax.ShapeDtypeStruct((n, d), F32),
        compiler_params=_cparams(("arbitrary",)),
        name="final_norm",
    )(h, w.reshape(1, d))


def _matmul_kernel(*refs, k_splits, has_res):
    n_x = len(k_splits)
    x_refs = refs[:n_x]
    w_ref = refs[n_x]
    r_ref = refs[n_x + 1] if has_res else None
    o_ref = refs[n_x + 1 + int(has_res)]
    wb_ref = refs[n_x + 2 + int(has_res)]

    @pl.when(pl.program_id(1) == 0)
    def _():
        wb_ref[...] = w_ref[...].astype(BF16)

    acc = None
    k0 = 0
    for x_ref, kk in zip(x_refs, k_splits):
        part = jnp.dot(x_ref[...], wb_ref[k0:k0 + kk, :], preferred_element_type=F32)
        acc = part if acc is None else acc + part
        k0 += kk
    if has_res:
        acc = acc + r_ref[...]
    o_ref[...] = acc.astype(o_ref.dtype)


def matmul(xs, w, tn, res=None, out_dtype=F32, col0=0, n=None):
    lp = xs[0].shape[0]
    k_splits = tuple(int(x.shape[1]) for x in xs)
    k = w.shape[0]
    n = w.shape[1] - col0 if n is None else n
    assert sum(k_splits) == k and n % tn == 0 and col0 % tn == 0
    cb0 = col0 // tn
    tm = _row_tile(lp)
    in_specs = [pl.BlockSpec((tm, kk), lambda j, i: (i, 0)) for kk in k_splits]
    in_specs.append(pl.BlockSpec((k, tn), lambda j, i: (0, cb0 + j)))
    args = list(xs) + [w]
    if res is not None:
        in_specs.append(pl.BlockSpec((tm, tn), lambda j, i: (i, j)))
        args.append(res)
    return pl.pallas_call(
        functools.partial(_matmul_kernel, k_splits=k_splits, has_res=res is not None),
        grid=(n // tn, lp // tm),
        in_specs=in_specs,
        out_specs=pl.BlockSpec((tm, tn), lambda j, i: (i, j)),
        out_shape=jax.ShapeDtypeStruct((lp, n), out_dtype),
        scratch_shapes=[pltpu.VMEM((k, tn), BF16)],
        compiler_params=_cparams(("arbitrary", "arbitrary")),
        name="matmul",
    )(*args)


def _ssd_conv_kernel(cur_ref, prev_ref, next_ref, w_ref, b_ref, o_ref, ext_ref):
    c = pl.program_id(0)
    nch = pl.num_programs(0)
    halo = 8
    ext_ref[0:halo, :] = jnp.where(c > 0, prev_ref[...], 0.0)
    ext_ref[halo:halo + CHUNK, :] = cur_ref[...]
    ext_ref[halo + CHUNK:2 * halo + CHUNK, :] = jnp.where(c < nch - 1, next_ref[...], 0.0)
    w = w_ref[...]
    acc = jnp.zeros(cur_ref.shape, F32) + b_ref[...]
    centre = (SSD_CONV - 1) // 2
    for j in range(SSD_CONV):
        acc = acc + w[j:j + 1, :] * ext_ref[pl.ds(halo - centre + j, CHUNK), :]
    y = acc * _sigmoid(acc)
    row = c * CHUNK + lax.broadcasted_iota(I32, y.shape, 0)
    o_ref[...] = jnp.where(row >= PAD, y, 0.0)


def ssd_conv(xbc, conv_w, conv_b):
    lp, ch = xbc.shape
    nch = lp // CHUNK
    sub = CHUNK // 8
    return pl.pallas_call(
        _ssd_conv_kernel,
        grid=(nch,),
        in_specs=[
            pl.BlockSpec((CHUNK, ch), lambda c: (c, 0)),
            pl.BlockSpec((8, ch), lambda c: (jnp.maximum(c * sub - 1, 0), 0)),
            pl.BlockSpec((8, ch), lambda c: (jnp.minimum((c + 1) * sub, nch * sub - 1), 0)),
            pl.BlockSpec((SSD_CONV, ch), lambda c: (0, 0)),
            pl.BlockSpec((1, ch), lambda c: (0, 0)),
        ],
        out_specs=pl.BlockSpec((CHUNK, ch), lambda c: (c, 0)),
        out_shape=jax.ShapeDtypeStruct((lp, ch), F32),
        scratch_shapes=[pltpu.VMEM((CHUNK + 16, ch), F32)],
        compiler_params=_cparams(("arbitrary",)),
        name="ssd_conv",
    )(xbc, xbc, xbc, conv_w, conv_b.reshape(1, ch))


def _ssd_direction(d, ce, x_ref, b_ref, c_ref, dt_ref, dtb_ref, acoef_ref, tri_ref, y_ref, state_ref):
    tri = tri_ref[d]
    trib = tri > 0.5
    row = ce * CHUNK + lax.broadcasted_iota(I32, (CHUNK, CHUNK), 0)
    dt = jnp.where(row >= PAD, _softplus(dt_ref[...] + dtb_ref[d]), 0.0)
    a = dt * acoef_ref[d]
    a_cum = jnp.dot(tri, a, precision=HIGHEST, preferred_element_type=F32)
    a_cum_t = a_cum.T
    a_tot = jnp.sum(a, axis=0, keepdims=True)
    e_cum = jnp.exp(a_cum)
    d_out = jnp.exp(a_tot - a_cum)
    e_tot = jnp.exp(a_tot)
    lane = lax.broadcasted_iota(I32, (CHUNK, CHUNK), 1)
    first = lane < SSD_HEAD_DIM
    first_row = first[0:1, :]

    def pair_cols(m, h0):
        return jnp.where(first, m[:, h0:h0 + 1], m[:, h0 + 1:h0 + 2])

    hpg = SSD_HEADS // SSD_GROUPS
    for g in range(SSD_GROUPS):
        bg = b_ref[:, g * SSD_STATE:(g + 1) * SSD_STATE]
        cg = c_ref[:, g * SSD_STATE:(g + 1) * SSD_STATE].astype(BF16)
        cb = _bdot(cg, bg.astype(BF16), NT)
        bt = bg.T.astype(BF16)
        for pp in range(hpg // 2):
            h0 = g * hpg + 2 * pp
            p_idx = d * (SSD_HEADS // 2) + h0 // 2
            lo = h0 * SSD_HEAD_DIM
            x_p = x_ref[:, lo:lo + 2 * SSD_HEAD_DIM]
            xdt = x_p * pair_cols(dt, h0)
            xw = (xdt * pair_cols(d_out, h0)).astype(BF16)
            xdt_b = xdt.astype(BF16)
            yd = []
            for hh in (h0, h0 + 1):
                diff = a_cum[:, hh:hh + 1] - a_cum_t[hh:hh + 1, :]
                dec = jnp.exp(jnp.where(trib, diff, NEG))
                yd.append(_bdot((cb * dec).astype(BF16), xdt_b))
            y_diag = jnp.where(first, yd[0], yd[1])
            st_prev = state_ref[p_idx]
            y_off = _bdot(cg, st_prev.astype(BF16)) * pair_cols(e_cum, h0)
            y_ref[:, lo:lo + 2 * SSD_HEAD_DIM] = y_diag + y_off
            st_new = _bdot(bt, xw)
            cdec = jnp.where(first_row, e_tot[:, h0:h0 + 1], e_tot[:, h0 + 1:h0 + 2])
            state_ref[p_idx] = st_prev * cdec + st_new


def _ssd_scan_kernel(xf_ref, bf_ref, cf_ref, dtf_ref, xb_ref, bb_ref, cb_ref, dtbk_ref, dtb_ref, acoef_ref, tri_ref,
                     yf_ref, yb_ref, state_ref):
    c = pl.program_id(0)
    nch = pl.num_programs(0)

    @pl.when(c == 0)
    def _():
        state_ref[...] = jnp.zeros(state_ref.shape, F32)

    _ssd_direction(0, c, xf_ref, bf_ref, cf_ref, dtf_ref, dtb_ref, acoef_ref, tri_ref, yf_ref, state_ref)
    _ssd_direction(1, nch - 1 - c, xb_ref, bb_ref, cb_ref, dtbk_ref, dtb_ref, acoef_ref, tri_ref, yb_ref, state_ref)


def ssd_scan(xbc_act, dt_raw, dt_bias_rows, acoef_rows, tri2):
    lp = xbc_act.shape[0]
    nch = lp // CHUNK
    gs = SSD_GROUPS * SSD_STATE

    def specs(chunk, d):
        return [
            pl.BlockSpec((CHUNK, SSD_INNER), lambda c: (chunk(c), 0)),
            pl.BlockSpec((CHUNK, gs), lambda c: (chunk(c), SSD_INNER // gs)),
            pl.BlockSpec((CHUNK, gs), lambda c: (chunk(c), SSD_INNER // gs + 1)),
            pl.BlockSpec((CHUNK, CHUNK), lambda c: (chunk(c), d)),
        ]

    fwd = lambda c: c
    bwd = lambda c: nch - 1 - c
    return pl.pallas_call(
        _ssd_scan_kernel,
        grid=(nch,),
        in_specs=specs(fwd, 0) + specs(bwd, 1) + [
            pl.BlockSpec((2, 1, CHUNK), lambda c: (0, 0, 0)),
            pl.BlockSpec((2, 1, CHUNK), lambda c: (0, 0, 0)),
            pl.BlockSpec((2, CHUNK, CHUNK), lambda c: (0, 0, 0)),
        ],
        out_specs=[pl.BlockSpec((CHUNK, SSD_INNER), lambda c: (c, 0)),
                   pl.BlockSpec((CHUNK, SSD_INNER), lambda c: (nch - 1 - c, 0))],
        out_shape=[jax.ShapeDtypeStruct((lp, SSD_INNER), F32), jax.ShapeDtypeStruct((lp, SSD_INNER), F32)],
        scratch_shapes=[pltpu.VMEM((SSD_HEADS, SSD_STATE, 2 * SSD_HEAD_DIM), F32)],
        compiler_params=_cparams(("arbitrary",)),
        name="ssd_scan",
    )(xbc_act, xbc_act, xbc_act, dt_raw, xbc_act, xbc_act, xbc_act, dt_raw, dt_bias_rows, acoef_rows, tri2)


def _ssd_final_kernel(yf_ref, yb_ref, x_ref, z_ref, dskip_ref, nw_ref, o_ref):
    y = yf_ref[...] + yb_ref[...] + dskip_ref[...] * x_ref[...]
    z = z_ref[...]
    y = y * (z * _sigmoid(z))
    gw = SSD_INNER // SSD_GROUPS
    for g in range(SSD_GROUPS):
        yg = y[:, g * gw:(g + 1) * gw]
        yg = yg * lax.rsqrt(jnp.mean(yg * yg, axis=-1, keepdims=True) + NORM_EPS)
        o_ref[:, g * gw:(g + 1) * gw] = (yg * nw_ref[:, g * gw:(g + 1) * gw]).astype(o_ref.dtype)


def ssd_final(yf, yb, xbc_act, z, d_skip_row, norm_w):
    lp = xbc_act.shape[0]
    blk = pl.BlockSpec((CHUNK, SSD_INNER), lambda c: (c, 0))
    row = pl.BlockSpec((1, SSD_INNER), lambda c: (0, 0))
    return pl.pallas_call(
        _ssd_final_kernel,
        grid=(lp // CHUNK,),
        in_specs=[blk, blk, blk, blk, row, row],
        out_specs=blk,
        out_shape=jax.ShapeDtypeStruct((lp, SSD_INNER), BF16),
        compiler_params=_cparams(("arbitrary",)),
        name="ssd_final",
    )(yf, yb, xbc_act, z, d_skip_row, norm_w.reshape(1, SSD_INNER))


def _attn_kernel(slope_ref, sink_ref, q_ref, km_ref, kp_ref, ks_ref, kn_ref, vm_ref, vp_ref, vs_ref, vn_ref, o_ref):
    hk = pl.program_id(0)
    b = pl.program_id(1)
    nb = pl.num_programs(1)
    scale = ATT_HEAD_DIM ** -0.5
    kcat = jnp.concatenate([km_ref[...], kp_ref[...], ks_ref[...], kn_ref[...]], axis=0).astype(BF16)
    vcat = jnp.concatenate([vm_ref[...], vp_ref[...], vs_ref[...], vn_ref[...]], axis=0).astype(BF16)

    qpos = b * CHUNK - PAD + lax.broadcasted_iota(I32, (CHUNK, 4 * CHUNK), 0)
    col = lax.broadcasted_iota(I32, (CHUNK, 4 * CHUNK), 1)
    slot = col // CHUNK
    j = col - slot * CHUNK
    kblock = jnp.where(slot == 0, 0, b - 2 + slot)
    kpos = kblock * CHUNK - PAD + j
    dist = jnp.abs(qpos - kpos)
    meta_ok = (slot == 0) & (kpos >= 0)
    real_ok = (slot > 0) & (kblock >= 1) & (kblock < nb) & (kpos >= N_META) & (dist <= ATT_WINDOW)
    valid = meta_ok | real_ok
    alibi = jnp.where(real_ok & (qpos >= N_META), dist.astype(F32), 0.0)

    for g in range(ATT_GRP):
        h = hk * ATT_GRP + g
        qg = q_ref[:, g * ATT_HEAD_DIM:(g + 1) * ATT_HEAD_DIM].astype(BF16)
        s = _bdot(qg, kcat, NT) * scale
        s = jnp.where(valid, s - slope_ref[h] * alibi, NEG)
        sink = sink_ref[h]
        m = jnp.maximum(jnp.max(s, axis=-1, keepdims=True), sink)
        p = jnp.exp(s - m)
        denom = jnp.sum(p, axis=-1, keepdims=True) + jnp.exp(sink - m)
        o = _bdot(p.astype(BF16), vcat) / denom
        o = jnp.where(qpos[:, 0:ATT_HEAD_DIM] >= 0, o, 0.0)
        o_ref[:, g * ATT_HEAD_DIM:(g + 1) * ATT_HEAD_DIM] = o.astype(o_ref.dtype)


def window_attention(qkv, slopes, sink):
    lp = qkv.shape[0]
    nb = lp // CHUNK
    qw = ATT_GRP * ATT_HEAD_DIM
    kcol = ATT_Q_HEADS * ATT_HEAD_DIM // ATT_HEAD_DIM
    vcol = kcol + ATT_KV_HEADS

    def kv_spec(col0, which):
        def imap(hk, b, *_):
            if which == 0:
                blk = 0
            else:
                blk = jnp.clip(b - 2 + which, 0, nb - 1)
            return (blk, col0 + hk)
        return pl.BlockSpec((CHUNK, ATT_HEAD_DIM), imap)

    return pl.pallas_call(
        _attn_kernel,
        grid_spec=pltpu.PrefetchScalarGridSpec(
            num_scalar_prefetch=2,
            grid=(ATT_KV_HEADS, nb),
            in_specs=[pl.BlockSpec((CHUNK, qw), lambda hk, b, *_: (b, hk))]
            + [kv_spec(kcol, w) for w in range(4)] + [kv_spec(vcol, w) for w in range(4)],
            out_specs=pl.BlockSpec((CHUNK, qw), lambda hk, b, *_: (b, hk)),
        ),
        out_shape=jax.ShapeDtypeStruct((lp, ATT_Q_HEADS * ATT_HEAD_DIM), BF16),
        compiler_params=_cparams(("arbitrary", "arbitrary")),
        name="window_attention",
    )(slopes, sink, *([qkv] * 9))


def _mlstm_direction(d, ce, q_ref, k_ref, v_ref, g_ref, gb_ref, tri_ref, h_ref, c_ref, n_ref, m_ref):
    tri = tri_ref[d]
    trib = tri > 0.5
    row = ce * CHUNK + lax.broadcasted_iota(I32, (CHUNK, CHUNK), 0)
    pre = g_ref[...] + gb_ref[d]
    real = row >= PAD
    real_col = real[:, 0:1]
    li_all = jnp.where(real, pre, 0.0)
    lf_all = jnp.where(real, -_softplus(-pre), 0.0)
    bl_all = jnp.dot(tri, lf_all, precision=HIGHEST, preferred_element_type=F32)
    li_t = li_all.T
    bl_t = bl_all.T
    g_all = jnp.sum(lf_all, axis=0, keepdims=True)

    for hd in range(ML_HEADS):
        fl = hd + ML_F_LANE
        li_col = li_all[:, hd:hd + 1]
        li_row = li_t[hd:hd + 1, :]
        bl_col = bl_all[:, fl:fl + 1]
        bl_row = bl_t[fl:fl + 1, :]
        g_tot = g_all[:, fl:fl + 1]
        st = d * ML_HEADS + hd
        ksl = slice(hd * ML_QK_DIM, (hd + 1) * ML_QK_DIM)
        vsl = slice(hd * ML_V_DIM, (hd + 1) * ML_V_DIM)

        qf = jnp.where(real_col, q_ref[:, ksl], 0.0)
        q = qf.astype(BF16)
        kf = jnp.where(real_col, k_ref[:, ksl], 0.0) * (ML_QK_DIM ** -0.5)
        kt = kf.T.astype(BF16)
        v = jnp.where(real_col, v_ref[:, vsl], 0.0)
        vb = v.astype(BF16)

        m_prev = m_ref[st, 0:1, 0:1]
        n_prev = n_ref[st]
        c_prev = c_ref[st]

        dmat = jnp.where(trib, bl_col - bl_row + li_row, NEG)
        m_inter = bl_col + m_prev
        m_t = jnp.maximum(jnp.max(dmat, axis=1, keepdims=True), m_inter)
        qk = jnp.dot(q, kt, preferred_element_type=F32)
        p = jnp.exp(dmat - m_t) * qk
        w_inter = jnp.exp(m_inter - m_t)
        num = (jnp.dot(p.astype(BF16), vb, preferred_element_type=F32)
               + w_inter * jnp.dot(q, c_prev.astype(BF16), preferred_element_type=F32))
        qn = jnp.sum(qf * n_prev, axis=1, keepdims=True)
        den = jnp.sum(p, axis=1, keepdims=True) + w_inter * qn
        h_ref[:, vsl] = num / jnp.maximum(jnp.abs(den), jnp.exp(-m_t))

        a_col = g_tot - bl_col + li_col
        m_loc = jnp.max(a_col, axis=0, keepdims=True)
        wa = jnp.exp(a_col - m_loc)
        c_loc = jnp.dot(kt, (wa * v).astype(BF16), preferred_element_type=F32)
        n_loc = jnp.sum(wa * kf, axis=0, keepdims=True)
        m_new = jnp.maximum(g_tot + m_prev, m_loc)
        s_old = jnp.exp(g_tot + m_prev - m_new)
        s_new = jnp.exp(m_loc - m_new)
        c_ref[st] = s_old * c_prev + s_new * c_loc
        n_ref[st] = s_old * n_prev + s_new * n_loc
        m_ref[st] = jnp.broadcast_to(m_new, m_ref.shape[1:])


def _mlstm_scan_kernel(qf_ref, kf_ref, vf_ref, gf_ref, qb_ref, kb_ref, vb_ref, gbk_ref, gb_ref, tri_ref,
                       hf_ref, hb_ref, c_ref, n_ref, m_ref):
    c = pl.program_id(0)
    nch = pl.num_programs(0)

    @pl.when(c == 0)
    def _():
        c_ref[...] = jnp.zeros(c_ref.shape, F32)
        n_ref[...] = jnp.zeros(n_ref.shape, F32)
        m_ref[...] = jnp.zeros(m_ref.shape, F32)

    _mlstm_direction(0, c, qf_ref, kf_ref, vf_ref, gf_ref, gb_ref, tri_ref, hf_ref, c_ref, n_ref, m_ref)
    _mlstm_direction(1, nch - 1 - c, qb_ref, kb_ref, vb_ref, gbk_ref, gb_ref, tri_ref, hb_ref, c_ref, n_ref, m_ref)


def mlstm_scan(qkvo, gates, gate_bias_rows, tri2):
    lp = qkvo.shape[0]
    nch = lp // CHUNK
    qk_w = ML_HEADS * ML_QK_DIM

    def specs(chunk, d):
        return [
            pl.BlockSpec((CHUNK, qk_w), lambda c: (chunk(c), 0)),
            pl.BlockSpec((CHUNK, qk_w), lambda c: (chunk(c), 1)),
            pl.BlockSpec((CHUNK, ML_WIDTH), lambda c: (chunk(c), 2 * qk_w // ML_WIDTH)),
            pl.BlockSpec((CHUNK, CHUNK), lambda c: (chunk(c), d)),
        ]

    fwd = lambda c: c
    bwd = lambda c: nch - 1 - c
    n_state = 2 * ML_HEADS
    return pl.pallas_call(
        _mlstm_scan_kernel,
        grid=(nch,),
        in_specs=specs(fwd, 0) + specs(bwd, 1) + [
            pl.BlockSpec((2, 1, CHUNK), lambda c: (0, 0, 0)),
            pl.BlockSpec((2, CHUNK, CHUNK), lambda c: (0, 0, 0)),
        ],
        out_specs=[pl.BlockSpec((CHUNK, ML_WIDTH), lambda c: (c, 0)),
                   pl.BlockSpec((CHUNK, ML_WIDTH), lambda c: (nch - 1 - c, 0))],
        out_shape=[jax.ShapeDtypeStruct((lp, ML_WIDTH), F32), jax.ShapeDtypeStruct((lp, ML_WIDTH), F32)],
        scratch_shapes=[pltpu.VMEM((n_state, ML_QK_DIM, ML_V_DIM), F32), pltpu.VMEM((n_state, 1, ML_QK_DIM), F32),
                        pltpu.VMEM((n_state, 8, CHUNK), F32)],
        compiler_params=_cparams(("arbitrary",)),
        name="mlstm_scan",
    )(qkvo, qkvo, qkvo, gates, qkvo, qkvo, qkvo, gates, gate_bias_rows, tri2)


def _mlstm_final_kernel(hf_ref, hb_ref, o_ref_in, nw_ref, y_ref):
    h = hf_ref[...] + hb_ref[...]
    o = o_ref_in[...]
    for hd in range(ML_HEADS):
        sl = slice(hd * ML_V_DIM, (hd + 1) * ML_V_DIM)
        hh = h[:, sl]
        hh = hh * lax.rsqrt(jnp.mean(hh * hh, axis=-1, keepdims=True) + NORM_EPS) * nw_ref[:, sl]
        y_ref[:, sl] = (_sigmoid(o[:, sl]) * hh).astype(y_ref.dtype)


def mlstm_final(hf, hb, qkvo, head_norm):
    lp = qkvo.shape[0]
    ocol = (2 * ML_HEADS * ML_QK_DIM + ML_WIDTH) // ML_WIDTH
    return pl.pallas_call(
        _mlstm_final_kernel,
        grid=(lp // CHUNK,),
        in_specs=[
            pl.BlockSpec((CHUNK, ML_WIDTH), lambda c: (c, 0)),
            pl.BlockSpec((CHUNK, ML_WIDTH), lambda c: (c, 0)),
            pl.BlockSpec((CHUNK, ML_WIDTH), lambda c: (c, ocol)),
            pl.BlockSpec((1, ML_WIDTH), lambda c: (0, 0)),
        ],
        out_specs=pl.BlockSpec((CHUNK, ML_WIDTH), lambda c: (c, 0)),
        out_shape=jax.ShapeDtypeStruct((lp, ML_WIDTH), BF16),
        compiler_params=_cparams(("arbitrary",)),
        name="mlstm_final",
    )(hf, hb, qkvo, head_norm.reshape(1, ML_WIDTH))


def _dft_factors(n):
    best = None
    for n1 in range(8, n + 1, 8):
        if n % n1 == 0:
            n2 = n // n1
            if best is None or n1 + n2 < best[0] + best[1]:
                best = (n1, n2)
    assert best is not None
    return best


def _fnet_stage1_kernel(x_ref, c2_ref, s2_ref, twc_ref, tws_ref, tr_ref, ti_ref, *, n2):
    x = x_ref[...]
    rows = lax.broadcasted_iota(I32, x.shape, 0)
    x = jnp.where(rows < n2, x, 0.0)
    cx = jnp.dot(c2_ref[...], x, precision=HIGHEST, preferred_element_type=F32)
    sx = jnp.dot(s2_ref[...], x, precision=HIGHEST, preferred_element_type=F32)
    twc = twc_ref[0]
    tws = tws_ref[0]
    tr_ref[...] = cx * twc - sx * tws
    ti_ref[...] = -(cx * tws + sx * twc)


def _fnet_stage2_kernel(tr_ref, ti_ref, c1_ref, s1_ref, cc_ref, sc_ref, o_ref, *, scale):
    tr = tr_ref[0]
    ti = ti_ref[0]
    c1 = c1_ref[...]
    s1 = s1_ref[...]
    yr = (jnp.dot(c1, tr, precision=HIGHEST, preferred_element_type=F32)
          + jnp.dot(s1, ti, precision=HIGHEST, preferred_element_type=F32))
    yi = (jnp.dot(c1, ti, precision=HIGHEST, preferred_element_type=F32)
          - jnp.dot(s1, tr, precision=HIGHEST, preferred_element_type=F32))
    cc = cc_ref[...]
    sc = sc_ref[...]
    for g in range(FN_GROUPS):
        sl = slice(g * FN_GROUP_DIM, (g + 1) * FN_GROUP_DIM)
        out = (jnp.dot(yr[:, sl], cc, precision=HIGHEST, preferred_element_type=F32)
               + jnp.dot(yi[:, sl], sc, precision=HIGHEST, preferred_element_type=F32))
        o_ref[:, sl] = out * scale


def fourier_mixer(u_fn):
    n, ch = u_fn.shape
    n1, n2 = _dft_factors(n)
    n2p = -(-n2 // 8) * 8
    two_pi = 2.0 * np.pi

    def cs(size, pad):
        idx = np.arange(size)
        ang = two_pi * ((idx[:, None] * idx[None, :]) % size) / size
        cm = np.zeros((pad, pad), np.float32)
        sm = np.zeros((pad, pad), np.float32)
        cm[:size, :size] = np.cos(ang)
        sm[:size, :size] = np.sin(ang)
        return jnp.asarray(cm), jnp.asarray(sm)

    c2, s2 = cs(n2, n2p)
    c1, s1 = cs(n1, n1)
    cc, sc = cs(FN_GROUP_DIM, FN_GROUP_DIM)
    tw_ang = two_pi * ((np.arange(n1)[:, None] * np.arange(n2)[None, :]) % n) / n
    twc = np.zeros((n1, n2p, 1), np.float32)
    tws = np.zeros((n1, n2p, 1), np.float32)
    twc[:, :n2, 0] = np.cos(tw_ang)
    tws[:, :n2, 0] = np.sin(tw_ang)

    x2 = u_fn.reshape(n2, n1 * ch)
    tr, ti = pl.pallas_call(
        functools.partial(_fnet_stage1_kernel, n2=n2),
        grid=(n1,),
        in_specs=[
            pl.BlockSpec((n2p, ch), lambda i: (0, i)),
            pl.BlockSpec((n2p, n2p), lambda i: (0, 0)),
            pl.BlockSpec((n2p, n2p), lambda i: (0, 0)),
            pl.BlockSpec((1, n2p, 1), lambda i: (i, 0, 0)),
            pl.BlockSpec((1, n2p, 1), lambda i: (i, 0, 0)),
        ],
        out_specs=[pl.BlockSpec((n2p, ch), lambda i: (0, i)), pl.BlockSpec((n2p, ch), lambda i: (0, i))],
        out_shape=[jax.ShapeDtypeStruct((n2p, n1 * ch), F32), jax.ShapeDtypeStruct((n2p, n1 * ch), F32)],
        compiler_params=_cparams(("arbitrary",)),
        name="fnet_stage1",
    )(x2, c2, s2, jnp.asarray(twc), jnp.asarray(tws))

    tr3 = tr.reshape(n2p, n1, ch)
    ti3 = ti.reshape(n2p, n1, ch)
    scale = 1.0 / math.sqrt(float(n) * FN_GROUP_DIM)
    out = pl.pallas_call(
        functools.partial(_fnet_stage2_kernel, scale=scale),
        grid=(n2,),
        in_specs=[
            pl.BlockSpec((1, n1, ch), lambda i: (i, 0, 0)),
            pl.BlockSpec((1, n1, ch), lambda i: (i, 0, 0)),
            pl.BlockSpec((n1, n1), lambda i: (0, 0)),
            pl.BlockSpec((n1, n1), lambda i: (0, 0)),
            pl.BlockSpec((FN_GROUP_DIM, FN_GROUP_DIM), lambda i: (0, 0)),
            pl.BlockSpec((FN_GROUP_DIM, FN_GROUP_DIM), lambda i: (0, 0)),
        ],
        out_specs=pl.BlockSpec((n1, ch), lambda i: (0, i)),
        out_shape=jax.ShapeDtypeStruct((n1, n2 * ch), F32),
        compiler_params=_cparams(("arbitrary",)),
        name="fnet_stage2",
    )(tr3, ti3, c1, s1, cc, sc)
    return out.reshape(n, ch)


def _cap_sizes(lp):
    n_tok = lp - PAD
    cap = EC_FACTOR * n_tok // N_EXPERTS
    capp = -(-cap // 16) * 16
    return cap, capp


def _topk_kernel(lg_ref, su_ref, pos_ref, aff_ref, start_ref, idx_ref, rank_scr, pos_scr, *, cap, capp, jb):
    ne, lp = lg_ref.shape
    nch = lp // CHUNK
    lg = lg_ref[...]
    mx = jnp.max(lg, axis=0, keepdims=True)
    ex = jnp.exp(lg - mx)
    aff = ex / jnp.sum(ex, axis=0, keepdims=True)
    lane = lax.broadcasted_iota(I32, (ne, lp), 1)
    real = lane >= PAD
    bits = jnp.where(real, pltpu.bitcast(aff, I32), -1)
    capf = float(cap)

    def bisect(i, thr):
        cand = thr | jnp.left_shift(jnp.int32(1), 30 - i)
        cnt = jnp.sum(jnp.where(bits >= cand, 1.0, 0.0), axis=1, keepdims=True)
        return jnp.where(cnt >= capf, cand, thr)

    thr = lax.fori_loop(0, 31, bisect, jnp.zeros((ne, 1), I32))
    gt = bits > thr
    eq = bits == thr
    need = capf - jnp.sum(jnp.where(gt, 1.0, 0.0), axis=1, keepdims=True)

    su = su_ref[...]

    def excl_cumsum(mask_f32, out_scr, want_starts):
        carry = jnp.zeros((ne, 1), F32)
        starts = jnp.zeros((ne, CHUNK), F32)
        lane_c = lax.broadcasted_iota(I32, (ne, CHUNK), 1)
        for c in range(nch):
            tile = mask_f32[:, c * CHUNK:(c + 1) * CHUNK]
            within = jnp.dot(tile.astype(BF16), su, preferred_element_type=F32)
            out_scr[:, c * CHUNK:(c + 1) * CHUNK] = within + carry
            if want_starts:
                starts = jnp.where(lane_c == c, carry, starts)
            carry = carry + jnp.sum(tile, axis=1, keepdims=True)
        if want_starts:
            starts = jnp.where(lane_c == nch, carry, starts)
        return starts

    excl_cumsum(jnp.where(eq, 1.0, 0.0), rank_scr, False)
    sel = gt | (eq & (rank_scr[...] < need))
    starts = excl_cumsum(jnp.where(sel, 1.0, 0.0), pos_scr, True)
    posf = jnp.where(sel, pos_scr[...], -1.0)
    pos_scr[...] = posf
    pos_ref[...] = posf.astype(I32)
    aff_ref[...] = jnp.where(sel, aff, 0.0)
    start_ref[...] = starts.astype(I32)

    tok = lax.broadcasted_iota(I32, (jb, lp), 1).astype(F32)
    lane_e = lax.broadcasted_iota(I32, (jb, ne), 1)
    for blk in range(capp // jb):
        jcol = (blk * jb + lax.broadcasted_iota(I32, (jb, 1), 0)).astype(F32)

        def per_expert(e, acc):
            prow = pos_scr[pl.ds(e, 1), :]
            hit = jnp.sum(jnp.where(prow == jcol, tok, 0.0), axis=1, keepdims=True)
            return jnp.where(lane_e == e, hit, acc)

        acc = lax.fori_loop(0, ne, per_expert, jnp.zeros((jb, ne), F32))
        idx_ref[blk * jb:(blk + 1) * jb, :] = acc.astype(I32)


def topk_route(logits_t):
    ne, lp = logits_t.shape
    cap, capp = _cap_sizes(lp)
    jb = 80 if capp % 80 == 0 else 16
    su = jnp.asarray(np.triu(np.ones((CHUNK, CHUNK), np.float32), 1), BF16)
    return pl.pallas_call(
        functools.partial(_topk_kernel, cap=cap, capp=capp, jb=jb),
        out_shape=[jax.ShapeDtypeStruct((ne, lp), I32), jax.ShapeDtypeStruct((ne, lp), F32),
                   jax.ShapeDtypeStruct((ne, CHUNK), I32), jax.ShapeDtypeStruct((capp, ne), I32)],
        scratch_shapes=[pltpu.VMEM((ne, lp), F32), pltpu.VMEM((ne, lp), F32)],
        compiler_params=pltpu.CompilerParams(vmem_limit_bytes=VMEM_LIMIT),
        name="topk_route",
    )(logits_t, su)


def _gather_kernel(idx_ref, u_hbm, o_ref, stage_ref, sem):
    e = pl.program_id(0)
    capp = stage_ref.shape[0]

    def row_copy(j):
        t = idx_ref[e, j]
        return pltpu.make_async_copy(u_hbm.at[pl.ds(t, 1), :], stage_ref.at[pl.ds(j, 1), :], sem)

    def issue(j, carry):
        row_copy(j).start()
        return carry

    def drain(j, carry):
        row_copy(j).wait()
        return carry

    lax.fori_loop(0, capp, issue, 0)
    lax.fori_loop(0, capp, drain, 0)
    o_ref[0] = stage_ref[...].astype(o_ref.dtype)


def gather_tokens(idx, u):
    ne, capp = idx.shape
    lp, d = u.shape
    return pl.pallas_call(
        _gather_kernel,
        grid_spec=pltpu.PrefetchScalarGridSpec(
            num_scalar_prefetch=1,
            grid=(ne,),
            in_specs=[pl.BlockSpec(memory_space=pl.ANY)],
            out_specs=pl.BlockSpec((1, capp, d), lambda e, *_: (e, 0, 0)),
            scratch_shapes=[pltpu.VMEM((capp, d), F32), pltpu.SemaphoreType.DMA],
        ),
        out_shape=jax.ShapeDtypeStruct((ne, capp, d), BF16),
        compiler_params=_cparams(("arbitrary",)),
        name="moe_gather",
    )(idx, u)


def _ffn_kernel(xs_ref, wg_ref, wu_ref, wd_ref, o_ref, hdn_ref, *, n_ff):
    s = pl.program_id(1)

    @pl.when(s < n_ff)
    def _():
        x = xs_ref[0]
        g = jnp.dot(x, wg_ref[0, 0].astype(BF16), preferred_element_type=F32)
        u = jnp.dot(x, wu_ref[0, 0].astype(BF16), preferred_element_type=F32)
        hdn_ref[s] = (g * _sigmoid(g) * u).astype(BF16)

    @pl.when(s >= n_ff)
    def _():
        wd = wd_ref[0, 0].astype(BF16)
        acc = None
        for j in range(n_ff):
            part = jnp.dot(hdn_ref[j], wd[j * FF_TILE:(j + 1) * FF_TILE, :], preferred_element_type=F32)
            acc = part if acc is None else acc + part
        o_ref[0] = acc


def expert_ffn(xs, w_gate, w_up, w_down, layer):
    ne, capp, d = xs.shape
    n_ff = EXPERT_FF // FF_TILE
    n_out = d // OUT_TILE
    return pl.pallas_call(
        functools.partial(_ffn_kernel, n_ff=n_ff),
        grid=(ne, n_ff + n_out),
        in_specs=[
            pl.BlockSpec((1, capp, d), lambda e, s: (e, 0, 0)),
            pl.BlockSpec((1, 1, d, FF_TILE), lambda e, s: (layer, e, 0, jnp.minimum(s, n_ff - 1))),
            pl.BlockSpec((1, 1, d, FF_TILE), lambda e, s: (layer, e, 0, jnp.minimum(s, n_ff - 1))),
            pl.BlockSpec((1, 1, EXPERT_FF, OUT_TILE), lambda e, s: (layer, e, 0, jnp.maximum(s - n_ff, 0))),
        ],
        out_specs=pl.BlockSpec((1, capp, OUT_TILE), lambda e, s: (e, 0, jnp.maximum(s - n_ff, 0))),
        out_shape=jax.ShapeDtypeStruct((ne, capp, d), F32),
        scratch_shapes=[pltpu.VMEM((n_ff, capp, FF_TILE), BF16)],
        compiler_params=_cparams(("arbitrary", "arbitrary")),
        name="expert_ffn",
    )(xs, w_gate, w_up, w_down)


def _combine_kernel(start_ref, h_ref, pos_ref, aff_ref, out_hbm, o_ref, g_ref, sem, *, capp, rows, n_half):
    b = pl.program_id(0)
    hh = pl.program_id(1)
    nb = pl.num_programs(0)
    ne = pos_ref.shape[0]
    eh = ne // n_half

    def row_copy(src, dst, slot):
        return pltpu.make_async_copy(out_hbm.at[pl.ds(src, 1), :], g_ref.at[slot, pl.ds(dst, 1), :], sem.at[slot])

    def block_rows(bq, hq):
        s0 = [start_ref[hq * eh + i, bq] for i in range(eh)]
        n = [start_ref[hq * eh + i, bq + 1] - s0[i] for i in range(eh)]
        return s0, n

    def issue(bq, hq, slot):
        s0, n = block_rows(bq, hq)
        off = jnp.int32(0)
        for i in range(eh):
            base = (hq * eh + i) * capp + s0[i]

            def body(j, carry, base=base, off=off):
                row_copy(base + j, off + j, slot).start()
                return carry

            lax.fori_loop(0, n[i], body, 0)
            off = off + n[i]

    @pl.when((b == 0) & (hh == 0))
    def _():
        g_ref[...] = jnp.zeros(g_ref.shape, F32)
        issue(b, hh, 0)

    @pl.when(hh + 1 < n_half)
    def _():
        issue(b, hh + 1, 1 - hh % 2)

    if n_half % 2 == 0:
        @pl.when((hh + 1 == n_half) & (b + 1 < nb))
        def _():
            issue(b + 1, 0, 0)

    slot = hh % 2
    s0, n = block_rows(b, hh)
    offs = []
    off = jnp.int32(0)
    for i in range(eh):
        offs.append(off)
        off = off + n[i]
    total = off

    def drain(j, carry):
        row_copy(0, j, slot).wait()
        return carry

    lax.fori_loop(0, total, drain, 0)

    r_iota = lax.broadcasted_iota(I32, (rows, CHUNK), 0)

    def chunk(ci, acc):
        r0 = pl.multiple_of(ci * rows, rows)
        rr = r_iota + r0
        onehot = jnp.zeros((rows, CHUNK), F32)
        gate = jnp.zeros((rows, CHUNK), F32)
        for i in range(eh):
            prow = pos_ref[pl.ds(hh * eh + i, 1), :]
            arow = aff_ref[pl.ds(hh * eh + i, 1), :]
            tgt = jnp.where(prow >= 0, prow - s0[i] + offs[i], -1)
            hit = rr == tgt
            onehot = jnp.where(hit, 1.0, onehot)
            gate = jnp.where(hit, arow, gate)
        gcol = jnp.sum(gate, axis=1, keepdims=True)
        gs = g_ref[slot, pl.ds(r0, rows), :] * gcol
        hi = gs.astype(BF16)
        lo = (gs - hi.astype(F32)).astype(BF16)
        oh = onehot.T.astype(BF16)
        return (acc + jnp.dot(oh, hi, preferred_element_type=F32)
                + jnp.dot(oh, lo, preferred_element_type=F32))

    n_chunks = (total + rows - 1) // rows

    @pl.when(hh == 0)
    def _():
        o_ref[...] = lax.fori_loop(0, n_chunks, chunk, h_ref[...])

    @pl.when(hh > 0)
    def _():
        o_ref[...] = lax.fori_loop(0, n_chunks, chunk, o_ref[...])


def moe_combine(h, pos, aff, starts, ffn_out):
    lp, d = h.shape
    ne, capp, _ = ffn_out.shape
    rows = CHUNK
    n_half = 2
    stage_rows = (ne // n_half) * CHUNK
    return pl.pallas_call(
        functools.partial(_combine_kernel, capp=capp, rows=rows, n_half=n_half),
        grid_spec=pltpu.PrefetchScalarGridSpec(
            num_scalar_prefetch=1,
            grid=(lp // CHUNK, n_half),
            in_specs=[
                pl.BlockSpec((CHUNK, d), lambda b, hh, *_: (b, 0)),
                pl.BlockSpec((ne, CHUNK), lambda b, hh, *_: (0, b)),
                pl.BlockSpec((ne, CHUNK), lambda b, hh, *_: (0, b)),
                pl.BlockSpec(memory_space=pl.ANY),
            ],
            out_specs=pl.BlockSpec((CHUNK, d), lambda b, hh, *_: (b, 0)),
            scratch_shapes=[pltpu.VMEM((2, stage_rows, d), F32), pltpu.SemaphoreType.DMA((2,))],
        ),
        out_shape=jax.ShapeDtypeStruct((lp, d), F32),
        compiler_params=_cparams(("arbitrary", "arbitrary")),
        name="moe_combine",
    )(starts, h, pos, aff, ffn_out.reshape(ne * capp, d))


def moe_layer(h, norm_w, w_router, w_gate, w_up, w_down, layer):
    u, logits_t = rmsnorm_router(h, norm_w, w_router)
    pos, aff, starts, idx_t = topk_route(logits_t)
    xs = gather_tokens(idx_t.T, u)
    ffn_out = expert_ffn(xs, w_gate, w_up, w_down, layer)
    return moe_combine(h, pos, aff, starts, ffn_out)


def _scan_masks():
    lower = np.tril(np.ones((CHUNK, CHUNK), np.float32))
    return jnp.asarray(np.stack([lower, lower.T]))


def _pad_cols(w, width):
    return jnp.pad(w, ((0, 0), (0, width - w.shape[1])))


def ab_mixer(h, norm_w, w_in, conv_w, conv_b, a_log, dt_bias, d_skip, ssd_norm, sink, w_out):
    u = rmsnorm(h, norm_w, BF16)
    zx_w = SSD_INNER + SSD_CONV_CH
    dt0 = zx_w
    qkv0 = zx_w + 2 * SSD_HEADS
    z = matmul([u], w_in, 512, col0=0, n=SSD_INNER)
    xbc = matmul([u], w_in, 512, col0=SSD_INNER, n=SSD_CONV_CH)
    w_dt = jnp.concatenate([_pad_cols(w_in[:, dt0:dt0 + SSD_HEADS], CHUNK),
                            _pad_cols(w_in[:, dt0 + SSD_HEADS:qkv0], CHUNK)], axis=1)
    dt_raw = matmul([u], w_dt, 2 * CHUNK)
    qkv = matmul([u], w_in[:, qkv0:], 512)

    xbc_act = ssd_conv(xbc, conv_w, conv_b)
    dt_bias_rows = _pad_cols(dt_bias, CHUNK).reshape(2, 1, CHUNK)
    acoef_rows = _pad_cols(-jnp.exp(a_log), CHUNK).reshape(2, 1, CHUNK)
    yf, yb = ssd_scan(xbc_act, dt_raw, dt_bias_rows, acoef_rows, _scan_masks())
    d_skip_row = jnp.repeat(d_skip, SSD_HEAD_DIM).reshape(1, SSD_INNER)
    y_ssd = ssd_final(yf, yb, xbc_act, z, d_skip_row, ssd_norm)

    slopes = 2.0 ** (-8.0 * jnp.arange(1, ATT_Q_HEADS + 1, dtype=F32) / ATT_Q_HEADS)
    y_att = window_attention(qkv, slopes, sink)
    return matmul([y_ssd, y_att], w_out, 512, res=h)


def cd_mixer(h, norm_w, w_in, i_bias, f_bias, head_norm, w_out):
    lp = h.shape[0]
    u = rmsnorm(h, norm_w, BF16)
    qkvo_w = 2 * ML_HEADS * ML_QK_DIM + 2 * ML_WIDTH
    i0 = qkvo_w
    f0 = i0 + 2 * ML_HEADS
    fn0 = f0 + 2 * ML_HEADS
    qkvo = matmul([u], w_in, 512, col0=0, n=qkvo_w)

    def gate_block(dirn):
        wi = w_in[:, i0 + dirn * ML_HEADS:i0 + (dirn + 1) * ML_HEADS]
        wf = w_in[:, f0 + dirn * ML_HEADS:f0 + (dirn + 1) * ML_HEADS]
        return _pad_cols(jnp.concatenate([_pad_cols(wi, ML_F_LANE), wf], axis=1), CHUNK)

    def bias_block(dirn):
        row = jnp.concatenate([jnp.pad(i_bias[dirn], (0, ML_F_LANE - ML_HEADS)), f_bias[dirn]])
        return jnp.pad(row, (0, CHUNK - row.shape[0]))

    gates = matmul([u], jnp.concatenate([gate_block(0), gate_block(1)], axis=1), 2 * CHUNK)
    gate_bias_rows = jnp.stack([bias_block(0), bias_block(1)]).reshape(2, 1, CHUNK)
    hf, hb = mlstm_scan(qkvo, gates, gate_bias_rows, _scan_masks())
    y_ml = mlstm_final(hf, hb, qkvo, head_norm)

    u_fn = matmul([u], w_in[:, fn0:], 512)
    y_fn = fourier_mixer(u_fn[PAD:])
    y_fn = jnp.pad(y_fn, ((PAD, 0), (0, 0))).astype(BF16)
    return matmul([y_ml, y_fn], w_out, 512, res=h)


def kernel(x, meta_tokens, norm_mix, ab_w_in, ab_conv_w, ab_conv_b, ab_a_log, ab_dt_bias, ab_d_skip, ab_ssd_norm,
           ab_sink, ab_w_out, cd_w_in, cd_i_bias, cd_f_bias, cd_head_norm, cd_w_out, norm_ffn, moe_router,
           moe_w_gate, moe_w_up, moe_w_down, final_norm):
    bsz, seq, d = x.shape
    assert bsz == 1 and d == D_MODEL and seq % CHUNK == 0
    depth = norm_mix.shape[0]
    h = jnp.concatenate([jnp.zeros((PAD, d), F32), meta_tokens.astype(F32), x[0]], axis=0)
    for layer in range(depth):
        j = layer // 2
        if layer % 2 == 0:
            h = ab_mixer(h, norm_mix[layer], ab_w_in[j], ab_conv_w[j], ab_conv_b[j], ab_a_log[j], ab_dt_bias[j],
                         ab_d_skip[j], ab_ssd_norm[j], ab_sink[j], ab_w_out[j])
        else:
            h = cd_mixer(h, norm_mix[layer], cd_w_in[j], cd_i_bias[j], cd_f_bias[j], cd_head_norm[j], cd_w_out[j])
        h = moe_layer(h, norm_ffn[layer], moe_router[layer], moe_w_gate, moe_w_up, moe_w_down, layer)
    return final_rmsnorm(h, final_norm)[None]
```

```python
import functools
import math

import numpy as np
import jax
import jax.numpy as jnp
from jax import lax
from jax.experimental import pallas as pl
from jax.experimental.pallas import tpu as pltpu

F32 = jnp.float32
BF16 = jnp.bfloat16
I32 = jnp.int32
HIGHEST = lax.Precision.HIGHEST

D_MODEL = 4096
N_META = 16
CHUNK = 128
PAD = CHUNK - N_META
NORM_EPS = 1e-6
NEG = -1e30

SSD_HEADS = 32
SSD_HEAD_DIM = 64
SSD_INNER = SSD_HEADS * SSD_HEAD_DIM
SSD_GROUPS = 4
SSD_STATE = 128
SSD_CONV = 5
SSD_CONV_CH = SSD_INNER + 2 * SSD_GROUPS * SSD_STATE

ATT_Q_HEADS = 16
ATT_KV_HEADS = 4
ATT_HEAD_DIM = 128
ATT_WINDOW = 128
ATT_GRP = ATT_Q_HEADS // ATT_KV_HEADS

ML_HEADS = 6
ML_QK_DIM = 256
ML_V_DIM = 512
ML_WIDTH = ML_HEADS * ML_V_DIM
ML_F_LANE = 8

FN_GROUPS = 4
FN_GROUP_DIM = 256
FN_WIDTH = FN_GROUPS * FN_GROUP_DIM

N_EXPERTS = 16
EC_FACTOR = 2
EXPERT_FF = 1536
FF_TILE = 256
OUT_TILE = 512
COMBINE_SLAB = 512

VMEM_LIMIT = 56 * 1024 * 1024


def _cparams(sem):
    return pltpu.CompilerParams(dimension_semantics=sem, vmem_limit_bytes=VMEM_LIMIT)


def _row_tile(lp):
    return 640 if lp % 640 == 0 else CHUNK


def _sigmoid(x):
    return 1.0 / (1.0 + jnp.exp(-x))


def _softplus(x):
    return jnp.maximum(x, 0.0) + jnp.log1p(jnp.exp(-jnp.abs(x)))


NT = (((1,), (1,)), ((), ()))


def _bdot(a, b, dims=None):
    if dims is None:
        return jnp.dot(a, b, preferred_element_type=F32)
    return lax.dot_general(a, b, dims, preferred_element_type=F32)


def _rmsnorm_kernel(h_ref, w_ref, o_ref):
    x = h_ref[...]
    y = x * lax.rsqrt(jnp.mean(x * x, axis=-1, keepdims=True) + NORM_EPS) * w_ref[...]
    o_ref[...] = y.astype(o_ref.dtype)


def rmsnorm(h, w, out_dtype):
    lp, d = h.shape
    tm = _row_tile(lp)
    return pl.pallas_call(
        _rmsnorm_kernel,
        grid=(lp // tm,),
        in_specs=[pl.BlockSpec((tm, d), lambda i: (i, 0)), pl.BlockSpec((1, d), lambda i: (0, 0))],
        out_specs=pl.BlockSpec((tm, d), lambda i: (i, 0)),
        out_shape=jax.ShapeDtypeStruct((lp, d), out_dtype),
        compiler_params=_cparams(("arbitrary",)),
        name="rmsnorm",
    )(h, w.reshape(1, d))


def _rmsnorm_router_kernel(h_ref, w_ref, wrt_ref, u_ref, lg_ref):
    x = h_ref[...]
    y = x * lax.rsqrt(jnp.mean(x * x, axis=-1, keepdims=True) + NORM_EPS) * w_ref[...]
    u_ref[...] = y
    lg_ref[...] = lax.dot_general(wrt_ref[...], y, (((1,), (1,)), ((), ())), precision=HIGHEST,
                                  preferred_element_type=F32)


def rmsnorm_router(h, w, w_router):
    lp, d = h.shape
    tm = _row_tile(lp)
    return pl.pallas_call(
        _rmsnorm_router_kernel,
        grid=(lp // tm,),
        in_specs=[pl.BlockSpec((tm, d), lambda i: (i, 0)), pl.BlockSpec((1, d), lambda i: (0, 0)),
                  pl.BlockSpec((N_EXPERTS, d), lambda i: (0, 0))],
        out_specs=[pl.BlockSpec((tm, d), lambda i: (i, 0)), pl.BlockSpec((N_EXPERTS, tm), lambda i: (0, i))],
        out_shape=[jax.ShapeDtypeStruct((lp, d), F32), jax.ShapeDtypeStruct((N_EXPERTS, lp), F32)],
        compiler_params=_cparams(("arbitrary",)),
        name="rmsnorm_router",
    )(h, w.reshape(1, d), w_router.T)


def _final_norm_kernel(h_ref, w_ref, o_ref):
    x = h_ref[...]
    o_ref[...] = x * lax.rsqrt(jnp.mean(x * x, axis=-1, keepdims=True) + NORM_EPS) * w_ref[...]


def final_rmsnorm(h, w):
    lp, d = h.shape
    n = lp - CHUNK
    return pl.pallas_call(
        _final_norm_kernel,
        grid=(n // CHUNK,),
        in_specs=[pl.BlockSpec((CHUNK, d), lambda i: (i + 1, 0)), pl.BlockSpec((1, d), lambda i: (0, 0))],
        out_specs=pl.BlockSpec((CHUNK, d), lambda i: (i, 0)),
        out_shape=jax.ShapeDtypeStruct((n, d), F32),
        compiler_params=_cparams(("arbitrary",)),
        name="final_norm",
    )(h, w.reshape(1, d))


def _matmul_kernel(*refs, k_splits, has_res):
    n_x = len(k_splits)
    x_refs = refs[:n_x]
    w_ref = refs[n_x]
    r_ref = refs[n_x + 1] if has_res else None
    o_ref = refs[n_x + 1 + int(has_res)]
    wb_ref = refs[n_x + 2 + int(has_res)]

    @pl.when(pl.program_id(1) == 0)
    def _():
        wb_ref[...] = w_ref[...].astype(BF16)

    acc = None
    k0 = 0
    for x_ref, kk in zip(x_refs, k_splits):
        part = jnp.dot(x_ref[...], wb_ref[k0:k0 + kk, :], preferred_element_type=F32)
        acc = part if acc is None else acc + part
        k0 += kk
    if has_res:
        acc = acc + r_ref[...]
    o_ref[...] = acc.astype(o_ref.dtype)


def matmul(xs, w, tn, res=None, out_dtype=F32, col0=0, n=None):
    lp = xs[0].shape[0]
    k_splits = tuple(int(x.shape[1]) for x in xs)
    k = w.shape[0]
    n = w.shape[1] - col0 if n is None else n
    assert sum(k_splits) == k and n % tn == 0 and col0 % tn == 0
    cb0 = col0 // tn
    tm = _row_tile(lp)
    in_specs = [pl.BlockSpec((tm, kk), lambda j, i: (i, 0)) for kk in k_splits]
    in_specs.append(pl.BlockSpec((k, tn), lambda j, i: (0, cb0 + j)))
    args = list(xs) + [w]
    if res is not None:
        in_specs.append(pl.BlockSpec((tm, tn), lambda j, i: (i, j)))
        args.append(res)
    return pl.pallas_call(
        functools.partial(_matmul_kernel, k_splits=k_splits, has_res=res is not None),
        grid=(n // tn, lp // tm),
        in_specs=in_specs,
        out_specs=pl.BlockSpec((tm, tn), lambda j, i: (i, j)),
        out_shape=jax.ShapeDtypeStruct((lp, n), out_dtype),
        scratch_shapes=[pltpu.VMEM((k, tn), BF16)],
        compiler_params=_cparams(("arbitrary", "arbitrary")),
        name="matmul",
    )(*args)


def _ssd_conv_kernel(cur_ref, prev_ref, next_ref, w_ref, b_ref, o_ref, ext_ref):
    c = pl.program_id(0)
    nch = pl.num_programs(0)
    halo = 8
    ext_ref[0:halo, :] = jnp.where(c > 0, prev_ref[...], 0.0)
    ext_ref[halo:halo + CHUNK, :] = cur_ref[...]
    ext_ref[halo + CHUNK:2 * halo + CHUNK, :] = jnp.where(c < nch - 1, next_ref[...], 0.0)
    w = w_ref[...]
    acc = jnp.zeros(cur_ref.shape, F32) + b_ref[...]
    centre = (SSD_CONV - 1) // 2
    for j in range(SSD_CONV):
        acc = acc + w[j:j + 1, :] * ext_ref[pl.ds(halo - centre + j, CHUNK), :]
    y = acc * _sigmoid(acc)
    row = c * CHUNK + lax.broadcasted_iota(I32, y.shape, 0)
    o_ref[...] = jnp.where(row >= PAD, y, 0.0)


def ssd_conv(xbc, conv_w, conv_b):
    lp, ch = xbc.shape
    nch = lp // CHUNK
    sub = CHUNK // 8
    return pl.pallas_call(
        _ssd_conv_kernel,
        grid=(nch,),
        in_specs=[
            pl.BlockSpec((CHUNK, ch), lambda c: (c, 0)),
            pl.BlockSpec((8, ch), lambda c: (jnp.maximum(c * sub - 1, 0), 0)),
            pl.BlockSpec((8, ch), lambda c: (jnp.minimum((c + 1) * sub, nch * sub - 1), 0)),
            pl.BlockSpec((SSD_CONV, ch), lambda c: (0, 0)),
            pl.BlockSpec((1, ch), lambda c: (0, 0)),
        ],
        out_specs=pl.BlockSpec((CHUNK, ch), lambda c: (c, 0)),
        out_shape=jax.ShapeDtypeStruct((lp, ch), F32),
        scratch_shapes=[pltpu.VMEM((CHUNK + 16, ch), F32)],
        compiler_params=_cparams(("arbitrary",)),
        name="ssd_conv",
    )(xbc, xbc, xbc, conv_w, conv_b.reshape(1, ch))


def _ssd_direction(d, ce, x_ref, b_ref, c_ref, dt_ref, dtb_ref, acoef_ref, tri_ref, y_ref, state_ref):
    tri = tri_ref[d]
    trib = tri > 0.5
    row = ce * CHUNK + lax.broadcasted_iota(I32, (CHUNK, CHUNK), 0)
    dt = jnp.where(row >= PAD, _softplus(dt_ref[...] + dtb_ref[d]), 0.0)
    a = dt * acoef_ref[d]
    a_cum = jnp.dot(tri, a, precision=HIGHEST, preferred_element_type=F32)
    a_cum_t = a_cum.T
    a_tot = jnp.sum(a, axis=0, keepdims=True)
    e_cum = jnp.exp(a_cum)
    d_out = jnp.exp(a_tot - a_cum)
    e_tot = jnp.exp(a_tot)
    lane = lax.broadcasted_iota(I32, (CHUNK, CHUNK), 1)
    first = lane < SSD_HEAD_DIM
    first_row = first[0:1, :]

    def pair_cols(m, h0):
        return jnp.where(first, m[:, h0:h0 + 1], m[:, h0 + 1:h0 + 2])

    hpg = SSD_HEADS // SSD_GROUPS
    for g in range(SSD_GROUPS):
        bg = b_ref[:, g * SSD_STATE:(g + 1) * SSD_STATE]
        cg = c_ref[:, g * SSD_STATE:(g + 1) * SSD_STATE].astype(BF16)
        cb = _bdot(cg, bg.astype(BF16), NT)
        bt = bg.T.astype(BF16)
        for pp in range(hpg // 2):
            h0 = g * hpg + 2 * pp
            p_idx = d * (SSD_HEADS // 2) + h0 // 2
            lo = h0 * SSD_HEAD_DIM
            x_p = x_ref[:, lo:lo + 2 * SSD_HEAD_DIM]
            xdt = x_p * pair_cols(dt, h0)
            xw = (xdt * pair_cols(d_out, h0)).astype(BF16)
            xdt_b = xdt.astype(BF16)
            yd = []
            for hh in (h0, h0 + 1):
                diff = a_cum[:, hh:hh + 1] - a_cum_t[hh:hh + 1, :]
                dec = jnp.exp(jnp.where(trib, diff, NEG))
                yd.append(_bdot((cb * dec).astype(BF16), xdt_b))
            y_diag = jnp.where(first, yd[0], yd[1])
            st_prev = state_ref[p_idx]
            y_off = _bdot(cg, st_prev.astype(BF16)) * pair_cols(e_cum, h0)
            y_ref[:, lo:lo + 2 * SSD_HEAD_DIM] = y_diag + y_off
            st_new = _bdot(bt, xw)
            cdec = jnp.where(first_row, e_tot[:, h0:h0 + 1], e_tot[:, h0 + 1:h0 + 2])
            state_ref[p_idx] = st_prev * cdec + st_new


def _ssd_scan_kernel(xf_ref, bf_ref, cf_ref, dtf_ref, xb_ref, bb_ref, cb_ref, dtbk_ref, dtb_ref, acoef_ref, tri_ref,
                     yf_ref, yb_ref, state_ref):
    c = pl.program_id(0)
    nch = pl.num_programs(0)

    @pl.when(c == 0)
    def _():
        state_ref[...] = jnp.zeros(state_ref.shape, F32)

    _ssd_direction(0, c, xf_ref, bf_ref, cf_ref, dtf_ref, dtb_ref, acoef_ref, tri_ref, yf_ref, state_ref)
    _ssd_direction(1, nch - 1 - c, xb_ref, bb_ref, cb_ref, dtbk_ref, dtb_ref, acoef_ref, tri_ref, yb_ref, state_ref)


def ssd_scan(xbc_act, dt_raw, dt_bias_rows, acoef_rows, tri2):
    lp = xbc_act.shape[0]
    nch = lp // CHUNK
    gs = SSD_GROUPS * SSD_STATE

    def specs(chunk, d):
        return [
            pl.BlockSpec((CHUNK, SSD_INNER), lambda c: (chunk(c), 0)),
            pl.BlockSpec((CHUNK, gs), lambda c: (chunk(c), SSD_INNER // gs)),
            pl.BlockSpec((CHUNK, gs), lambda c: (chunk(c), SSD_INNER // gs + 1)),
            pl.BlockSpec((CHUNK, CHUNK), lambda c: (chunk(c), d)),
        ]

    fwd = lambda c: c
    bwd = lambda c: nch - 1 - c
    return pl.pallas_call(
        _ssd_scan_kernel,
        grid=(nch,),
        in_specs=specs(fwd, 0) + specs(bwd, 1) + [
            pl.BlockSpec((2, 1, CHUNK), lambda c: (0, 0, 0)),
            pl.BlockSpec((2, 1, CHUNK), lambda c: (0, 0, 0)),
            pl.BlockSpec((2, CHUNK, CHUNK), lambda c: (0, 0, 0)),
        ],
        out_specs=[pl.BlockSpec((CHUNK, SSD_INNER), lambda c: (c, 0)),
                   pl.BlockSpec((CHUNK, SSD_INNER), lambda c: (nch - 1 - c, 0))],
        out_shape=[jax.ShapeDtypeStruct((lp, SSD_INNER), F32), jax.ShapeDtypeStruct((lp, SSD_INNER), F32)],
        scratch_shapes=[pltpu.VMEM((SSD_HEADS, SSD_STATE, 2 * SSD_HEAD_DIM), F32)],
        compiler_params=_cparams(("arbitrary",)),
        name="ssd_scan",
    )(xbc_act, xbc_act, xbc_act, dt_raw, xbc_act, xbc_act, xbc_act, dt_raw, dt_bias_rows, acoef_rows, tri2)


def _ssd_final_kernel(yf_ref, yb_ref, x_ref, z_ref, dskip_ref, nw_ref, o_ref):
    y = yf_ref[...] + yb_ref[...] + dskip_ref[...] * x_ref[...]
    z = z_ref[...]
    y = y * (z * _sigmoid(z))
    gw = SSD_INNER // SSD_GROUPS
    for g in range(SSD_GROUPS):
        yg = y[:, g * gw:(g + 1) * gw]
        yg = yg * lax.rsqrt(jnp.mean(yg * yg, axis=-1, keepdims=True) + NORM_EPS)
        o_ref[:, g * gw:(g + 1) * gw] = (yg * nw_ref[:, g * gw:(g + 1) * gw]).astype(o_ref.dtype)


def ssd_final(yf, yb, xbc_act, z, d_skip_row, norm_w):
    lp = xbc_act.shape[0]
    blk = pl.BlockSpec((CHUNK, SSD_INNER), lambda c: (c, 0))
    row = pl.BlockSpec((1, SSD_INNER), lambda c: (0, 0))
    return pl.pallas_call(
        _ssd_final_kernel,
        grid=(lp // CHUNK,),
        in_specs=[blk, blk, blk, blk, row, row],
        out_specs=blk,
        out_shape=jax.ShapeDtypeStruct((lp, SSD_INNER), BF16),
        compiler_params=_cparams(("arbitrary",)),
        name="ssd_final",
    )(yf, yb, xbc_act, z, d_skip_row, norm_w.reshape(1, SSD_INNER))


def _attn_kernel(slope_ref, sink_ref, q_ref, km_ref, kp_ref, ks_ref, kn_ref, vm_ref, vp_ref, vs_ref, vn_ref, o_ref):
    b = pl.program_id(0)
    nb = pl.num_programs(0)
    scale = ATT_HEAD_DIM ** -0.5

    qpos = b * CHUNK - PAD + lax.broadcasted_iota(I32, (CHUNK, 4 * CHUNK), 0)
    col = lax.broadcasted_iota(I32, (CHUNK, 4 * CHUNK), 1)
    slot = col // CHUNK
    j = col - slot * CHUNK
    kblock = jnp.where(slot == 0, 0, b - 2 + slot)
    kpos = kblock * CHUNK - PAD + j
    dist = jnp.abs(qpos - kpos)
    meta_ok = (slot == 0) & (kpos >= 0)
    real_ok = (slot > 0) & (kblock >= 1) & (kblock < nb) & (kpos >= N_META) & (dist <= ATT_WINDOW)
    valid = meta_ok | real_ok
    alibi = jnp.where(real_ok & (qpos >= N_META), dist.astype(F32), 0.0)
    q_real = qpos[:, 0:ATT_HEAD_DIM] >= 0

    for hk in range(ATT_KV_HEADS):
        ksl = slice(hk * ATT_HEAD_DIM, (hk + 1) * ATT_HEAD_DIM)
        kcat = jnp.concatenate([km_ref[:, ksl], kp_ref[:, ksl], ks_ref[:, ksl], kn_ref[:, ksl]],
                               axis=0).astype(BF16)
        vcat = jnp.concatenate([vm_ref[:, ksl], vp_ref[:, ksl], vs_ref[:, ksl], vn_ref[:, ksl]],
                               axis=0).astype(BF16)
        for g in range(ATT_GRP):
            h = hk * ATT_GRP + g
            hsl = slice(h * ATT_HEAD_DIM, (h + 1) * ATT_HEAD_DIM)
            qg = q_ref[:, hsl].astype(BF16)
            s = _bdot(qg, kcat, NT) * scale
            s = jnp.where(valid, s - slope_ref[h] * alibi, NEG)
            sink = sink_ref[h]
            m = jnp.maximum(jnp.max(s, axis=-1, keepdims=True), sink)
            p = jnp.exp(s - m)
            denom = jnp.sum(p, axis=-1, keepdims=True) + jnp.exp(sink - m)
            o = _bdot(p.astype(BF16), vcat) / denom
            o_ref[:, hsl] = jnp.where(q_real, o, 0.0).astype(o_ref.dtype)


def window_attention(qkv, slopes, sink):
    lp = qkv.shape[0]
    nb = lp // CHUNK
    qw = ATT_Q_HEADS * ATT_HEAD_DIM
    kvw = ATT_KV_HEADS * ATT_HEAD_DIM
    kcol = qw // kvw
    vcol = kcol + 1

    def kv_spec(col, which):
        def imap(b, *_):
            if which == 0:
                blk = 0
            else:
                blk = jnp.clip(b - 2 + which, 0, nb - 1)
            return (blk, col)
        return pl.BlockSpec((CHUNK, kvw), imap)

    return pl.pallas_call(
        _attn_kernel,
        grid_spec=pltpu.PrefetchScalarGridSpec(
            num_scalar_prefetch=2,
            grid=(nb,),
            in_specs=[pl.BlockSpec((CHUNK, qw), lambda b, *_: (b, 0))]
            + [kv_spec(kcol, w) for w in range(4)] + [kv_spec(vcol, w) for w in range(4)],
            out_specs=pl.BlockSpec((CHUNK, qw), lambda b, *_: (b, 0)),
        ),
        out_shape=jax.ShapeDtypeStruct((lp, qw), BF16),
        compiler_params=_cparams(("arbitrary",)),
        name="window_attention",
    )(slopes, sink, *([qkv] * 9))


def _mlstm_direction(d, ce, q_ref, k_ref, v_ref, g_ref, gb_ref, tri_ref, h_ref, c_ref, n_ref, m_ref):
    tri = tri_ref[d]
    trib = tri > 0.5
    row = ce * CHUNK + lax.broadcasted_iota(I32, (CHUNK, CHUNK), 0)
    pre = g_ref[...] + gb_ref[d]
    real = row >= PAD
    real_col = real[:, 0:1]
    li_all = jnp.where(real, pre, 0.0)
    lf_all = jnp.where(real, -_softplus(-pre), 0.0)
    bl_all = jnp.dot(tri, lf_all, precision=HIGHEST, preferred_element_type=F32)
    li_t = li_all.T
    bl_t = bl_all.T
    g_all = jnp.sum(lf_all, axis=0, keepdims=True)

    for hd in range(ML_HEADS):
        fl = hd + ML_F_LANE
        li_col = li_all[:, hd:hd + 1]
        li_row = li_t[hd:hd + 1, :]
        bl_col = bl_all[:, fl:fl + 1]
        bl_row = bl_t[fl:fl + 1, :]
        g_tot = g_all[:, fl:fl + 1]
        st = d * ML_HEADS + hd
        ksl = slice(hd * ML_QK_DIM, (hd + 1) * ML_QK_DIM)
        vsl = slice(hd * ML_V_DIM, (hd + 1) * ML_V_DIM)

        qf = jnp.where(real_col, q_ref[:, ksl], 0.0)
        q = qf.astype(BF16)
        kf = jnp.where(real_col, k_ref[:, ksl], 0.0) * (ML_QK_DIM ** -0.5)
        kt = kf.T.astype(BF16)
        v = jnp.where(real_col, v_ref[:, vsl], 0.0)
        vb = v.astype(BF16)

        m_prev = m_ref[st, 0:1, 0:1]
        n_prev = n_ref[st]
        c_prev = c_ref[st]

        dmat = jnp.where(trib, bl_col - bl_row + li_row, NEG)
        m_inter = bl_col + m_prev
        m_t = jnp.maximum(jnp.max(dmat, axis=1, keepdims=True), m_inter)
        qk = jnp.dot(q, kt, preferred_element_type=F32)
        p = jnp.exp(dmat - m_t) * qk
        w_inter = jnp.exp(m_inter - m_t)
        num = (jnp.dot(p.astype(BF16), vb, preferred_element_type=F32)
               + w_inter * jnp.dot(q, c_prev.astype(BF16), preferred_element_type=F32))
        qn = jnp.sum(qf * n_prev, axis=1, keepdims=True)
        den = jnp.sum(p, axis=1, keepdims=True) + w_inter * qn
        h_ref[:, vsl] = num / jnp.maximum(jnp.abs(den), jnp.exp(-m_t))

        a_col = g_tot - bl_col + li_col
        m_loc = jnp.max(a_col, axis=0, keepdims=True)
        wa = jnp.exp(a_col - m_loc)
        c_loc = jnp.dot(kt, (wa * v).astype(BF16), preferred_element_type=F32)
        n_loc = jnp.sum(wa * kf, axis=0, keepdims=True)
        m_new = jnp.maximum(g_tot + m_prev, m_loc)
        s_old = jnp.exp(g_tot + m_prev - m_new)
        s_new = jnp.exp(m_loc - m_new)
        c_ref[st] = s_old * c_prev + s_new * c_loc
        n_ref[st] = s_old * n_prev + s_new * n_loc
        m_ref[st] = jnp.broadcast_to(m_new, m_ref.shape[1:])


def _mlstm_scan_kernel(qf_ref, kf_ref, vf_ref, gf_ref, qb_ref, kb_ref, vb_ref, gbk_ref, gb_ref, tri_ref,
                       hf_ref, hb_ref, c_ref, n_ref, m_ref):
    c = pl.program_id(0)
    nch = pl.num_programs(0)

    @pl.when(c == 0)
    def _():
        c_ref[...] = jnp.zeros(c_ref.shape, F32)
        n_ref[...] = jnp.zeros(n_ref.shape, F32)
        m_ref[...] = jnp.zeros(m_ref.shape, F32)

    _mlstm_direction(0, c, qf_ref, kf_ref, vf_ref, gf_ref, gb_ref, tri_ref, hf_ref, c_ref, n_ref, m_ref)
    _mlstm_direction(1, nch - 1 - c, qb_ref, kb_ref, vb_ref, gbk_ref, gb_ref, tri_ref, hb_ref, c_ref, n_ref, m_ref)


def mlstm_scan(qkvo, gates, gate_bias_rows, tri2):
    lp = qkvo.shape[0]
    nch = lp // CHUNK
    qk_w = ML_HEADS * ML_QK_DIM

    def specs(chunk, d):
        return [
            pl.BlockSpec((CHUNK, qk_w), lambda c: (chunk(c), 0)),
            pl.BlockSpec((CHUNK, qk_w), lambda c: (chunk(c), 1)),
            pl.BlockSpec((CHUNK, ML_WIDTH), lambda c: (chunk(c), 2 * qk_w // ML_WIDTH)),
            pl.BlockSpec((CHUNK, CHUNK), lambda c: (chunk(c), d)),
        ]

    fwd = lambda c: c
    bwd = lambda c: nch - 1 - c
    n_state = 2 * ML_HEADS
    return pl.pallas_call(
        _mlstm_scan_kernel,
        grid=(nch,),
        in_specs=specs(fwd, 0) + specs(bwd, 1) + [
            pl.BlockSpec((2, 1, CHUNK), lambda c: (0, 0, 0)),
            pl.BlockSpec((2, CHUNK, CHUNK), lambda c: (0, 0, 0)),
        ],
        out_specs=[pl.BlockSpec((CHUNK, ML_WIDTH), lambda c: (c, 0)),
                   pl.BlockSpec((CHUNK, ML_WIDTH), lambda c: (nch - 1 - c, 0))],
        out_shape=[jax.ShapeDtypeStruct((lp, ML_WIDTH), F32), jax.ShapeDtypeStruct((lp, ML_WIDTH), F32)],
        scratch_shapes=[pltpu.VMEM((n_state, ML_QK_DIM, ML_V_DIM), F32), pltpu.VMEM((n_state, 1, ML_QK_DIM), F32),
                        pltpu.VMEM((n_state, 8, CHUNK), F32)],
        compiler_params=_cparams(("arbitrary",)),
        name="mlstm_scan",
    )(qkvo, qkvo, qkvo, gates, qkvo, qkvo, qkvo, gates, gate_bias_rows, tri2)


def _mlstm_final_kernel(hf_ref, hb_ref, o_ref_in, nw_ref, y_ref):
    h = hf_ref[...] + hb_ref[...]
    o = o_ref_in[...]
    for hd in range(ML_HEADS):
        sl = slice(hd * ML_V_DIM, (hd + 1) * ML_V_DIM)
        hh = h[:, sl]
        hh = hh * lax.rsqrt(jnp.mean(hh * hh, axis=-1, keepdims=True) + NORM_EPS) * nw_ref[:, sl]
        y_ref[:, sl] = (_sigmoid(o[:, sl]) * hh).astype(y_ref.dtype)


def mlstm_final(hf, hb, qkvo, head_norm):
    lp = qkvo.shape[0]
    ocol = (2 * ML_HEADS * ML_QK_DIM + ML_WIDTH) // ML_WIDTH
    return pl.pallas_call(
        _mlstm_final_kernel,
        grid=(lp // CHUNK,),
        in_specs=[
            pl.BlockSpec((CHUNK, ML_WIDTH), lambda c: (c, 0)),
            pl.BlockSpec((CHUNK, ML_WIDTH), lambda c: (c, 0)),
            pl.BlockSpec((CHUNK, ML_WIDTH), lambda c: (c, ocol)),
            pl.BlockSpec((1, ML_WIDTH), lambda c: (0, 0)),
        ],
        out_specs=pl.BlockSpec((CHUNK, ML_WIDTH), lambda c: (c, 0)),
        out_shape=jax.ShapeDtypeStruct((lp, ML_WIDTH), BF16),
        compiler_params=_cparams(("arbitrary",)),
        name="mlstm_final",
    )(hf, hb, qkvo, head_norm.reshape(1, ML_WIDTH))


def _dft_factors(n):
    best = None
    for n1 in range(8, n + 1, 8):
        if n % n1 == 0:
            n2 = n // n1
            if best is None or n1 + n2 < best[0] + best[1]:
                best = (n1, n2)
    assert best is not None
    return best


def _split_bf16(a):
    hi = a.astype(BF16)
    return hi, (a - hi.astype(F32)).astype(BF16)


def _dot3(a, b):
    return _bdot(a[0], b[0]) + _bdot(a[1], b[0]) + _bdot(a[0], b[1])


def _fnet_stage1_kernel(x_ref, c2h_ref, c2l_ref, s2h_ref, s2l_ref, twc_ref, tws_ref, tr_ref, ti_ref, *, n2):
    x = x_ref[...]
    rows = lax.broadcasted_iota(I32, x.shape, 0)
    xs = _split_bf16(jnp.where(rows < n2, x, 0.0))
    cx = _dot3((c2h_ref[...], c2l_ref[...]), xs)
    sx = _dot3((s2h_ref[...], s2l_ref[...]), xs)
    twc = twc_ref[0]
    tws = tws_ref[0]
    tr_ref[...] = cx * twc - sx * tws
    ti_ref[...] = -(cx * tws + sx * twc)


def _fnet_stage2_kernel(tr_ref, ti_ref, c1h_ref, c1l_ref, s1h_ref, s1l_ref, cch_ref, ccl_ref, sch_ref, scl_ref, o_ref,
                        *, scale):
    tr = _split_bf16(tr_ref[0])
    ti = _split_bf16(ti_ref[0])
    c1 = (c1h_ref[...], c1l_ref[...])
    s1 = (s1h_ref[...], s1l_ref[...])
    yr = _dot3(c1, tr) + _dot3(s1, ti)
    yi = _dot3(c1, ti) - _dot3(s1, tr)
    cc = (cch_ref[...], ccl_ref[...])
    sc = (sch_ref[...], scl_ref[...])
    for g in range(FN_GROUPS):
        sl = slice(g * FN_GROUP_DIM, (g + 1) * FN_GROUP_DIM)
        out = _dot3(_split_bf16(yr[:, sl]), cc) + _dot3(_split_bf16(yi[:, sl]), sc)
        o_ref[:, sl] = out * scale


def fourier_mixer(u_fn):
    n, ch = u_fn.shape
    n1, n2 = _dft_factors(n)
    n2p = -(-n2 // 8) * 8
    two_pi = 2.0 * np.pi

    def cs(size, pad):
        idx = np.arange(size)
        ang = two_pi * ((idx[:, None] * idx[None, :]) % size) / size
        out = []
        for fn in (np.cos, np.sin):
            m = np.zeros((pad, pad), np.float32)
            m[:size, :size] = fn(ang)
            hi = jnp.asarray(m).astype(BF16)
            out += [hi, (jnp.asarray(m) - hi.astype(F32)).astype(BF16)]
        return out

    dft2 = cs(n2, n2p)
    dft1 = cs(n1, n1)
    dftc = cs(FN_GROUP_DIM, FN_GROUP_DIM)
    tw_ang = two_pi * ((np.arange(n1)[:, None] * np.arange(n2)[None, :]) % n) / n
    twc = np.zeros((n1, n2p, 1), np.float32)
    tws = np.zeros((n1, n2p, 1), np.float32)
    twc[:, :n2, 0] = np.cos(tw_ang)
    tws[:, :n2, 0] = np.sin(tw_ang)

    def full(shape):
        return pl.BlockSpec(shape, lambda i: (0,) * len(shape))

    x2 = u_fn.reshape(n2, n1 * ch)
    tr, ti = pl.pallas_call(
        functools.partial(_fnet_stage1_kernel, n2=n2),
        grid=(n1,),
        in_specs=[pl.BlockSpec((n2p, ch), lambda i: (0, i))] + [full((n2p, n2p))] * 4 + [
            pl.BlockSpec((1, n2p, 1), lambda i: (i, 0, 0)),
            pl.BlockSpec((1, n2p, 1), lambda i: (i, 0, 0)),
        ],
        out_specs=[pl.BlockSpec((n2p, ch), lambda i: (0, i)), pl.BlockSpec((n2p, ch), lambda i: (0, i))],
        out_shape=[jax.ShapeDtypeStruct((n2p, n1 * ch), F32), jax.ShapeDtypeStruct((n2p, n1 * ch), F32)],
        compiler_params=_cparams(("arbitrary",)),
        name="fnet_stage1",
    )(x2, *dft2, jnp.asarray(twc), jnp.asarray(tws))

    tr3 = tr.reshape(n2p, n1, ch)
    ti3 = ti.reshape(n2p, n1, ch)
    scale = 1.0 / math.sqrt(float(n) * FN_GROUP_DIM)
    out = pl.pallas_call(
        functools.partial(_fnet_stage2_kernel, scale=scale),
        grid=(n2,),
        in_specs=[
            pl.BlockSpec((1, n1, ch), lambda i: (i, 0, 0)),
            pl.BlockSpec((1, n1, ch), lambda i: (i, 0, 0)),
        ] + [full((n1, n1))] * 4 + [full((FN_GROUP_DIM, FN_GROUP_DIM))] * 4,
        out_specs=pl.BlockSpec((n1, ch), lambda i: (0, i)),
        out_shape=jax.ShapeDtypeStruct((n1, n2 * ch), F32),
        compiler_params=_cparams(("arbitrary",)),
        name="fnet_stage2",
    )(tr3, ti3, *dft1, *dftc)
    return out.reshape(n, ch)


def _cap_sizes(lp):
    n_tok = lp - PAD
    cap = EC_FACTOR * n_tok // N_EXPERTS
    capp = -(-cap // 16) * 16
    return cap, capp


def _topk_kernel(lg_ref, su_ref, pos_ref, aff_ref, start_ref, idx_ref, rank_scr, pos_scr, *, cap, capp, jb):
    ne, lp = lg_ref.shape
    nch = lp // CHUNK
    lg = lg_ref[...]
    mx = jnp.max(lg, axis=0, keepdims=True)
    ex = jnp.exp(lg - mx)
    aff = ex / jnp.sum(ex, axis=0, keepdims=True)
    lane = lax.broadcasted_iota(I32, (ne, lp), 1)
    real = lane >= PAD
    bits = jnp.where(real, pltpu.bitcast(aff, I32), -1)
    capf = float(cap)

    def bisect(i, thr):
        cand = thr | jnp.left_shift(jnp.int32(1), 30 - i)
        cnt = jnp.sum(jnp.where(bits >= cand, 1.0, 0.0), axis=1, keepdims=True)
        return jnp.where(cnt >= capf, cand, thr)

    thr = lax.fori_loop(0, 31, bisect, jnp.zeros((ne, 1), I32))
    gt = bits > thr
    eq = bits == thr
    need = capf - jnp.sum(jnp.where(gt, 1.0, 0.0), axis=1, keepdims=True)

    su = su_ref[...]

    def excl_cumsum(mask_f32, out_scr, want_starts):
        carry = jnp.zeros((ne, 1), F32)
        starts = jnp.zeros((ne, CHUNK), F32)
        lane_c = lax.broadcasted_iota(I32, (ne, CHUNK), 1)
        for c in range(nch):
            tile = mask_f32[:, c * CHUNK:(c + 1) * CHUNK]
            within = jnp.dot(tile.astype(BF16), su, preferred_element_type=F32)
            out_scr[:, c * CHUNK:(c + 1) * CHUNK] = within + carry
            if want_starts:
                starts = jnp.where(lane_c == c, carry, starts)
            carry = carry + jnp.sum(tile, axis=1, keepdims=True)
        if want_starts:
            starts = jnp.where(lane_c == nch, carry, starts)
        return starts

    excl_cumsum(jnp.where(eq, 1.0, 0.0), rank_scr, False)
    sel = gt | (eq & (rank_scr[...] < need))
    starts = excl_cumsum(jnp.where(sel, 1.0, 0.0), pos_scr, True)
    posf = jnp.where(sel, pos_scr[...], -1.0)
    pos_scr[...] = posf
    pos_ref[...] = posf.astype(I32)
    aff_ref[...] = jnp.where(sel, aff, 0.0)
    start_ref[...] = starts.astype(I32)

    tok = lax.broadcasted_iota(I32, (jb, lp), 1).astype(F32)
    lane_e = lax.broadcasted_iota(I32, (jb, ne), 1)
    for blk in range(capp // jb):
        jcol = (blk * jb + lax.broadcasted_iota(I32, (jb, 1), 0)).astype(F32)

        def per_expert(e, acc):
            prow = pos_scr[pl.ds(e, 1), :]
            hit = jnp.sum(jnp.where(prow == jcol, tok, 0.0), axis=1, keepdims=True)
            return jnp.where(lane_e == e, hit, acc)

        acc = lax.fori_loop(0, ne, per_expert, jnp.zeros((jb, ne), F32))
        idx_ref[blk * jb:(blk + 1) * jb, :] = acc.astype(I32)


def topk_route(logits_t):
    ne, lp = logits_t.shape
    cap, capp = _cap_sizes(lp)
    jb = 80 if capp % 80 == 0 else 16
    su = jnp.asarray(np.triu(np.ones((CHUNK, CHUNK), np.float32), 1), BF16)
    return pl.pallas_call(
        functools.partial(_topk_kernel, cap=cap, capp=capp, jb=jb),
        out_shape=[jax.ShapeDtypeStruct((ne, lp), I32), jax.ShapeDtypeStruct((ne, lp), F32),
                   jax.ShapeDtypeStruct((ne, CHUNK), I32), jax.ShapeDtypeStruct((capp, ne), I32)],
        scratch_shapes=[pltpu.VMEM((ne, lp), F32), pltpu.VMEM((ne, lp), F32)],
        compiler_params=pltpu.CompilerParams(vmem_limit_bytes=VMEM_LIMIT),
        name="topk_route",
    )(logits_t, su)


GATHER_UNROLL = 8


def _gather_kernel(idx_ref, u_hbm, o_ref, stage_ref, sem):
    e = pl.program_id(0)
    ne = pl.num_programs(0)
    capp = stage_ref.shape[1]

    def issue(eq, slot):
        def body(jo, carry):
            for ji in range(GATHER_UNROLL):
                j = jo * GATHER_UNROLL + ji
                t = idx_ref[eq, j]
                pltpu.make_async_copy(u_hbm.at[pl.ds(t, 1), :], stage_ref.at[slot, pl.ds(j, 1), :],
                                      sem.at[slot]).start()
            return carry

        lax.fori_loop(0, capp // GATHER_UNROLL, body, 0)

    @pl.when(e == 0)
    def _():
        issue(e, 0)

    @pl.when(e + 1 < ne)
    def _():
        issue(e + 1, (e + 1) % 2)

    slot = e % 2
    pltpu.make_async_copy(u_hbm.at[pl.ds(0, capp), :], stage_ref.at[slot], sem.at[slot]).wait()
    o_ref[0] = stage_ref[slot].astype(o_ref.dtype)


def gather_tokens(idx, u):
    ne, capp = idx.shape
    lp, d = u.shape
    assert capp % GATHER_UNROLL == 0
    return pl.pallas_call(
        _gather_kernel,
        grid_spec=pltpu.PrefetchScalarGridSpec(
            num_scalar_prefetch=1,
            grid=(ne,),
            in_specs=[pl.BlockSpec(memory_space=pl.ANY)],
            out_specs=pl.BlockSpec((1, capp, d), lambda e, *_: (e, 0, 0)),
            scratch_shapes=[pltpu.VMEM((2, capp, d), F32), pltpu.SemaphoreType.DMA((2,))],
        ),
        out_shape=jax.ShapeDtypeStruct((ne, capp, d), BF16),
        compiler_params=_cparams(("arbitrary",)),
        name="moe_gather",
    )(idx, u)


def _ffn_kernel(xs_ref, wg_ref, wu_ref, wd_ref, o_ref, hdn_ref, *, n_ff):
    s = pl.program_id(1)

    @pl.when(s < n_ff)
    def _():
        x = xs_ref[0]
        g = jnp.dot(x, wg_ref[0, 0].astype(BF16), preferred_element_type=F32)
        u = jnp.dot(x, wu_ref[0, 0].astype(BF16), preferred_element_type=F32)
        hdn_ref[s] = (g * _sigmoid(g) * u).astype(BF16)

    @pl.when(s >= n_ff)
    def _():
        wd = wd_ref[0, 0].astype(BF16)
        acc = None
        for j in range(n_ff):
            part = jnp.dot(hdn_ref[j], wd[j * FF_TILE:(j + 1) * FF_TILE, :], preferred_element_type=F32)
            acc = part if acc is None else acc + part
        o_ref[0] = acc


def expert_ffn(xs, w_gate, w_up, w_down, layer):
    ne, capp, d = xs.shape
    n_ff = EXPERT_FF // FF_TILE
    n_out = d // OUT_TILE
    return pl.pallas_call(
        functools.partial(_ffn_kernel, n_ff=n_ff),
        grid=(ne, n_ff + n_out),
        in_specs=[
            pl.BlockSpec((1, capp, d), lambda e, s: (e, 0, 0)),
            pl.BlockSpec((1, 1, d, FF_TILE), lambda e, s: (layer, e, 0, jnp.minimum(s, n_ff - 1))),
            pl.BlockSpec((1, 1, d, FF_TILE), lambda e, s: (layer, e, 0, jnp.minimum(s, n_ff - 1))),
            pl.BlockSpec((1, 1, EXPERT_FF, OUT_TILE), lambda e, s: (layer, e, 0, jnp.maximum(s - n_ff, 0))),
        ],
        out_specs=pl.BlockSpec((1, capp, OUT_TILE), lambda e, s: (e, 0, jnp.maximum(s - n_ff, 0))),
        out_shape=jax.ShapeDtypeStruct((ne, capp, d), F32),
        scratch_shapes=[pltpu.VMEM((n_ff, capp, FF_TILE), BF16)],
        compiler_params=_cparams(("arbitrary", "arbitrary")),
        name="expert_ffn",
    )(xs, w_gate, w_up, w_down)


def _combine_kernel(start_ref, h_ref, pos_ref, aff_ref, out_hbm, o_ref, g_ref, sem, *, capp, rows, n_half):
    b = pl.program_id(0)
    hh = pl.program_id(1)
    nb = pl.num_programs(0)
    ne = pos_ref.shape[0]
    eh = ne // n_half

    def row_copy(src, dst, slot):
        return pltpu.make_async_copy(out_hbm.at[pl.ds(src, 1), :], g_ref.at[slot, pl.ds(dst, 1), :], sem.at[slot])

    def block_rows(bq, hq):
        s0 = [start_ref[hq * eh + i, bq] for i in range(eh)]
        n = [start_ref[hq * eh + i, bq + 1] - s0[i] for i in range(eh)]
        return s0, n

    def issue(bq, hq, slot):
        s0, n = block_rows(bq, hq)
        off = jnp.int32(0)
        for i in range(eh):
            base = (hq * eh + i) * capp + s0[i]

            def body(j, carry, base=base, off=off):
                row_copy(base + j, off + j, slot).start()
                return carry

            lax.fori_loop(0, n[i], body, 0)
            off = off + n[i]

    @pl.when((b == 0) & (hh == 0))
    def _():
        g_ref[...] = jnp.zeros(g_ref.shape, F32)
        issue(b, hh, 0)

    @pl.when(hh + 1 < n_half)
    def _():
        issue(b, hh + 1, 1 - hh % 2)

    if n_half % 2 == 0:
        @pl.when((hh + 1 == n_half) & (b + 1 < nb))
        def _():
            issue(b + 1, 0, 0)

    slot = hh % 2
    s0, n = block_rows(b, hh)
    offs = []
    off = jnp.int32(0)
    for i in range(eh):
        offs.append(off)
        off = off + n[i]
    total = off

    def drain(j, carry):
        row_copy(0, j, slot).wait()
        return carry

    lax.fori_loop(0, total, drain, 0)

    @pl.when(hh == 0)
    def _():
        o_ref[...] = h_ref[...]

    r_iota = lax.broadcasted_iota(I32, (rows, CHUNK), 0)
    d = o_ref.shape[1]

    def chunk(ci, carry):
        r0 = pl.multiple_of(ci * rows, rows)
        rr = r_iota + r0
        onehot = jnp.zeros((rows, CHUNK), F32)
        gate = jnp.zeros((rows, CHUNK), F32)
        for i in range(eh):
            prow = pos_ref[pl.ds(hh * eh + i, 1), :]
            arow = aff_ref[pl.ds(hh * eh + i, 1), :]
            tgt = jnp.where(prow >= 0, prow - s0[i] + offs[i], -1)
            hit = rr == tgt
            onehot = jnp.where(hit, 1.0, onehot)
            gate = jnp.where(hit, arow, gate)
        gcol = jnp.sum(gate, axis=1, keepdims=True)
        oh = onehot.T.astype(BF16)
        for c0 in range(0, d, COMBINE_SLAB):
            gs = (g_ref[slot, pl.ds(r0, rows), c0:c0 + COMBINE_SLAB] * gcol).astype(BF16)
            o_ref[:, c0:c0 + COMBINE_SLAB] += jnp.dot(oh, gs, preferred_element_type=F32)
        return carry

    lax.fori_loop(0, (total + rows - 1) // rows, chunk, 0)


def moe_combine(h, pos, aff, starts, ffn_out):
    lp, d = h.shape
    ne, capp, _ = ffn_out.shape
    rows = 2 * CHUNK
    n_half = 2
    stage_rows = (ne // n_half) * CHUNK
    return pl.pallas_call(
        functools.partial(_combine_kernel, capp=capp, rows=rows, n_half=n_half),
        grid_spec=pltpu.PrefetchScalarGridSpec(
            num_scalar_prefetch=1,
            grid=(lp // CHUNK, n_half),
            in_specs=[
                pl.BlockSpec((CHUNK, d), lambda b, hh, *_: (b, 0)),
                pl.BlockSpec((ne, CHUNK), lambda b, hh, *_: (0, b)),
                pl.BlockSpec((ne, CHUNK), lambda b, hh, *_: (0, b)),
                pl.BlockSpec(memory_space=pl.ANY),
            ],
            out_specs=pl.BlockSpec((CHUNK, d), lambda b, hh, *_: (b, 0)),
            scratch_shapes=[pltpu.VMEM((2, stage_rows, d), F32), pltpu.SemaphoreType.DMA((2,))],
        ),
        out_shape=jax.ShapeDtypeStruct((lp, d), F32),
        compiler_params=_cparams(("arbitrary", "arbitrary")),
        name="moe_combine",
    )(starts, h, pos, aff, ffn_out.reshape(ne * capp, d))


def moe_layer(h, norm_w, w_router, w_gate, w_up, w_down, layer):
    u, logits_t = rmsnorm_router(h, norm_w, w_router)
    pos, aff, starts, idx_t = topk_route(logits_t)
    xs = gather_tokens(idx_t.T, u)
    ffn_out = expert_ffn(xs, w_gate, w_up, w_down, layer)
    return moe_combine(h, pos, aff, starts, ffn_out)


def _scan_masks():
    lower = np.tril(np.ones((CHUNK, CHUNK), np.float32))
    return jnp.asarray(np.stack([lower, lower.T]))


def _pad_cols(w, width):
    return jnp.pad(w, ((0, 0), (0, width - w.shape[1])))


def ab_mixer(h, norm_w, w_in, conv_w, conv_b, a_log, dt_bias, d_skip, ssd_norm, sink, w_out):
    u = rmsnorm(h, norm_w, BF16)
    zx_w = SSD_INNER + SSD_CONV_CH
    dt0 = zx_w
    qkv0 = zx_w + 2 * SSD_HEADS
    z = matmul([u], w_in, 512, col0=0, n=SSD_INNER)
    xbc = matmul([u], w_in, 512, col0=SSD_INNER, n=SSD_CONV_CH)
    w_dt = jnp.concatenate([_pad_cols(w_in[:, dt0:dt0 + SSD_HEADS], CHUNK),
                            _pad_cols(w_in[:, dt0 + SSD_HEADS:qkv0], CHUNK)], axis=1)
    dt_raw = matmul([u], w_dt, 2 * CHUNK)
    qkv = matmul([u], w_in[:, qkv0:], 512)

    xbc_act = ssd_conv(xbc, conv_w, conv_b)
    dt_bias_rows = _pad_cols(dt_bias, CHUNK).reshape(2, 1, CHUNK)
    acoef_rows = _pad_cols(-jnp.exp(a_log), CHUNK).reshape(2, 1, CHUNK)
    yf, yb = ssd_scan(xbc_act, dt_raw, dt_bias_rows, acoef_rows, _scan_masks())
    d_skip_row = jnp.repeat(d_skip, SSD_HEAD_DIM).reshape(1, SSD_INNER)
    y_ssd = ssd_final(yf, yb, xbc_act, z, d_skip_row, ssd_norm)

    slopes = 2.0 ** (-8.0 * jnp.arange(1, ATT_Q_HEADS + 1, dtype=F32) / ATT_Q_HEADS)
    y_att = window_attention(qkv, slopes, sink)
    return matmul([y_ssd, y_att], w_out, 512, res=h)


def cd_mixer(h, norm_w, w_in, i_bias, f_bias, head_norm, w_out):
    lp = h.shape[0]
    u = rmsnorm(h, norm_w, BF16)
    qkvo_w = 2 * ML_HEADS * ML_QK_DIM + 2 * ML_WIDTH
    i0 = qkvo_w
    f0 = i0 + 2 * ML_HEADS
    fn0 = f0 + 2 * ML_HEADS
    qkvo = matmul([u], w_in, 512, col0=0, n=qkvo_w)

    def gate_block(dirn):
        wi = w_in[:, i0 + dirn * ML_HEADS:i0 + (dirn + 1) * ML_HEADS]
        wf = w_in[:, f0 + dirn * ML_HEADS:f0 + (dirn + 1) * ML_HEADS]
        return _pad_cols(jnp.concatenate([_pad_cols(wi, ML_F_LANE), wf], axis=1), CHUNK)

    def bias_block(dirn):
        row = jnp.concatenate([jnp.pad(i_bias[dirn], (0, ML_F_LANE - ML_HEADS)), f_bias[dirn]])
        return jnp.pad(row, (0, CHUNK - row.shape[0]))

    gates = matmul([u], jnp.concatenate([gate_block(0), gate_block(1)], axis=1), 2 * CHUNK)
    gate_bias_rows = jnp.stack([bias_block(0), bias_block(1)]).reshape(2, 1, CHUNK)
    hf, hb = mlstm_scan(qkvo, gates, gate_bias_rows, _scan_masks())
    y_ml = mlstm_final(hf, hb, qkvo, head_norm)

    u_fn = matmul([u], w_in[:, fn0:], 512)
    y_fn = fourier_mixer(u_fn[PAD:])
    y_fn = jnp.pad(y_fn, ((PAD, 0), (0, 0))).astype(BF16)
    return matmul([y_ml, y_fn], w_out, 512, res=h)


def kernel(x, meta_tokens, norm_mix, ab_w_in, ab_conv_w, ab_conv_b, ab_a_log, ab_dt_bias, ab_d_skip, ab_ssd_norm,
           ab_sink, ab_w_out, cd_w_in, cd_i_bias, cd_f_bias, cd_head_norm, cd_w_out, norm_ffn, moe_router,
           moe_w_gate, moe_w_up, moe_w_down, final_norm):
    bsz, seq, d = x.shape
    assert bsz == 1 and d == D_MODEL and seq % CHUNK == 0
    depth = norm_mix.shape[0]
    h = jnp.concatenate([jnp.zeros((PAD, d), F32), meta_tokens.astype(F32), x[0]], axis=0)
    for layer in range(depth):
        j = layer // 2
        if layer % 2 == 0:
            h = ab_mixer(h, norm_mix[layer], ab_w_in[j], ab_conv_w[j], ab_conv_b[j], ab_a_log[j], ab_dt_bias[j],
                         ab_d_skip[j], ab_ssd_norm[j], ab_sink[j], ab_w_out[j])
        else:
            h = cd_mixer(h, norm_mix[layer], cd_w_in[j], cd_i_bias[j], cd_f_bias[j], cd_head_norm[j], cd_w_out[j])
        h = moe_layer(h, norm_ffn[layer], moe_router[layer], moe_w_gate, moe_w_up, moe_w_down, layer)
    return final_rmsnorm(h, final_norm)[None]
```

```python
import functools
import math

import numpy as np
import jax
import jax.numpy as jnp
from jax import lax
from jax.experimental import pallas as pl
from jax.experimental.pallas import tpu as pltpu

F32 = jnp.float32
BF16 = jnp.bfloat16
I32 = jnp.int32
HIGHEST = lax.Precision.HIGHEST

D_MODEL = 4096
N_META = 16
CHUNK = 128
PAD = CHUNK - N_META
NORM_EPS = 1e-6
NEG = -1e30

SSD_HEADS = 32
SSD_HEAD_DIM = 64
SSD_INNER = SSD_HEADS * SSD_HEAD_DIM
SSD_GROUPS = 4
SSD_STATE = 128
SSD_CONV = 5
SSD_CONV_CH = SSD_INNER + 2 * SSD_GROUPS * SSD_STATE

ATT_Q_HEADS = 16
ATT_KV_HEADS = 4
ATT_HEAD_DIM = 128
ATT_WINDOW = 128
ATT_GRP = ATT_Q_HEADS // ATT_KV_HEADS

ML_HEADS = 6
ML_QK_DIM = 256
ML_V_DIM = 512
ML_WIDTH = ML_HEADS * ML_V_DIM
ML_F_LANE = 8

FN_GROUPS = 4
FN_GROUP_DIM = 256
FN_WIDTH = FN_GROUPS * FN_GROUP_DIM

N_EXPERTS = 16
EC_FACTOR = 2
EXPERT_FF = 1536
FF_TILE = 256
OUT_TILE = 512
COMBINE_SLAB = 512

VMEM_LIMIT = 56 * 1024 * 1024


def _cparams(sem):
    return pltpu.CompilerParams(dimension_semantics=sem, vmem_limit_bytes=VMEM_LIMIT)


def _row_tile(lp):
    return 640 if lp % 640 == 0 else CHUNK


def _sigmoid(x):
    return 1.0 / (1.0 + jnp.exp(-x))


def _softplus(x):
    return jnp.maximum(x, 0.0) + jnp.log1p(jnp.exp(-jnp.abs(x)))


NT = (((1,), (1,)), ((), ()))


def _bdot(a, b, dims=None):
    if dims is None:
        return jnp.dot(a, b, preferred_element_type=F32)
    return lax.dot_general(a, b, dims, preferred_element_type=F32)


def _rmsnorm_kernel(h_ref, w_ref, o_ref):
    x = h_ref[...]
    y = x * lax.rsqrt(jnp.mean(x * x, axis=-1, keepdims=True) + NORM_EPS) * w_ref[...]
    o_ref[...] = y.astype(o_ref.dtype)


def rmsnorm(h, w, out_dtype):
    lp, d = h.shape
    tm = _row_tile(lp)
    return pl.pallas_call(
        _rmsnorm_kernel,
        grid=(lp // tm,),
        in_specs=[pl.BlockSpec((tm, d), lambda i: (i, 0)), pl.BlockSpec((1, d), lambda i: (0, 0))],
        out_specs=pl.BlockSpec((tm, d), lambda i: (i, 0)),
        out_shape=jax.ShapeDtypeStruct((lp, d), out_dtype),
        compiler_params=_cparams(("arbitrary",)),
        name="rmsnorm",
    )(h, w.reshape(1, d))


def _rmsnorm_router_kernel(h_ref, w_ref, wrt_ref, u_ref, lg_ref):
    x = h_ref[...]
    y = x * lax.rsqrt(jnp.mean(x * x, axis=-1, keepdims=True) + NORM_EPS) * w_ref[...]
    u_ref[...] = y
    lg_ref[...] = lax.dot_general(wrt_ref[...], y, (((1,), (1,)), ((), ())), precision=HIGHEST,
                                  preferred_element_type=F32)


def rmsnorm_router(h, w, w_router):
    lp, d = h.shape
    tm = _row_tile(lp)
    return pl.pallas_call(
        _rmsnorm_router_kernel,
        grid=(lp // tm,),
        in_specs=[pl.BlockSpec((tm, d), lambda i: (i, 0)), pl.BlockSpec((1, d), lambda i: (0, 0)),
                  pl.BlockSpec((N_EXPERTS, d), lambda i: (0, 0))],
        out_specs=[pl.BlockSpec((tm, d), lambda i: (i, 0)), pl.BlockSpec((N_EXPERTS, tm), lambda i: (0, i))],
        out_shape=[jax.ShapeDtypeStruct((lp, d), F32), jax.ShapeDtypeStruct((N_EXPERTS, lp), F32)],
        compiler_params=_cparams(("arbitrary",)),
        name="rmsnorm_router",
    )(h, w.reshape(1, d), w_router.T)


def _matmul_kernel(*refs, k_splits, has_res):
    n_x = len(k_splits)
    x_refs = refs[:n_x]
    w_ref = refs[n_x]
    r_ref = refs[n_x + 1] if has_res else None
    o_ref = refs[n_x + 1 + int(has_res)]
    wb_ref = refs[n_x + 2 + int(has_res)]

    @pl.when(pl.program_id(1) == 0)
    def _():
        wb_ref[...] = w_ref[...].astype(BF16)

    acc = None
    k0 = 0
    for x_ref, kk in zip(x_refs, k_splits):
        part = jnp.dot(x_ref[...], wb_ref[k0:k0 + kk, :], preferred_element_type=F32)
        acc = part if acc is None else acc + part
        k0 += kk
    if has_res:
        acc = acc + r_ref[...]
    o_ref[...] = acc.astype(o_ref.dtype)


def matmul(xs, w, tn, res=None, out_dtype=F32, col0=0, n=None):
    lp = xs[0].shape[0]
    k_splits = tuple(int(x.shape[1]) for x in xs)
    k = w.shape[0]
    n = w.shape[1] - col0 if n is None else n
    assert sum(k_splits) == k and n % tn == 0 and col0 % tn == 0
    cb0 = col0 // tn
    tm = _row_tile(lp)
    in_specs = [pl.BlockSpec((tm, kk), lambda j, i: (i, 0)) for kk in k_splits]
    in_specs.append(pl.BlockSpec((k, tn), lambda j, i: (0, cb0 + j)))
    args = list(xs) + [w]
    if res is not None:
        in_specs.append(pl.BlockSpec((tm, tn), lambda j, i: (i, j)))
        args.append(res)
    return pl.pallas_call(
        functools.partial(_matmul_kernel, k_splits=k_splits, has_res=res is not None),
        grid=(n // tn, lp // tm),
        in_specs=in_specs,
        out_specs=pl.BlockSpec((tm, tn), lambda j, i: (i, j)),
        out_shape=jax.ShapeDtypeStruct((lp, n), out_dtype),
        scratch_shapes=[pltpu.VMEM((k, tn), BF16)],
        compiler_params=_cparams(("arbitrary", "arbitrary")),
        name="matmul",
    )(*args)


def _ssd_conv_kernel(cur_ref, prev_ref, next_ref, w_ref, b_ref, o_ref, ext_ref):
    c = pl.program_id(0)
    nch = pl.num_programs(0)
    halo = 8
    ext_ref[0:halo, :] = jnp.where(c > 0, prev_ref[...], 0.0)
    ext_ref[halo:halo + CHUNK, :] = cur_ref[...]
    ext_ref[halo + CHUNK:2 * halo + CHUNK, :] = jnp.where(c < nch - 1, next_ref[...], 0.0)
    w = w_ref[...]
    acc = jnp.zeros(cur_ref.shape, F32) + b_ref[...]
    centre = (SSD_CONV - 1) // 2
    for j in range(SSD_CONV):
        acc = acc + w[j:j + 1, :] * ext_ref[pl.ds(halo - centre + j, CHUNK), :]
    y = acc * _sigmoid(acc)
    row = c * CHUNK + lax.broadcasted_iota(I32, y.shape, 0)
    o_ref[...] = jnp.where(row >= PAD, y, 0.0)


def ssd_conv(xbc, conv_w, conv_b):
    lp, ch = xbc.shape
    nch = lp // CHUNK
    sub = CHUNK // 8
    return pl.pallas_call(
        _ssd_conv_kernel,
        grid=(nch,),
        in_specs=[
            pl.BlockSpec((CHUNK, ch), lambda c: (c, 0)),
            pl.BlockSpec((8, ch), lambda c: (jnp.maximum(c * sub - 1, 0), 0)),
            pl.BlockSpec((8, ch), lambda c: (jnp.minimum((c + 1) * sub, nch * sub - 1), 0)),
            pl.BlockSpec((SSD_CONV, ch), lambda c: (0, 0)),
            pl.BlockSpec((1, ch), lambda c: (0, 0)),
        ],
        out_specs=pl.BlockSpec((CHUNK, ch), lambda c: (c, 0)),
        out_shape=jax.ShapeDtypeStruct((lp, ch), F32),
        scratch_shapes=[pltpu.VMEM((CHUNK + 16, ch), F32)],
        compiler_params=_cparams(("arbitrary",)),
        name="ssd_conv",
    )(xbc, xbc, xbc, conv_w, conv_b.reshape(1, ch))


def _ssd_direction(d, ce, x_ref, b_ref, c_ref, dt_ref, dtb_ref, acoef_ref, tri_ref, y_ref, state_ref):
    tri = tri_ref[d]
    trib = tri > 0.5
    row = ce * CHUNK + lax.broadcasted_iota(I32, (CHUNK, CHUNK), 0)
    dt = jnp.where(row >= PAD, _softplus(dt_ref[...] + dtb_ref[d]), 0.0)
    a = dt * acoef_ref[d]
    a_cum = jnp.dot(tri, a, precision=HIGHEST, preferred_element_type=F32)
    a_cum_t = a_cum.T
    a_tot = jnp.sum(a, axis=0, keepdims=True)
    e_cum = jnp.exp(a_cum)
    d_out = jnp.exp(a_tot - a_cum)
    e_tot = jnp.exp(a_tot)
    lane = lax.broadcasted_iota(I32, (CHUNK, CHUNK), 1)
    first = lane < SSD_HEAD_DIM
    first_row = first[0:1, :]

    def pair_cols(m, h0):
        return jnp.where(first, m[:, h0:h0 + 1], m[:, h0 + 1:h0 + 2])

    hpg = SSD_HEADS // SSD_GROUPS
    for g in range(SSD_GROUPS):
        bg = b_ref[:, g * SSD_STATE:(g + 1) * SSD_STATE]
        cg = c_ref[:, g * SSD_STATE:(g + 1) * SSD_STATE].astype(BF16)
        cb = _bdot(cg, bg.astype(BF16), NT)
        bt = bg.T.astype(BF16)
        for pp in range(hpg // 2):
            h0 = g * hpg + 2 * pp
            p_idx = d * (SSD_HEADS // 2) + h0 // 2
            lo = h0 * SSD_HEAD_DIM
            x_p = x_ref[:, lo:lo + 2 * SSD_HEAD_DIM]
            xdt = x_p * pair_cols(dt, h0)
            xw = (xdt * pair_cols(d_out, h0)).astype(BF16)
            xdt_b = xdt.astype(BF16)
            yd = []
            for hh in (h0, h0 + 1):
                diff = a_cum[:, hh:hh + 1] - a_cum_t[hh:hh + 1, :]
                dec = jnp.exp(jnp.where(trib, diff, NEG))
                yd.append(_bdot((cb * dec).astype(BF16), xdt_b))
            y_diag = jnp.where(first, yd[0], yd[1])
            st_prev = state_ref[p_idx]
            y_off = _bdot(cg, st_prev.astype(BF16)) * pair_cols(e_cum, h0)
            y_ref[:, lo:lo + 2 * SSD_HEAD_DIM] = y_diag + y_off
            st_new = _bdot(bt, xw)
            cdec = jnp.where(first_row, e_tot[:, h0:h0 + 1], e_tot[:, h0 + 1:h0 + 2])
            state_ref[p_idx] = st_prev * cdec + st_new


def _ssd_scan_kernel(xf_ref, bf_ref, cf_ref, dtf_ref, xb_ref, bb_ref, cb_ref, dtbk_ref, dtb_ref, acoef_ref, tri_ref,
                     yf_ref, yb_ref, state_ref):
    c = pl.program_id(0)
    nch = pl.num_programs(0)

    @pl.when(c == 0)
    def _():
        state_ref[...] = jnp.zeros(state_ref.shape, F32)

    _ssd_direction(0, c, xf_ref, bf_ref, cf_ref, dtf_ref, dtb_ref, acoef_ref, tri_ref, yf_ref, state_ref)
    _ssd_direction(1, nch - 1 - c, xb_ref, bb_ref, cb_ref, dtbk_ref, dtb_ref, acoef_ref, tri_ref, yb_ref, state_ref)


def ssd_scan(xbc_act, dt_raw, dt_bias_rows, acoef_rows, tri2):
    lp = xbc_act.shape[0]
    nch = lp // CHUNK
    gs = SSD_GROUPS * SSD_STATE

    def specs(chunk, d):
        return [
            pl.BlockSpec((CHUNK, SSD_INNER), lambda c: (chunk(c), 0)),
            pl.BlockSpec((CHUNK, gs), lambda c: (chunk(c), SSD_INNER // gs)),
            pl.BlockSpec((CHUNK, gs), lambda c: (chunk(c), SSD_INNER // gs + 1)),
            pl.BlockSpec((CHUNK, CHUNK), lambda c: (chunk(c), d)),
        ]

    fwd = lambda c: c
    bwd = lambda c: nch - 1 - c
    return pl.pallas_call(
        _ssd_scan_kernel,
        grid=(nch,),
        in_specs=specs(fwd, 0) + specs(bwd, 1) + [
            pl.BlockSpec((2, 1, CHUNK), lambda c: (0, 0, 0)),
            pl.BlockSpec((2, 1, CHUNK), lambda c: (0, 0, 0)),
            pl.BlockSpec((2, CHUNK, CHUNK), lambda c: (0, 0, 0)),
        ],
        out_specs=[pl.BlockSpec((CHUNK, SSD_INNER), lambda c: (c, 0)),
                   pl.BlockSpec((CHUNK, SSD_INNER), lambda c: (nch - 1 - c, 0))],
        out_shape=[jax.ShapeDtypeStruct((lp, SSD_INNER), F32), jax.ShapeDtypeStruct((lp, SSD_INNER), F32)],
        scratch_shapes=[pltpu.VMEM((SSD_HEADS, SSD_STATE, 2 * SSD_HEAD_DIM), F32)],
        compiler_params=_cparams(("arbitrary",)),
        name="ssd_scan",
    )(xbc_act, xbc_act, xbc_act, dt_raw, xbc_act, xbc_act, xbc_act, dt_raw, dt_bias_rows, acoef_rows, tri2)


def _ssd_final_kernel(yf_ref, yb_ref, x_ref, z_ref, dskip_ref, nw_ref, o_ref):
    y = yf_ref[...] + yb_ref[...] + dskip_ref[...] * x_ref[...]
    z = z_ref[...]
    y = y * (z * _sigmoid(z))
    gw = SSD_INNER // SSD_GROUPS
    for g in range(SSD_GROUPS):
        yg = y[:, g * gw:(g + 1) * gw]
        yg = yg * lax.rsqrt(jnp.mean(yg * yg, axis=-1, keepdims=True) + NORM_EPS)
        o_ref[:, g * gw:(g + 1) * gw] = (yg * nw_ref[:, g * gw:(g + 1) * gw]).astype(o_ref.dtype)


def ssd_final(yf, yb, xbc_act, z, d_skip_row, norm_w):
    lp = xbc_act.shape[0]
    blk = pl.BlockSpec((CHUNK, SSD_INNER), lambda c: (c, 0))
    row = pl.BlockSpec((1, SSD_INNER), lambda c: (0, 0))
    return pl.pallas_call(
        _ssd_final_kernel,
        grid=(lp // CHUNK,),
        in_specs=[blk, blk, blk, blk, row, row],
        out_specs=blk,
        out_shape=jax.ShapeDtypeStruct((lp, SSD_INNER), BF16),
        compiler_params=_cparams(("arbitrary",)),
        name="ssd_final",
    )(yf, yb, xbc_act, z, d_skip_row, norm_w.reshape(1, SSD_INNER))


def _attn_kernel(slope_ref, sink_ref, q_ref, km_ref, kp_ref, ks_ref, kn_ref, vm_ref, vp_ref, vs_ref, vn_ref, o_ref):
    b = pl.program_id(0)
    nb = pl.num_programs(0)
    scale = ATT_HEAD_DIM ** -0.5

    qpos = b * CHUNK - PAD + lax.broadcasted_iota(I32, (CHUNK, 4 * CHUNK), 0)
    col = lax.broadcasted_iota(I32, (CHUNK, 4 * CHUNK), 1)
    slot = col // CHUNK
    j = col - slot * CHUNK
    kblock = jnp.where(slot == 0, 0, b - 2 + slot)
    kpos = kblock * CHUNK - PAD + j
    dist = jnp.abs(qpos - kpos)
    meta_ok = (slot == 0) & (kpos >= 0)
    real_ok = (slot > 0) & (kblock >= 1) & (kblock < nb) & (kpos >= N_META) & (dist <= ATT_WINDOW)
    valid = meta_ok | real_ok
    alibi = jnp.where(real_ok & (qpos >= N_META), dist.astype(F32), 0.0)
    q_real = qpos[:, 0:ATT_HEAD_DIM] >= 0

    for hk in range(ATT_KV_HEADS):
        ksl = slice(hk * ATT_HEAD_DIM, (hk + 1) * ATT_HEAD_DIM)
        kcat = jnp.concatenate([km_ref[:, ksl], kp_ref[:, ksl], ks_ref[:, ksl], kn_ref[:, ksl]],
                               axis=0).astype(BF16)
        vcat = jnp.concatenate([vm_ref[:, ksl], vp_ref[:, ksl], vs_ref[:, ksl], vn_ref[:, ksl]],
                               axis=0).astype(BF16)
        for g in range(ATT_GRP):
            h = hk * ATT_GRP + g
            hsl = slice(h * ATT_HEAD_DIM, (h + 1) * ATT_HEAD_DIM)
            qg = q_ref[:, hsl].astype(BF16)
            s = _bdot(qg, kcat, NT) * scale
            s = jnp.where(valid, s - slope_ref[h] * alibi, NEG)
            sink = sink_ref[h]
            m = jnp.maximum(jnp.max(s, axis=-1, keepdims=True), sink)
            p = jnp.exp(s - m)
            denom = jnp.sum(p, axis=-1, keepdims=True) + jnp.exp(sink - m)
            o = _bdot(p.astype(BF16), vcat) / denom
            o_ref[:, hsl] = jnp.where(q_real, o, 0.0).astype(o_ref.dtype)


def window_attention(qkv, slopes, sink):
    lp = qkv.shape[0]
    nb = lp // CHUNK
    qw = ATT_Q_HEADS * ATT_HEAD_DIM
    kvw = ATT_KV_HEADS * ATT_HEAD_DIM
    kcol = qw // kvw
    vcol = kcol + 1

    def kv_spec(col, which):
        def imap(b, *_):
            if which == 0:
                blk = 0
            else:
                blk = jnp.clip(b - 2 + which, 0, nb - 1)
            return (blk, col)
        return pl.BlockSpec((CHUNK, kvw), imap)

    return pl.pallas_call(
        _attn_kernel,
        grid_spec=pltpu.PrefetchScalarGridSpec(
            num_scalar_prefetch=2,
            grid=(nb,),
            in_specs=[pl.BlockSpec((CHUNK, qw), lambda b, *_: (b, 0))]
            + [kv_spec(kcol, w) for w in range(4)] + [kv_spec(vcol, w) for w in range(4)],
            out_specs=pl.BlockSpec((CHUNK, qw), lambda b, *_: (b, 0)),
        ),
        out_shape=jax.ShapeDtypeStruct((lp, qw), BF16),
        compiler_params=_cparams(("arbitrary",)),
        name="window_attention",
    )(slopes, sink, *([qkv] * 9))


def _mlstm_direction(d, ce, q_ref, k_ref, v_ref, g_ref, gb_ref, tri_ref, h_ref, c_ref, n_ref, m_ref):
    tri = tri_ref[d]
    trib = tri > 0.5
    row = ce * CHUNK + lax.broadcasted_iota(I32, (CHUNK, CHUNK), 0)
    pre = g_ref[...] + gb_ref[d]
    real = row >= PAD
    real_col = real[:, 0:1]
    li_all = jnp.where(real, pre, 0.0)
    lf_all = jnp.where(real, -_softplus(-pre), 0.0)
    bl_all = jnp.dot(tri, lf_all, precision=HIGHEST, preferred_element_type=F32)
    li_t = li_all.T
    bl_t = bl_all.T
    g_all = jnp.sum(lf_all, axis=0, keepdims=True)

    for hd in range(ML_HEADS):
        fl = hd + ML_F_LANE
        li_col = li_all[:, hd:hd + 1]
        li_row = li_t[hd:hd + 1, :]
        bl_col = bl_all[:, fl:fl + 1]
        bl_row = bl_t[fl:fl + 1, :]
        g_tot = g_all[:, fl:fl + 1]
        st = d * ML_HEADS + hd
        ksl = slice(hd * ML_QK_DIM, (hd + 1) * ML_QK_DIM)
        vsl = slice(hd * ML_V_DIM, (hd + 1) * ML_V_DIM)

        qf = jnp.where(real_col, q_ref[:, ksl], 0.0)
        q = qf.astype(BF16)
        kf = jnp.where(real_col, k_ref[:, ksl], 0.0) * (ML_QK_DIM ** -0.5)
        kt = kf.T.astype(BF16)
        v = jnp.where(real_col, v_ref[:, vsl], 0.0)
        vb = v.astype(BF16)

        m_prev = m_ref[st, 0:1, 0:1]
        n_prev = n_ref[st]
        c_prev = c_ref[st]

        dmat = jnp.where(trib, bl_col - bl_row + li_row, NEG)
        m_inter = bl_col + m_prev
        m_t = jnp.maximum(jnp.max(dmat, axis=1, keepdims=True), m_inter)
        qk = jnp.dot(q, kt, preferred_element_type=F32)
        p = jnp.exp(dmat - m_t) * qk
        w_inter = jnp.exp(m_inter - m_t)
        num = (jnp.dot(p.astype(BF16), vb, preferred_element_type=F32)
               + w_inter * jnp.dot(q, c_prev.astype(BF16), preferred_element_type=F32))
        qn = jnp.sum(qf * n_prev, axis=1, keepdims=True)
        den = jnp.sum(p, axis=1, keepdims=True) + w_inter * qn
        h_ref[:, vsl] = num / jnp.maximum(jnp.abs(den), jnp.exp(-m_t))

        a_col = g_tot - bl_col + li_col
        m_loc = jnp.max(a_col, axis=0, keepdims=True)
        wa = jnp.exp(a_col - m_loc)
        c_loc = jnp.dot(kt, (wa * v).astype(BF16), preferred_element_type=F32)
        n_loc = jnp.sum(wa * kf, axis=0, keepdims=True)
        m_new = jnp.maximum(g_tot + m_prev, m_loc)
        s_old = jnp.exp(g_tot + m_prev - m_new)
        s_new = jnp.exp(m_loc - m_new)
        c_ref[st] = s_old * c_prev + s_new * c_loc
        n_ref[st] = s_old * n_prev + s_new * n_loc
        m_ref[st] = jnp.broadcast_to(m_new, m_ref.shape[1:])


def _mlstm_scan_kernel(qf_ref, kf_ref, vf_ref, gf_ref, qb_ref, kb_ref, vb_ref, gbk_ref, gb_ref, tri_ref,
                       hf_ref, hb_ref, c_ref, n_ref, m_ref):
    c = pl.program_id(0)
    nch = pl.num_programs(0)

    @pl.when(c == 0)
    def _():
        c_ref[...] = jnp.zeros(c_ref.shape, F32)
        n_ref[...] = jnp.zeros(n_ref.shape, F32)
        m_ref[...] = jnp.zeros(m_ref.shape, F32)

    _mlstm_direction(0, c, qf_ref, kf_ref, vf_ref, gf_ref, gb_ref, tri_ref, hf_ref, c_ref, n_ref, m_ref)
    _mlstm_direction(1, nch - 1 - c, qb_ref, kb_ref, vb_ref, gbk_ref, gb_ref, tri_ref, hb_ref, c_ref, n_ref, m_ref)


def mlstm_scan(qkvo, gates, gate_bias_rows, tri2):
    lp = qkvo.shape[0]
    nch = lp // CHUNK
    qk_w = ML_HEADS * ML_QK_DIM

    def specs(chunk, d):
        return [
            pl.BlockSpec((CHUNK, qk_w), lambda c: (chunk(c), 0)),
            pl.BlockSpec((CHUNK, qk_w), lambda c: (chunk(c), 1)),
            pl.BlockSpec((CHUNK, ML_WIDTH), lambda c: (chunk(c), 2 * qk_w // ML_WIDTH)),
            pl.BlockSpec((CHUNK, CHUNK), lambda c: (chunk(c), d)),
        ]

    fwd = lambda c: c
    bwd = lambda c: nch - 1 - c
    n_state = 2 * ML_HEADS
    return pl.pallas_call(
        _mlstm_scan_kernel,
        grid=(nch,),
        in_specs=specs(fwd, 0) + specs(bwd, 1) + [
            pl.BlockSpec((2, 1, CHUNK), lambda c: (0, 0, 0)),
            pl.BlockSpec((2, CHUNK, CHUNK), lambda c: (0, 0, 0)),
        ],
        out_specs=[pl.BlockSpec((CHUNK, ML_WIDTH), lambda c: (c, 0)),
                   pl.BlockSpec((CHUNK, ML_WIDTH), lambda c: (nch - 1 - c, 0))],
        out_shape=[jax.ShapeDtypeStruct((lp, ML_WIDTH), F32), jax.ShapeDtypeStruct((lp, ML_WIDTH), F32)],
        scratch_shapes=[pltpu.VMEM((n_state, ML_QK_DIM, ML_V_DIM), F32), pltpu.VMEM((n_state, 1, ML_QK_DIM), F32),
                        pltpu.VMEM((n_state, 8, CHUNK), F32)],
        compiler_params=_cparams(("arbitrary",)),
        name="mlstm_scan",
    )(qkvo, qkvo, qkvo, gates, qkvo, qkvo, qkvo, gates, gate_bias_rows, tri2)


def _mlstm_final_kernel(hf_ref, hb_ref, o_ref_in, nw_ref, y_ref):
    h = hf_ref[...] + hb_ref[...]
    o = o_ref_in[...]
    for hd in range(ML_HEADS):
        sl = slice(hd * ML_V_DIM, (hd + 1) * ML_V_DIM)
        hh = h[:, sl]
        hh = hh * lax.rsqrt(jnp.mean(hh * hh, axis=-1, keepdims=True) + NORM_EPS) * nw_ref[:, sl]
        y_ref[:, sl] = (_sigmoid(o[:, sl]) * hh).astype(y_ref.dtype)


def mlstm_final(hf, hb, qkvo, head_norm):
    lp = qkvo.shape[0]
    ocol = (2 * ML_HEADS * ML_QK_DIM + ML_WIDTH) // ML_WIDTH
    return pl.pallas_call(
        _mlstm_final_kernel,
        grid=(lp // CHUNK,),
        in_specs=[
            pl.BlockSpec((CHUNK, ML_WIDTH), lambda c: (c, 0)),
            pl.BlockSpec((CHUNK, ML_WIDTH), lambda c: (c, 0)),
            pl.BlockSpec((CHUNK, ML_WIDTH), lambda c: (c, ocol)),
            pl.BlockSpec((1, ML_WIDTH), lambda c: (0, 0)),
        ],
        out_specs=pl.BlockSpec((CHUNK, ML_WIDTH), lambda c: (c, 0)),
        out_shape=jax.ShapeDtypeStruct((lp, ML_WIDTH), BF16),
        compiler_params=_cparams(("arbitrary",)),
        name="mlstm_final",
    )(hf, hb, qkvo, head_norm.reshape(1, ML_WIDTH))


def _dft_factors(n):
    best = None
    for n1 in range(8, n + 1, 8):
        if n % n1 == 0:
            n2 = n // n1
            if best is None or n1 + n2 < best[0] + best[1]:
                best = (n1, n2)
    assert best is not None
    return best


def _split_bf16(a):
    hi = a.astype(BF16)
    return hi, (a - hi.astype(F32)).astype(BF16)


def _dot3(a, b):
    return _bdot(a[0], b[0]) + _bdot(a[1], b[0]) + _bdot(a[0], b[1])


def _fnet_stage1_kernel(x_ref, c2h_ref, c2l_ref, s2h_ref, s2l_ref, twc_ref, tws_ref, tr_ref, ti_ref, *, n2):
    x = x_ref[...]
    rows = lax.broadcasted_iota(I32, x.shape, 0)
    xs = _split_bf16(jnp.where(rows < n2, x, 0.0))
    cx = _dot3((c2h_ref[...], c2l_ref[...]), xs)
    sx = _dot3((s2h_ref[...], s2l_ref[...]), xs)
    twc = twc_ref[0]
    tws = tws_ref[0]
    tr_ref[...] = cx * twc - sx * tws
    ti_ref[...] = -(cx * tws + sx * twc)


def _fnet_stage2_kernel(tr_ref, ti_ref, c1h_ref, c1l_ref, s1h_ref, s1l_ref, cch_ref, ccl_ref, sch_ref, scl_ref, o_ref,
                        *, scale):
    tr = _split_bf16(tr_ref[0])
    ti = _split_bf16(ti_ref[0])
    c1 = (c1h_ref[...], c1l_ref[...])
    s1 = (s1h_ref[...], s1l_ref[...])
    yr = _dot3(c1, tr) + _dot3(s1, ti)
    yi = _dot3(c1, ti) - _dot3(s1, tr)
    cc = (cch_ref[...], ccl_ref[...])
    sc = (sch_ref[...], scl_ref[...])
    for g in range(FN_GROUPS):
        sl = slice(g * FN_GROUP_DIM, (g + 1) * FN_GROUP_DIM)
        out = _dot3(_split_bf16(yr[:, sl]), cc) + _dot3(_split_bf16(yi[:, sl]), sc)
        o_ref[:, sl] = out * scale


def fourier_mixer(u_fn):
    n, ch = u_fn.shape
    n1, n2 = _dft_factors(n)
    n2p = -(-n2 // 8) * 8
    two_pi = 2.0 * np.pi

    def cs(size, pad):
        idx = np.arange(size)
        ang = two_pi * ((idx[:, None] * idx[None, :]) % size) / size
        out = []
        for fn in (np.cos, np.sin):
            m = np.zeros((pad, pad), np.float32)
            m[:size, :size] = fn(ang)
            hi = jnp.asarray(m).astype(BF16)
            out += [hi, (jnp.asarray(m) - hi.astype(F32)).astype(BF16)]
        return out

    dft2 = cs(n2, n2p)
    dft1 = cs(n1, n1)
    dftc = cs(FN_GROUP_DIM, FN_GROUP_DIM)
    tw_ang = two_pi * ((np.arange(n1)[:, None] * np.arange(n2)[None, :]) % n) / n
    twc = np.zeros((n1, n2p, 1), np.float32)
    tws = np.zeros((n1, n2p, 1), np.float32)
    twc[:, :n2, 0] = np.cos(tw_ang)
    tws[:, :n2, 0] = np.sin(tw_ang)

    def full(shape):
        return pl.BlockSpec(shape, lambda i: (0,) * len(shape))

    x2 = u_fn.reshape(n2, n1 * ch)
    tr, ti = pl.pallas_call(
        functools.partial(_fnet_stage1_kernel, n2=n2),
        grid=(n1,),
        in_specs=[pl.BlockSpec((n2p, ch), lambda i: (0, i))] + [full((n2p, n2p))] * 4 + [
            pl.BlockSpec((1, n2p, 1), lambda i: (i, 0, 0)),
            pl.BlockSpec((1, n2p, 1), lambda i: (i, 0, 0)),
        ],
        out_specs=[pl.BlockSpec((n2p, ch), lambda i: (0, i)), pl.BlockSpec((n2p, ch), lambda i: (0, i))],
        out_shape=[jax.ShapeDtypeStruct((n2p, n1 * ch), F32), jax.ShapeDtypeStruct((n2p, n1 * ch), F32)],
        compiler_params=_cparams(("arbitrary",)),
        name="fnet_stage1",
    )(x2, *dft2, jnp.asarray(twc), jnp.asarray(tws))

    tr3 = tr.reshape(n2p, n1, ch)
    ti3 = ti.reshape(n2p, n1, ch)
    scale = 1.0 / math.sqrt(float(n) * FN_GROUP_DIM)
    out = pl.pallas_call(
        functools.partial(_fnet_stage2_kernel, scale=scale),
        grid=(n2,),
        in_specs=[
            pl.BlockSpec((1, n1, ch), lambda i: (i, 0, 0)),
            pl.BlockSpec((1, n1, ch), lambda i: (i, 0, 0)),
        ] + [full((n1, n1))] * 4 + [full((FN_GROUP_DIM, FN_GROUP_DIM))] * 4,
        out_specs=pl.BlockSpec((n1, ch), lambda i: (0, i)),
        out_shape=jax.ShapeDtypeStruct((n1, n2 * ch), F32),
        compiler_params=_cparams(("arbitrary",)),
        name="fnet_stage2",
    )(tr3, ti3, *dft1, *dftc)
    return out.reshape(n, ch)


def _cap_sizes(lp):
    n_tok = lp - PAD
    cap = EC_FACTOR * n_tok // N_EXPERTS
    capp = -(-cap // 16) * 16
    return cap, capp


def _topk_kernel(lg_ref, su_ref, pos_ref, aff_ref, start_ref, idx_ref, rank_scr, pos_scr, *, cap, capp, jb):
    ne, lp = lg_ref.shape
    nch = lp // CHUNK
    lg = lg_ref[...]
    mx = jnp.max(lg, axis=0, keepdims=True)
    ex = jnp.exp(lg - mx)
    aff = ex / jnp.sum(ex, axis=0, keepdims=True)
    lane = lax.broadcasted_iota(I32, (ne, lp), 1)
    real = lane >= PAD
    bits = jnp.where(real, pltpu.bitcast(aff, I32), -1)
    capf = float(cap)

    def bisect(i, thr):
        cand = thr | jnp.left_shift(jnp.int32(1), 30 - i)
        cnt = jnp.sum(jnp.where(bits >= cand, 1.0, 0.0), axis=1, keepdims=True)
        return jnp.where(cnt >= capf, cand, thr)

    thr = lax.fori_loop(0, 31, bisect, jnp.zeros((ne, 1), I32))
    gt = bits > thr
    eq = bits == thr
    need = capf - jnp.sum(jnp.where(gt, 1.0, 0.0), axis=1, keepdims=True)

    su = su_ref[...]

    def excl_cumsum(mask_f32, out_scr, want_starts):
        carry = jnp.zeros((ne, 1), F32)
        starts = jnp.zeros((ne, CHUNK), F32)
        lane_c = lax.broadcasted_iota(I32, (ne, CHUNK), 1)
        for c in range(nch):
            tile = mask_f32[:, c * CHUNK:(c + 1) * CHUNK]
            within = jnp.dot(tile.astype(BF16), su, preferred_element_type=F32)
            out_scr[:, c * CHUNK:(c + 1) * CHUNK] = within + carry
            if want_starts:
                starts = jnp.where(lane_c == c, carry, starts)
            carry = carry + jnp.sum(tile, axis=1, keepdims=True)
        if want_starts:
            starts = jnp.where(lane_c == nch, carry, starts)
        return starts

    excl_cumsum(jnp.where(eq, 1.0, 0.0), rank_scr, False)
    sel = gt | (eq & (rank_scr[...] < need))
    starts = excl_cumsum(jnp.where(sel, 1.0, 0.0), pos_scr, True)
    posf = jnp.where(sel, pos_scr[...], -1.0)
    pos_scr[...] = posf
    pos_ref[...] = posf.astype(I32)
    aff_ref[...] = jnp.where(sel, aff, 0.0)
    start_ref[...] = starts.astype(I32)

    tok = lax.broadcasted_iota(I32, (jb, lp), 1).astype(F32)
    lane_e = lax.broadcasted_iota(I32, (jb, ne), 1)
    for blk in range(capp // jb):
        jcol = (blk * jb + lax.broadcasted_iota(I32, (jb, 1), 0)).astype(F32)

        def per_expert(e, acc):
            prow = pos_scr[pl.ds(e, 1), :]
            hit = jnp.sum(jnp.where(prow == jcol, tok, 0.0), axis=1, keepdims=True)
            return jnp.where(lane_e == e, hit, acc)

        acc = lax.fori_loop(0, ne, per_expert, jnp.zeros((jb, ne), F32))
        idx_ref[blk * jb:(blk + 1) * jb, :] = acc.astype(I32)


def topk_route(logits_t):
    ne, lp = logits_t.shape
    cap, capp = _cap_sizes(lp)
    jb = 80 if capp % 80 == 0 else 16
    su = jnp.asarray(np.triu(np.ones((CHUNK, CHUNK), np.float32), 1), BF16)
    return pl.pallas_call(
        functools.partial(_topk_kernel, cap=cap, capp=capp, jb=jb),
        out_shape=[jax.ShapeDtypeStruct((ne, lp), I32), jax.ShapeDtypeStruct((ne, lp), F32),
                   jax.ShapeDtypeStruct((ne, CHUNK), I32), jax.ShapeDtypeStruct((capp, ne), I32)],
        scratch_shapes=[pltpu.VMEM((ne, lp), F32), pltpu.VMEM((ne, lp), F32)],
        compiler_params=pltpu.CompilerParams(vmem_limit_bytes=VMEM_LIMIT),
        name="topk_route",
    )(logits_t, su)


GATHER_UNROLL = 8


def _gather_kernel(idx_ref, u_hbm, o_ref, stage_ref, sem):
    e = pl.program_id(0)
    ne = pl.num_programs(0)
    capp = stage_ref.shape[1]

    def issue(eq, slot):
        def body(jo, carry):
            for ji in range(GATHER_UNROLL):
                j = jo * GATHER_UNROLL + ji
                t = idx_ref[eq, j]
                pltpu.make_async_copy(u_hbm.at[pl.ds(t, 1), :], stage_ref.at[slot, pl.ds(j, 1), :],
                                      sem.at[slot]).start()
            return carry

        lax.fori_loop(0, capp // GATHER_UNROLL, body, 0)

    @pl.when(e == 0)
    def _():
        issue(e, 0)

    @pl.when(e + 1 < ne)
    def _():
        issue(e + 1, (e + 1) % 2)

    slot = e % 2
    pltpu.make_async_copy(u_hbm.at[pl.ds(0, capp), :], stage_ref.at[slot], sem.at[slot]).wait()
    o_ref[0] = stage_ref[slot].astype(o_ref.dtype)


def gather_tokens(idx, u):
    ne, capp = idx.shape
    lp, d = u.shape
    assert capp % GATHER_UNROLL == 0
    return pl.pallas_call(
        _gather_kernel,
        grid_spec=pltpu.PrefetchScalarGridSpec(
            num_scalar_prefetch=1,
            grid=(ne,),
            in_specs=[pl.BlockSpec(memory_space=pl.ANY)],
            out_specs=pl.BlockSpec((1, capp, d), lambda e, *_: (e, 0, 0)),
            scratch_shapes=[pltpu.VMEM((2, capp, d), F32), pltpu.SemaphoreType.DMA((2,))],
        ),
        out_shape=jax.ShapeDtypeStruct((ne, capp, d), BF16),
        compiler_params=_cparams(("arbitrary",)),
        name="moe_gather",
    )(idx, u)


def _ffn_kernel(xs_ref, wg_ref, wu_ref, wd_ref, o_ref, hdn_ref, *, n_ff):
    s = pl.program_id(1)

    @pl.when(s < n_ff)
    def _():
        x = xs_ref[0]
        g = jnp.dot(x, wg_ref[0, 0].astype(BF16), preferred_element_type=F32)
        u = jnp.dot(x, wu_ref[0, 0].astype(BF16), preferred_element_type=F32)
        hdn_ref[s] = (g * _sigmoid(g) * u).astype(BF16)

    @pl.when(s >= n_ff)
    def _():
        wd = wd_ref[0, 0].astype(BF16)
        acc = None
        for j in range(n_ff):
            part = jnp.dot(hdn_ref[j], wd[j * FF_TILE:(j + 1) * FF_TILE, :], preferred_element_type=F32)
            acc = part if acc is None else acc + part
        o_ref[0] = acc


def expert_ffn(xs, w_gate, w_up, w_down, layer):
    ne, capp, d = xs.shape
    n_ff = EXPERT_FF // FF_TILE
    n_out = d // OUT_TILE
    return pl.pallas_call(
        functools.partial(_ffn_kernel, n_ff=n_ff),
        grid=(ne, n_ff + n_out),
        in_specs=[
            pl.BlockSpec((1, capp, d), lambda e, s: (e, 0, 0)),
            pl.BlockSpec((1, 1, d, FF_TILE), lambda e, s: (layer, e, 0, jnp.minimum(s, n_ff - 1))),
            pl.BlockSpec((1, 1, d, FF_TILE), lambda e, s: (layer, e, 0, jnp.minimum(s, n_ff - 1))),
            pl.BlockSpec((1, 1, EXPERT_FF, OUT_TILE), lambda e, s: (layer, e, 0, jnp.maximum(s - n_ff, 0))),
        ],
        out_specs=pl.BlockSpec((1, capp, OUT_TILE), lambda e, s: (e, 0, jnp.maximum(s - n_ff, 0))),
        out_shape=jax.ShapeDtypeStruct((ne, capp, d), F32),
        scratch_shapes=[pltpu.VMEM((n_ff, capp, FF_TILE), BF16)],
        compiler_params=_cparams(("arbitrary", "arbitrary")),
        name="expert_ffn",
    )(xs, w_gate, w_up, w_down)


def _combine_kernel(start_ref, h_ref, pos_ref, aff_ref, nw_ref, out_hbm, *refs, capp, rows, n_half, emit_h):
    if emit_h:
        o_ref, y_ref, acc_ref, g_ref, sem = refs
    else:
        y_ref, acc_ref, g_ref, sem = refs
    b = pl.program_id(0)
    hh = pl.program_id(1)
    nb = pl.num_programs(0)
    ne = pos_ref.shape[0]
    eh = ne // n_half

    def row_copy(src, dst, slot, n_rows=1):
        return pltpu.make_async_copy(out_hbm.at[pl.ds(src, n_rows), :], g_ref.at[slot, pl.ds(dst, n_rows), :],
                                     sem.at[slot])

    def block_rows(bq, hq):
        s0 = [start_ref[hq * eh + i, bq] for i in range(eh)]
        n = [start_ref[hq * eh + i, bq + 1] - s0[i] for i in range(eh)]
        return s0, n

    def issue(bq, hq, slot):
        s0, n = block_rows(bq, hq)
        off = jnp.int32(0)
        for i in range(eh):
            base = (hq * eh + i) * capp + s0[i]

            def body(j, carry, base=base, off=off):
                row_copy(base + j, off + j, slot).start()
                return carry

            lax.fori_loop(0, n[i], body, 0)
            off = off + n[i]

    @pl.when((b == 0) & (hh == 0))
    def _():
        g_ref[...] = jnp.zeros(g_ref.shape, F32)
        issue(b, hh, 0)

    @pl.when(hh + 1 < n_half)
    def _():
        issue(b, hh + 1, 1 - hh % 2)

    if n_half % 2 == 0:
        @pl.when((hh + 1 == n_half) & (b + 1 < nb))
        def _():
            issue(b + 1, 0, 0)

    slot = hh % 2
    s0, n = block_rows(b, hh)
    offs = []
    off = jnp.int32(0)
    for i in range(eh):
        offs.append(off)
        off = off + n[i]
    total = off

    stage_rows = g_ref.shape[1]
    bit = 1
    while bit <= stage_rows:
        @pl.when((total & bit) != 0)
        def _(bit=bit):
            row_copy(0, 0, slot, n_rows=bit).wait()
        bit *= 2

    @pl.when(hh == 0)
    def _():
        acc_ref[...] = h_ref[...]

    r_iota = lax.broadcasted_iota(I32, (rows, CHUNK), 0)
    d = acc_ref.shape[1]

    def chunk(ci, carry):
        r0 = pl.multiple_of(ci * rows, rows)
        rr = r_iota + r0
        onehot = jnp.zeros((rows, CHUNK), F32)
        gate = jnp.zeros((rows, CHUNK), F32)
        for i in range(eh):
            prow = pos_ref[pl.ds(hh * eh + i, 1), :]
            arow = aff_ref[pl.ds(hh * eh + i, 1), :]
            tgt = jnp.where(prow >= 0, prow - s0[i] + offs[i], -1)
            hit = rr == tgt
            onehot = jnp.where(hit, 1.0, onehot)
            gate = jnp.where(hit, arow, gate)
        gcol = jnp.sum(gate, axis=1, keepdims=True)
        oh = onehot.T.astype(BF16)
        for c0 in range(0, d, COMBINE_SLAB):
            gs = (g_ref[slot, pl.ds(r0, rows), c0:c0 + COMBINE_SLAB] * gcol).astype(BF16)
            acc_ref[:, c0:c0 + COMBINE_SLAB] += jnp.dot(oh, gs, preferred_element_type=F32)
        return carry

    lax.fori_loop(0, (total + rows - 1) // rows, chunk, 0)

    @pl.when(hh == n_half - 1)
    def _():
        x = acc_ref[...]
        if emit_h:
            o_ref[...] = x
        y = x * lax.rsqrt(jnp.mean(x * x, axis=-1, keepdims=True) + NORM_EPS) * nw_ref[...]
        y_ref[...] = y.astype(y_ref.dtype)


def moe_combine(h, pos, aff, starts, ffn_out, next_norm_w, last):
    lp, d = h.shape
    ne, capp, _ = ffn_out.shape
    rows = 2 * CHUNK
    n_half = 2
    stage_rows = (ne // n_half) * CHUNK
    blk = pl.BlockSpec((CHUNK, d), lambda b, hh, *_: (b, 0))
    if last:
        out_specs = pl.BlockSpec((CHUNK, d), lambda b, hh, *_: (jnp.maximum(b - 1, 0), 0))
        out_shape = jax.ShapeDtypeStruct((lp - CHUNK, d), F32)
    else:
        out_specs = [blk, blk]
        out_shape = [jax.ShapeDtypeStruct((lp, d), F32), jax.ShapeDtypeStruct((lp, d), BF16)]
    return pl.pallas_call(
        functools.partial(_combine_kernel, capp=capp, rows=rows, n_half=n_half, emit_h=not last),
        grid_spec=pltpu.PrefetchScalarGridSpec(
            num_scalar_prefetch=1,
            grid=(lp // CHUNK, n_half),
            in_specs=[
                blk,
                pl.BlockSpec((ne, CHUNK), lambda b, hh, *_: (0, b)),
                pl.BlockSpec((ne, CHUNK), lambda b, hh, *_: (0, b)),
                pl.BlockSpec((1, d), lambda b, hh, *_: (0, 0)),
                pl.BlockSpec(memory_space=pl.ANY),
            ],
            out_specs=out_specs,
            scratch_shapes=[pltpu.VMEM((CHUNK, d), F32), pltpu.VMEM((2, stage_rows, d), F32),
                            pltpu.SemaphoreType.DMA((2,))],
        ),
        out_shape=out_shape,
        compiler_params=_cparams(("arbitrary", "arbitrary")),
        name="moe_combine",
    )(starts, h, pos, aff, next_norm_w.reshape(1, d), ffn_out.reshape(ne * capp, d))


def moe_layer(h, norm_w, w_router, w_gate, w_up, w_down, layer, next_norm_w, last):
    u, logits_t = rmsnorm_router(h, norm_w, w_router)
    pos, aff, starts, idx_t = topk_route(logits_t)
    xs = gather_tokens(idx_t.T, u)
    ffn_out = expert_ffn(xs, w_gate, w_up, w_down, layer)
    return moe_combine(h, pos, aff, starts, ffn_out, next_norm_w, last)


def _scan_masks():
    lower = np.tril(np.ones((CHUNK, CHUNK), np.float32))
    return jnp.asarray(np.stack([lower, lower.T]))


def _pad_cols(w, width):
    return jnp.pad(w, ((0, 0), (0, width - w.shape[1])))


def ab_mixer(h, u, w_in, conv_w, conv_b, a_log, dt_bias, d_skip, ssd_norm, sink, w_out):
    zx_w = SSD_INNER + SSD_CONV_CH
    dt0 = zx_w
    qkv0 = zx_w + 2 * SSD_HEADS
    z = matmul([u], w_in, 512, col0=0, n=SSD_INNER)
    xbc = matmul([u], w_in, 512, col0=SSD_INNER, n=SSD_CONV_CH)
    w_dt = jnp.concatenate([_pad_cols(w_in[:, dt0:dt0 + SSD_HEADS], CHUNK),
                            _pad_cols(w_in[:, dt0 + SSD_HEADS:qkv0], CHUNK)], axis=1)
    dt_raw = matmul([u], w_dt, 2 * CHUNK)
    qkv = matmul([u], w_in[:, qkv0:], 512)

    xbc_act = ssd_conv(xbc, conv_w, conv_b)
    dt_bias_rows = _pad_cols(dt_bias, CHUNK).reshape(2, 1, CHUNK)
    acoef_rows = _pad_cols(-jnp.exp(a_log), CHUNK).reshape(2, 1, CHUNK)
    yf, yb = ssd_scan(xbc_act, dt_raw, dt_bias_rows, acoef_rows, _scan_masks())
    d_skip_row = jnp.repeat(d_skip, SSD_HEAD_DIM).reshape(1, SSD_INNER)
    y_ssd = ssd_final(yf, yb, xbc_act, z, d_skip_row, ssd_norm)

    slopes = 2.0 ** (-8.0 * jnp.arange(1, ATT_Q_HEADS + 1, dtype=F32) / ATT_Q_HEADS)
    y_att = window_attention(qkv, slopes, sink)
    return matmul([y_ssd, y_att], w_out, 512, res=h)


def cd_mixer(h, u, w_in, i_bias, f_bias, head_norm, w_out):
    qkvo_w = 2 * ML_HEADS * ML_QK_DIM + 2 * ML_WIDTH
    i0 = qkvo_w
    f0 = i0 + 2 * ML_HEADS
    fn0 = f0 + 2 * ML_HEADS
    qkvo = matmul([u], w_in, 512, col0=0, n=qkvo_w)

    def gate_block(dirn):
        wi = w_in[:, i0 + dirn * ML_HEADS:i0 + (dirn + 1) * ML_HEADS]
        wf = w_in[:, f0 + dirn * ML_HEADS:f0 + (dirn + 1) * ML_HEADS]
        return _pad_cols(jnp.concatenate([_pad_cols(wi, ML_F_LANE), wf], axis=1), CHUNK)

    def bias_block(dirn):
        row = jnp.concatenate([jnp.pad(i_bias[dirn], (0, ML_F_LANE - ML_HEADS)), f_bias[dirn]])
        return jnp.pad(row, (0, CHUNK - row.shape[0]))

    gates = matmul([u], jnp.concatenate([gate_block(0), gate_block(1)], axis=1), 2 * CHUNK)
    gate_bias_rows = jnp.stack([bias_block(0), bias_block(1)]).reshape(2, 1, CHUNK)
    hf, hb = mlstm_scan(qkvo, gates, gate_bias_rows, _scan_masks())
    y_ml = mlstm_final(hf, hb, qkvo, head_norm)

    u_fn = matmul([u], w_in[:, fn0:], 512)
    y_fn = fourier_mixer(u_fn[PAD:])
    y_fn = jnp.pad(y_fn, ((PAD, 0), (0, 0))).astype(BF16)
    return matmul([y_ml, y_fn], w_out, 512, res=h)


def kernel(x, meta_tokens, norm_mix, ab_w_in, ab_conv_w, ab_conv_b, ab_a_log, ab_dt_bias, ab_d_skip, ab_ssd_norm,
           ab_sink, ab_w_out, cd_w_in, cd_i_bias, cd_f_bias, cd_head_norm, cd_w_out, norm_ffn, moe_router,
           moe_w_gate, moe_w_up, moe_w_down, final_norm):
    bsz, seq, d = x.shape
    assert bsz == 1 and d == D_MODEL and seq % CHUNK == 0
    depth = norm_mix.shape[0]
    h = jnp.concatenate([jnp.zeros((PAD, d), F32), meta_tokens.astype(F32), x[0]], axis=0)
    u = rmsnorm(h, norm_mix[0], BF16)
    for layer in range(depth):
        j = layer // 2
        if layer % 2 == 0:
            h = ab_mixer(h, u, ab_w_in[j], ab_conv_w[j], ab_conv_b[j], ab_a_log[j], ab_dt_bias[j],
                         ab_d_skip[j], ab_ssd_norm[j], ab_sink[j], ab_w_out[j])
        else:
            h = cd_mixer(h, u, cd_w_in[j], cd_i_bias[j], cd_f_bias[j], cd_head_norm[j], cd_w_out[j])
        last = layer == depth - 1
        next_norm_w = final_norm if last else norm_mix[layer + 1]
        res = moe_layer(h, norm_ffn[layer], moe_router[layer], moe_w_gate, moe_w_up, moe_w_down, layer,
                        next_norm_w, last)
        if last:
            return res[None]
        h, u = res
```

```python
import functools
import math

import numpy as np
import jax
import jax.numpy as jnp
from jax import lax
from jax.experimental import pallas as pl
from jax.experimental.pallas import tpu as pltpu

F32 = jnp.float32
BF16 = jnp.bfloat16
I32 = jnp.int32
HIGHEST = lax.Precision.HIGHEST

D_MODEL = 4096
N_META = 16
CHUNK = 128
PAD = CHUNK - N_META
NORM_EPS = 1e-6
NEG = -1e30

SSD_HEADS = 32
SSD_HEAD_DIM = 64
SSD_INNER = SSD_HEADS * SSD_HEAD_DIM
SSD_GROUPS = 4
SSD_STATE = 128
SSD_CONV = 5
SSD_CONV_CH = SSD_INNER + 2 * SSD_GROUPS * SSD_STATE

ATT_Q_HEADS = 16
ATT_KV_HEADS = 4
ATT_HEAD_DIM = 128
ATT_WINDOW = 128
ATT_GRP = ATT_Q_HEADS // ATT_KV_HEADS

ML_HEADS = 6
ML_QK_DIM = 256
ML_V_DIM = 512
ML_WIDTH = ML_HEADS * ML_V_DIM
ML_F_LANE = 8

FN_GROUPS = 4
FN_GROUP_DIM = 256
FN_WIDTH = FN_GROUPS * FN_GROUP_DIM

N_EXPERTS = 16
EC_FACTOR = 2
EXPERT_FF = 1536
FF_TILE = 256
OUT_TILE = 512
COMBINE_SLAB = 512

VMEM_LIMIT = 56 * 1024 * 1024


def _cparams(sem):
    return pltpu.CompilerParams(dimension_semantics=sem, vmem_limit_bytes=VMEM_LIMIT)


def _row_tile(lp):
    return 640 if lp % 640 == 0 else CHUNK


def _sigmoid(x):
    return 1.0 / (1.0 + jnp.exp(-x))


def _softplus(x):
    return jnp.maximum(x, 0.0) + jnp.log1p(jnp.exp(-jnp.abs(x)))


NT = (((1,), (1,)), ((), ()))


def _bdot(a, b, dims=None):
    if dims is None:
        return jnp.dot(a, b, preferred_element_type=F32)
    return lax.dot_general(a, b, dims, preferred_element_type=F32)


def _rmsnorm_kernel(h_ref, w_ref, o_ref):
    x = h_ref[...]
    y = x * lax.rsqrt(jnp.mean(x * x, axis=-1, keepdims=True) + NORM_EPS) * w_ref[...]
    o_ref[...] = y.astype(o_ref.dtype)


def rmsnorm(h, w, out_dtype):
    lp, d = h.shape
    tm = _row_tile(lp)
    return pl.pallas_call(
        _rmsnorm_kernel,
        grid=(lp // tm,),
        in_specs=[pl.BlockSpec((tm, d), lambda i: (i, 0)), pl.BlockSpec((1, d), lambda i: (0, 0))],
        out_specs=pl.BlockSpec((tm, d), lambda i: (i, 0)),
        out_shape=jax.ShapeDtypeStruct((lp, d), out_dtype),
        compiler_params=_cparams(("arbitrary",)),
        name="rmsnorm",
    )(h, w.reshape(1, d))


def _rmsnorm_router_kernel(h_ref, w_ref, wrt_ref, u_ref, lg_ref):
    x = h_ref[...]
    y = x * lax.rsqrt(jnp.mean(x * x, axis=-1, keepdims=True) + NORM_EPS) * w_ref[...]
    u_ref[...] = y
    lg_ref[...] = lax.dot_general(wrt_ref[...], y, (((1,), (1,)), ((), ())), precision=HIGHEST,
                                  preferred_element_type=F32)


def rmsnorm_router(h, w, w_router):
    lp, d = h.shape
    tm = _row_tile(lp)
    return pl.pallas_call(
        _rmsnorm_router_kernel,
        grid=(lp // tm,),
        in_specs=[pl.BlockSpec((tm, d), lambda i: (i, 0)), pl.BlockSpec((1, d), lambda i: (0, 0)),
                  pl.BlockSpec((N_EXPERTS, d), lambda i: (0, 0))],
        out_specs=[pl.BlockSpec((tm, d), lambda i: (i, 0)), pl.BlockSpec((N_EXPERTS, tm), lambda i: (0, i))],
        out_shape=[jax.ShapeDtypeStruct((lp, d), F32), jax.ShapeDtypeStruct((N_EXPERTS, lp), F32)],
        compiler_params=_cparams(("arbitrary",)),
        name="rmsnorm_router",
    )(h, w.reshape(1, d), w_router.T)


def _matmul_kernel(*refs, k_splits, has_res, w_t):
    n_x = len(k_splits)
    x_refs = refs[:n_x]
    w_ref = refs[n_x]
    r_ref = refs[n_x + 1] if has_res else None
    o_ref = refs[n_x + 1 + int(has_res)]
    wb_ref = refs[n_x + 2 + int(has_res)]

    @pl.when(pl.program_id(1) == 0)
    def _():
        w = w_ref[...].T if w_t else w_ref[...]
        wb_ref[...] = w.astype(BF16)

    acc = None
    k0 = 0
    for x_ref, kk in zip(x_refs, k_splits):
        part = jnp.dot(x_ref[...], wb_ref[k0:k0 + kk, :], preferred_element_type=F32)
        acc = part if acc is None else acc + part
        k0 += kk
    if has_res:
        acc = acc + r_ref[...]
    o_ref[...] = acc.astype(o_ref.dtype)


def matmul(xs, w, tn, res=None, out_dtype=F32, col0=0, n=None, w_t=False):
    lp = xs[0].shape[0]
    k_splits = tuple(int(x.shape[1]) for x in xs)
    k, n_all = (w.shape[1], w.shape[0]) if w_t else w.shape
    n = n_all - col0 if n is None else n
    assert sum(k_splits) == k and n % tn == 0 and col0 % tn == 0
    cb0 = col0 // tn
    tm = _row_tile(lp)
    in_specs = [pl.BlockSpec((tm, kk), lambda j, i: (i, 0)) for kk in k_splits]
    if w_t:
        in_specs.append(pl.BlockSpec((tn, k), lambda j, i: (cb0 + j, 0)))
    else:
        in_specs.append(pl.BlockSpec((k, tn), lambda j, i: (0, cb0 + j)))
    args = list(xs) + [w]
    if res is not None:
        in_specs.append(pl.BlockSpec((tm, tn), lambda j, i: (i, j)))
        args.append(res)
    return pl.pallas_call(
        functools.partial(_matmul_kernel, k_splits=k_splits, has_res=res is not None, w_t=w_t),
        grid=(n // tn, lp // tm),
        in_specs=in_specs,
        out_specs=pl.BlockSpec((tm, tn), lambda j, i: (i, j)),
        out_shape=jax.ShapeDtypeStruct((lp, n), out_dtype),
        scratch_shapes=[pltpu.VMEM((k, tn), BF16)],
        compiler_params=_cparams(("arbitrary", "arbitrary")),
        name="matmul",
    )(*args)


def _ssd_conv_kernel(cur_ref, prev_ref, next_ref, w_ref, b_ref, o_ref, ext_ref):
    c = pl.program_id(0)
    nch = pl.num_programs(0)
    halo = 8
    ext_ref[0:halo, :] = jnp.where(c > 0, prev_ref[...], 0.0)
    ext_ref[halo:halo + CHUNK, :] = cur_ref[...]
    ext_ref[halo + CHUNK:2 * halo + CHUNK, :] = jnp.where(c < nch - 1, next_ref[...], 0.0)
    w = w_ref[...]
    acc = jnp.zeros(cur_ref.shape, F32) + b_ref[...]
    centre = (SSD_CONV - 1) // 2
    for j in range(SSD_CONV):
        acc = acc + w[j:j + 1, :] * ext_ref[pl.ds(halo - centre + j, CHUNK), :]
    y = acc * _sigmoid(acc)
    row = c * CHUNK + lax.broadcasted_iota(I32, y.shape, 0)
    o_ref[...] = jnp.where(row >= PAD, y, 0.0)


def ssd_conv(xbc, conv_w, conv_b):
    lp, ch = xbc.shape
    nch = lp // CHUNK
    sub = CHUNK // 8
    return pl.pallas_call(
        _ssd_conv_kernel,
        grid=(nch,),
        in_specs=[
            pl.BlockSpec((CHUNK, ch), lambda c: (c, 0)),
            pl.BlockSpec((8, ch), lambda c: (jnp.maximum(c * sub - 1, 0), 0)),
            pl.BlockSpec((8, ch), lambda c: (jnp.minimum((c + 1) * sub, nch * sub - 1), 0)),
            pl.BlockSpec((SSD_CONV, ch), lambda c: (0, 0)),
            pl.BlockSpec((1, ch), lambda c: (0, 0)),
        ],
        out_specs=pl.BlockSpec((CHUNK, ch), lambda c: (c, 0)),
        out_shape=jax.ShapeDtypeStruct((lp, ch), F32),
        scratch_shapes=[pltpu.VMEM((CHUNK + 16, ch), F32)],
        compiler_params=_cparams(("arbitrary",)),
        name="ssd_conv",
    )(xbc, xbc, xbc, conv_w, conv_b.reshape(1, ch))


def _ssd_direction(d, ce, x_ref, b_ref, c_ref, dt_ref, dtb_ref, acoef_ref, tri_ref, y_ref, state_ref):
    tri = tri_ref[d]
    trib = tri > 0.5
    row = ce * CHUNK + lax.broadcasted_iota(I32, (CHUNK, CHUNK), 0)
    dt = jnp.where(row >= PAD, _softplus(dt_ref[...] + dtb_ref[d]), 0.0)
    a = dt * acoef_ref[d]
    a_cum = jnp.dot(tri, a, precision=HIGHEST, preferred_element_type=F32)
    a_cum_t = a_cum.T
    a_tot = jnp.sum(a, axis=0, keepdims=True)
    e_cum = jnp.exp(a_cum)
    d_out = jnp.exp(a_tot - a_cum)
    e_tot = jnp.exp(a_tot)
    lane = lax.broadcasted_iota(I32, (CHUNK, CHUNK), 1)
    first = lane < SSD_HEAD_DIM
    first_row = first[0:1, :]

    def pair_cols(m, h0):
        return jnp.where(first, m[:, h0:h0 + 1], m[:, h0 + 1:h0 + 2])

    hpg = SSD_HEADS // SSD_GROUPS
    for g in range(SSD_GROUPS):
        bg = b_ref[:, g * SSD_STATE:(g + 1) * SSD_STATE]
        cg = c_ref[:, g * SSD_STATE:(g + 1) * SSD_STATE].astype(BF16)
        cb = _bdot(cg, bg.astype(BF16), NT)
        bt = bg.T.astype(BF16)
        for pp in range(hpg // 2):
            h0 = g * hpg + 2 * pp
            p_idx = d * (SSD_HEADS // 2) + h0 // 2
            lo = h0 * SSD_HEAD_DIM
            x_p = x_ref[:, lo:lo + 2 * SSD_HEAD_DIM]
            xdt = x_p * pair_cols(dt, h0)
            xw = (xdt * pair_cols(d_out, h0)).astype(BF16)
            xdt_b = xdt.astype(BF16)
            yd = []
            for hh in (h0, h0 + 1):
                diff = a_cum[:, hh:hh + 1] - a_cum_t[hh:hh + 1, :]
                dec = jnp.exp(jnp.where(trib, diff, NEG))
                yd.append(_bdot((cb * dec).astype(BF16), xdt_b))
            y_diag = jnp.where(first, yd[0], yd[1])
            st_prev = state_ref[p_idx]
            y_off = _bdot(cg, st_prev.astype(BF16)) * pair_cols(e_cum, h0)
            y_ref[:, lo:lo + 2 * SSD_HEAD_DIM] = y_diag + y_off
            st_new = _bdot(bt, xw)
            cdec = jnp.where(first_row, e_tot[:, h0:h0 + 1], e_tot[:, h0 + 1:h0 + 2])
            state_ref[p_idx] = st_prev * cdec + st_new


def _ssd_scan_kernel(xf_ref, bf_ref, cf_ref, dtf_ref, xb_ref, bb_ref, cb_ref, dtbk_ref, dtb_ref, acoef_ref, tri_ref,
                     yf_ref, yb_ref, state_ref):
    c = pl.program_id(0)
    nch = pl.num_programs(0)

    @pl.when(c == 0)
    def _():
        state_ref[...] = jnp.zeros(state_ref.shape, F32)

    _ssd_direction(0, c, xf_ref, bf_ref, cf_ref, dtf_ref, dtb_ref, acoef_ref, tri_ref, yf_ref, state_ref)
    _ssd_direction(1, nch - 1 - c, xb_ref, bb_ref, cb_ref, dtbk_ref, dtb_ref, acoef_ref, tri_ref, yb_ref, state_ref)


def ssd_scan(xbc_act, dt_raw, dt_bias_rows, acoef_rows, tri2):
    lp = xbc_act.shape[0]
    nch = lp // CHUNK
    gs = SSD_GROUPS * SSD_STATE

    def specs(chunk, d):
        return [
            pl.BlockSpec((CHUNK, SSD_INNER), lambda c: (chunk(c), 0)),
            pl.BlockSpec((CHUNK, gs), lambda c: (chunk(c), SSD_INNER // gs)),
            pl.BlockSpec((CHUNK, gs), lambda c: (chunk(c), SSD_INNER // gs + 1)),
            pl.BlockSpec((CHUNK, CHUNK), lambda c: (chunk(c), d)),
        ]

    fwd = lambda c: c
    bwd = lambda c: nch - 1 - c
    return pl.pallas_call(
        _ssd_scan_kernel,
        grid=(nch,),
        in_specs=specs(fwd, 0) + specs(bwd, 1) + [
            pl.BlockSpec((2, 1, CHUNK), lambda c: (0, 0, 0)),
            pl.BlockSpec((2, 1, CHUNK), lambda c: (0, 0, 0)),
            pl.BlockSpec((2, CHUNK, CHUNK), lambda c: (0, 0, 0)),
        ],
        out_specs=[pl.BlockSpec((CHUNK, SSD_INNER), lambda c: (c, 0)),
                   pl.BlockSpec((CHUNK, SSD_INNER), lambda c: (nch - 1 - c, 0))],
        out_shape=[jax.ShapeDtypeStruct((lp, SSD_INNER), F32), jax.ShapeDtypeStruct((lp, SSD_INNER), F32)],
        scratch_shapes=[pltpu.VMEM((SSD_HEADS, SSD_STATE, 2 * SSD_HEAD_DIM), F32)],
        compiler_params=_cparams(("arbitrary",)),
        name="ssd_scan",
    )(xbc_act, xbc_act, xbc_act, dt_raw, xbc_act, xbc_act, xbc_act, dt_raw, dt_bias_rows, acoef_rows, tri2)


def _ssd_final_kernel(yf_ref, yb_ref, x_ref, z_ref, dskip_ref, nw_ref, o_ref):
    y = yf_ref[...] + yb_ref[...] + dskip_ref[...] * x_ref[...]
    z = z_ref[...]
    y = y * (z * _sigmoid(z))
    gw = SSD_INNER // SSD_GROUPS
    for g in range(SSD_GROUPS):
        yg = y[:, g * gw:(g + 1) * gw]
        yg = yg * lax.rsqrt(jnp.mean(yg * yg, axis=-1, keepdims=True) + NORM_EPS)
        o_ref[:, g * gw:(g + 1) * gw] = (yg * nw_ref[:, g * gw:(g + 1) * gw]).astype(o_ref.dtype)


def ssd_final(yf, yb, xbc_act, z, d_skip_row, norm_w):
    lp = xbc_act.shape[0]
    blk = pl.BlockSpec((CHUNK, SSD_INNER), lambda c: (c, 0))
    row = pl.BlockSpec((1, SSD_INNER), lambda c: (0, 0))
    return pl.pallas_call(
        _ssd_final_kernel,
        grid=(lp // CHUNK,),
        in_specs=[blk, blk, blk, blk, row, row],
        out_specs=blk,
        out_shape=jax.ShapeDtypeStruct((lp, SSD_INNER), BF16),
        compiler_params=_cparams(("arbitrary",)),
        name="ssd_final",
    )(yf, yb, xbc_act, z, d_skip_row, norm_w.reshape(1, SSD_INNER))


def _attn_kernel(slope_ref, sink_ref, q_ref, km_ref, kp_ref, ks_ref, kn_ref, vm_ref, vp_ref, vs_ref, vn_ref, o_ref):
    b = pl.program_id(0)
    nb = pl.num_programs(0)
    scale = ATT_HEAD_DIM ** -0.5

    qpos = b * CHUNK - PAD + lax.broadcasted_iota(I32, (CHUNK, 4 * CHUNK), 0)
    col = lax.broadcasted_iota(I32, (CHUNK, 4 * CHUNK), 1)
    slot = col // CHUNK
    j = col - slot * CHUNK
    kblock = jnp.where(slot == 0, 0, b - 2 + slot)
    kpos = kblock * CHUNK - PAD + j
    dist = jnp.abs(qpos - kpos)
    meta_ok = (slot == 0) & (kpos >= 0)
    real_ok = (slot > 0) & (kblock >= 1) & (kblock < nb) & (kpos >= N_META) & (dist <= ATT_WINDOW)
    valid = meta_ok | real_ok
    alibi = jnp.where(real_ok & (qpos >= N_META), dist.astype(F32), 0.0)
    q_real = qpos[:, 0:ATT_HEAD_DIM] >= 0

    for hk in range(ATT_KV_HEADS):
        ksl = slice(hk * ATT_HEAD_DIM, (hk + 1) * ATT_HEAD_DIM)
        kcat = jnp.concatenate([km_ref[:, ksl], kp_ref[:, ksl], ks_ref[:, ksl], kn_ref[:, ksl]],
                               axis=0).astype(BF16)
        vcat = jnp.concatenate([vm_ref[:, ksl], vp_ref[:, ksl], vs_ref[:, ksl], vn_ref[:, ksl]],
                               axis=0).astype(BF16)
        for g in range(ATT_GRP):
            h = hk * ATT_GRP + g
            hsl = slice(h * ATT_HEAD_DIM, (h + 1) * ATT_HEAD_DIM)
            qg = q_ref[:, hsl].astype(BF16)
            s = _bdot(qg, kcat, NT) * scale
            s = jnp.where(valid, s - slope_ref[h] * alibi, NEG)
            sink = sink_ref[h]
            m = jnp.maximum(jnp.max(s, axis=-1, keepdims=True), sink)
            p = jnp.exp(s - m)
            denom = jnp.sum(p, axis=-1, keepdims=True) + jnp.exp(sink - m)
            o = _bdot(p.astype(BF16), vcat) / denom
            o_ref[:, hsl] = jnp.where(q_real, o, 0.0).astype(o_ref.dtype)


def window_attention(qkv, slopes, sink):
    lp = qkv.shape[0]
    nb = lp // CHUNK
    qw = ATT_Q_HEADS * ATT_HEAD_DIM
    kvw = ATT_KV_HEADS * ATT_HEAD_DIM
    kcol = qw // kvw
    vcol = kcol + 1

    def kv_spec(col, which):
        def imap(b, *_):
            if which == 0:
                blk = 0
            else:
                blk = jnp.clip(b - 2 + which, 0, nb - 1)
            return (blk, col)
        return pl.BlockSpec((CHUNK, kvw), imap)

    return pl.pallas_call(
        _attn_kernel,
        grid_spec=pltpu.PrefetchScalarGridSpec(
            num_scalar_prefetch=2,
            grid=(nb,),
            in_specs=[pl.BlockSpec((CHUNK, qw), lambda b, *_: (b, 0))]
            + [kv_spec(kcol, w) for w in range(4)] + [kv_spec(vcol, w) for w in range(4)],
            out_specs=pl.BlockSpec((CHUNK, qw), lambda b, *_: (b, 0)),
        ),
        out_shape=jax.ShapeDtypeStruct((lp, qw), BF16),
        compiler_params=_cparams(("arbitrary",)),
        name="window_attention",
    )(slopes, sink, *([qkv] * 9))


def _mlstm_direction(d, ce, q_ref, k_ref, v_ref, g_ref, gb_ref, tri_ref, h_ref, c_ref, n_ref, m_ref):
    tri = tri_ref[d]
    trib = tri > 0.5
    row = ce * CHUNK + lax.broadcasted_iota(I32, (CHUNK, CHUNK), 0)
    pre = g_ref[...] + gb_ref[d]
    real = row >= PAD
    real_col = real[:, 0:1]
    li_all = jnp.where(real, pre, 0.0)
    lf_all = jnp.where(real, -_softplus(-pre), 0.0)
    bl_all = jnp.dot(tri, lf_all, precision=HIGHEST, preferred_element_type=F32)
    li_t = li_all.T
    bl_t = bl_all.T
    g_all = jnp.sum(lf_all, axis=0, keepdims=True)

    for hd in range(ML_HEADS):
        fl = hd + ML_F_LANE
        li_col = li_all[:, hd:hd + 1]
        li_row = li_t[hd:hd + 1, :]
        bl_col = bl_all[:, fl:fl + 1]
        bl_row = bl_t[fl:fl + 1, :]
        g_tot = g_all[:, fl:fl + 1]
        st = d * ML_HEADS + hd
        ksl = slice(hd * ML_QK_DIM, (hd + 1) * ML_QK_DIM)
        vsl = slice(hd * ML_V_DIM, (hd + 1) * ML_V_DIM)

        qf = jnp.where(real_col, q_ref[:, ksl], 0.0)
        q = qf.astype(BF16)
        kf = jnp.where(real_col, k_ref[:, ksl], 0.0) * (ML_QK_DIM ** -0.5)
        kt = kf.T.astype(BF16)
        v = jnp.where(real_col, v_ref[:, vsl], 0.0)
        vb = v.astype(BF16)

        m_prev = m_ref[st, 0:1, 0:1]
        n_prev = n_ref[st]
        c_prev = c_ref[st]

        dmat = jnp.where(trib, bl_col - bl_row + li_row, NEG)
        m_inter = bl_col + m_prev
        m_t = jnp.maximum(jnp.max(dmat, axis=1, keepdims=True), m_inter)
        qk = jnp.dot(q, kt, preferred_element_type=F32)
        p = jnp.exp(dmat - m_t) * qk
        w_inter = jnp.exp(m_inter - m_t)
        num = (jnp.dot(p.astype(BF16), vb, preferred_element_type=F32)
               + w_inter * jnp.dot(q, c_prev.astype(BF16), preferred_element_type=F32))
        qn = jnp.sum(qf * n_prev, axis=1, keepdims=True)
        den = jnp.sum(p, axis=1, keepdims=True) + w_inter * qn
        h_ref[:, vsl] = num / jnp.maximum(jnp.abs(den), jnp.exp(-m_t))

        a_col = g_tot - bl_col + li_col
        m_loc = jnp.max(a_col, axis=0, keepdims=True)
        wa = jnp.exp(a_col - m_loc)
        c_loc = jnp.dot(kt, (wa * v).astype(BF16), preferred_element_type=F32)
        n_loc = jnp.sum(wa * kf, axis=0, keepdims=True)
        m_new = jnp.maximum(g_tot + m_prev, m_loc)
        s_old = jnp.exp(g_tot + m_prev - m_new)
        s_new = jnp.exp(m_loc - m_new)
        c_ref[st] = s_old * c_prev + s_new * c_loc
        n_ref[st] = s_old * n_prev + s_new * n_loc
        m_ref[st] = jnp.broadcast_to(m_new, m_ref.shape[1:])


def _mlstm_scan_kernel(qf_ref, kf_ref, vf_ref, gf_ref, qb_ref, kb_ref, vb_ref, gbk_ref, gb_ref, tri_ref,
                       hf_ref, hb_ref, c_ref, n_ref, m_ref):
    c = pl.program_id(0)
    nch = pl.num_programs(0)

    @pl.when(c == 0)
    def _():
        c_ref[...] = jnp.zeros(c_ref.shape, F32)
        n_ref[...] = jnp.zeros(n_ref.shape, F32)
        m_ref[...] = jnp.zeros(m_ref.shape, F32)

    _mlstm_direction(0, c, qf_ref, kf_ref, vf_ref, gf_ref, gb_ref, tri_ref, hf_ref, c_ref, n_ref, m_ref)
    _mlstm_direction(1, nch - 1 - c, qb_ref, kb_ref, vb_ref, gbk_ref, gb_ref, tri_ref, hb_ref, c_ref, n_ref, m_ref)


def mlstm_scan(qkvo, gates, gate_bias_rows, tri2):
    lp = qkvo.shape[0]
    nch = lp // CHUNK
    qk_w = ML_HEADS * ML_QK_DIM

    def specs(chunk, d):
        return [
            pl.BlockSpec((CHUNK, qk_w), lambda c: (chunk(c), 0)),
            pl.BlockSpec((CHUNK, qk_w), lambda c: (chunk(c), 1)),
            pl.BlockSpec((CHUNK, ML_WIDTH), lambda c: (chunk(c), 2 * qk_w // ML_WIDTH)),
            pl.BlockSpec((CHUNK, CHUNK), lambda c: (chunk(c), d)),
        ]

    fwd = lambda c: c
    bwd = lambda c: nch - 1 - c
    n_state = 2 * ML_HEADS
    return pl.pallas_call(
        _mlstm_scan_kernel,
        grid=(nch,),
        in_specs=specs(fwd, 0) + specs(bwd, 1) + [
            pl.BlockSpec((2, 1, CHUNK), lambda c: (0, 0, 0)),
            pl.BlockSpec((2, CHUNK, CHUNK), lambda c: (0, 0, 0)),
        ],
        out_specs=[pl.BlockSpec((CHUNK, ML_WIDTH), lambda c: (c, 0)),
                   pl.BlockSpec((CHUNK, ML_WIDTH), lambda c: (nch - 1 - c, 0))],
        out_shape=[jax.ShapeDtypeStruct((lp, ML_WIDTH), F32), jax.ShapeDtypeStruct((lp, ML_WIDTH), F32)],
        scratch_shapes=[pltpu.VMEM((n_state, ML_QK_DIM, ML_V_DIM), F32), pltpu.VMEM((n_state, 1, ML_QK_DIM), F32),
                        pltpu.VMEM((n_state, 8, CHUNK), F32)],
        compiler_params=_cparams(("arbitrary",)),
        name="mlstm_scan",
    )(qkvo, qkvo, qkvo, gates, qkvo, qkvo, qkvo, gates, gate_bias_rows, tri2)


def _mlstm_final_kernel(hf_ref, hb_ref, o_ref_in, nw_ref, y_ref):
    h = hf_ref[...] + hb_ref[...]
    o = o_ref_in[...]
    for hd in range(ML_HEADS):
        sl = slice(hd * ML_V_DIM, (hd + 1) * ML_V_DIM)
        hh = h[:, sl]
        hh = hh * lax.rsqrt(jnp.mean(hh * hh, axis=-1, keepdims=True) + NORM_EPS) * nw_ref[:, sl]
        y_ref[:, sl] = (_sigmoid(o[:, sl]) * hh).astype(y_ref.dtype)


def mlstm_final(hf, hb, qkvo, head_norm):
    lp = qkvo.shape[0]
    ocol = (2 * ML_HEADS * ML_QK_DIM + ML_WIDTH) // ML_WIDTH
    return pl.pallas_call(
        _mlstm_final_kernel,
        grid=(lp // CHUNK,),
        in_specs=[
            pl.BlockSpec((CHUNK, ML_WIDTH), lambda c: (c, 0)),
            pl.BlockSpec((CHUNK, ML_WIDTH), lambda c: (c, 0)),
            pl.BlockSpec((CHUNK, ML_WIDTH), lambda c: (c, ocol)),
            pl.BlockSpec((1, ML_WIDTH), lambda c: (0, 0)),
        ],
        out_specs=pl.BlockSpec((CHUNK, ML_WIDTH), lambda c: (c, 0)),
        out_shape=jax.ShapeDtypeStruct((lp, ML_WIDTH), BF16),
        compiler_params=_cparams(("arbitrary",)),
        name="mlstm_final",
    )(hf, hb, qkvo, head_norm.reshape(1, ML_WIDTH))


def _dft_factors(n):
    best = None
    for n1 in range(8, n + 1, 8):
        if n % n1 == 0:
            n2 = n // n1
            if best is None or n1 + n2 < best[0] + best[1]:
                best = (n1, n2)
    assert best is not None
    return best


def _split_bf16(a):
    hi = a.astype(BF16)
    return hi, (a - hi.astype(F32)).astype(BF16)


def _dot3(a, b):
    return _bdot(a[0], b[0]) + _bdot(a[1], b[0]) + _bdot(a[0], b[1])


def _fnet_stage1_kernel(x_ref, c2h_ref, c2l_ref, s2h_ref, s2l_ref, twc_ref, tws_ref, tr_ref, ti_ref, *, n2):
    x = x_ref[...]
    rows = lax.broadcasted_iota(I32, x.shape, 0)
    xs = _split_bf16(jnp.where(rows < n2, x, 0.0))
    cx = _dot3((c2h_ref[...], c2l_ref[...]), xs)
    sx = _dot3((s2h_ref[...], s2l_ref[...]), xs)
    twc = twc_ref[0]
    tws = tws_ref[0]
    tr_ref[...] = cx * twc - sx * tws
    ti_ref[...] = -(cx * tws + sx * twc)


def _fnet_stage2_kernel(tr_ref, ti_ref, c1h_ref, c1l_ref, s1h_ref, s1l_ref, cch_ref, ccl_ref, sch_ref, scl_ref, o_ref,
                        *, scale):
    tr = _split_bf16(tr_ref[0])
    ti = _split_bf16(ti_ref[0])
    c1 = (c1h_ref[...], c1l_ref[...])
    s1 = (s1h_ref[...], s1l_ref[...])
    yr = _dot3(c1, tr) + _dot3(s1, ti)
    yi = _dot3(c1, ti) - _dot3(s1, tr)
    cc = (cch_ref[...], ccl_ref[...])
    sc = (sch_ref[...], scl_ref[...])
    for g in range(FN_GROUPS):
        sl = slice(g * FN_GROUP_DIM, (g + 1) * FN_GROUP_DIM)
        out = _dot3(_split_bf16(yr[:, sl]), cc) + _dot3(_split_bf16(yi[:, sl]), sc)
        o_ref[:, sl] = out * scale


def fourier_mixer(u_fn):
    n, ch = u_fn.shape
    n1, n2 = _dft_factors(n)
    n2p = -(-n2 // 8) * 8
    two_pi = 2.0 * np.pi

    def cs(size, pad):
        idx = np.arange(size)
        ang = two_pi * ((idx[:, None] * idx[None, :]) % size) / size
        out = []
        for fn in (np.cos, np.sin):
            m = np.zeros((pad, pad), np.float32)
            m[:size, :size] = fn(ang)
            hi = jnp.asarray(m).astype(BF16)
            out += [hi, (jnp.asarray(m) - hi.astype(F32)).astype(BF16)]
        return out

    dft2 = cs(n2, n2p)
    dft1 = cs(n1, n1)
    dftc = cs(FN_GROUP_DIM, FN_GROUP_DIM)
    tw_ang = two_pi * ((np.arange(n1)[:, None] * np.arange(n2)[None, :]) % n) / n
    twc = np.zeros((n1, n2p, 1), np.float32)
    tws = np.zeros((n1, n2p, 1), np.float32)
    twc[:, :n2, 0] = np.cos(tw_ang)
    tws[:, :n2, 0] = np.sin(tw_ang)

    def full(shape):
        return pl.BlockSpec(shape, lambda i: (0,) * len(shape))

    x2 = u_fn.reshape(n2, n1 * ch)
    tr, ti = pl.pallas_call(
        functools.partial(_fnet_stage1_kernel, n2=n2),
        grid=(n1,),
        in_specs=[pl.BlockSpec((n2p, ch), lambda i: (0, i))] + [full((n2p, n2p))] * 4 + [
            pl.BlockSpec((1, n2p, 1), lambda i: (i, 0, 0)),
            pl.BlockSpec((1, n2p, 1), lambda i: (i, 0, 0)),
        ],
        out_specs=[pl.BlockSpec((n2p, ch), lambda i: (0, i)), pl.BlockSpec((n2p, ch), lambda i: (0, i))],
        out_shape=[jax.ShapeDtypeStruct((n2p, n1 * ch), F32), jax.ShapeDtypeStruct((n2p, n1 * ch), F32)],
        compiler_params=_cparams(("arbitrary",)),
        name="fnet_stage1",
    )(x2, *dft2, jnp.asarray(twc), jnp.asarray(tws))

    tr3 = tr.reshape(n2p, n1, ch)
    ti3 = ti.reshape(n2p, n1, ch)
    scale = 1.0 / math.sqrt(float(n) * FN_GROUP_DIM)
    out = pl.pallas_call(
        functools.partial(_fnet_stage2_kernel, scale=scale),
        grid=(n2,),
        in_specs=[
            pl.BlockSpec((1, n1, ch), lambda i: (i, 0, 0)),
            pl.BlockSpec((1, n1, ch), lambda i: (i, 0, 0)),
        ] + [full((n1, n1))] * 4 + [full((FN_GROUP_DIM, FN_GROUP_DIM))] * 4,
        out_specs=pl.BlockSpec((n1, ch), lambda i: (0, i)),
        out_shape=jax.ShapeDtypeStruct((n1, n2 * ch), F32),
        compiler_params=_cparams(("arbitrary",)),
        name="fnet_stage2",
    )(tr3, ti3, *dft1, *dftc)
    return out.reshape(n, ch)


def _cap_sizes(lp):
    n_tok = lp - PAD
    cap = EC_FACTOR * n_tok // N_EXPERTS
    capp = -(-cap // 16) * 16
    return cap, capp


def _topk_kernel(lg_ref, su_ref, pos_ref, aff_ref, start_ref, idx_ref, rank_scr, pos_scr, *, cap, capp, jb):
    ne, lp = lg_ref.shape
    nch = lp // CHUNK
    lg = lg_ref[...]
    mx = jnp.max(lg, axis=0, keepdims=True)
    ex = jnp.exp(lg - mx)
    aff = ex / jnp.sum(ex, axis=0, keepdims=True)
    lane = lax.broadcasted_iota(I32, (ne, lp), 1)
    real = lane >= PAD
    bits = jnp.where(real, pltpu.bitcast(aff, I32), -1)
    capf = float(cap)

    def bisect(i, thr):
        cand = thr | jnp.left_shift(jnp.int32(1), 30 - i)
        cnt = jnp.sum(jnp.where(bits >= cand, 1.0, 0.0), axis=1, keepdims=True)
        return jnp.where(cnt >= capf, cand, thr)

    thr = lax.fori_loop(0, 31, bisect, jnp.zeros((ne, 1), I32))
    gt = bits > thr
    eq = bits == thr
    need = capf - jnp.sum(jnp.where(gt, 1.0, 0.0), axis=1, keepdims=True)

    su = su_ref[...]

    def excl_cumsum(mask_f32, out_scr, want_starts):
        carry = jnp.zeros((ne, 1), F32)
        starts = jnp.zeros((ne, CHUNK), F32)
        lane_c = lax.broadcasted_iota(I32, (ne, CHUNK), 1)
        for c in range(nch):
            tile = mask_f32[:, c * CHUNK:(c + 1) * CHUNK]
            within = jnp.dot(tile.astype(BF16), su, preferred_element_type=F32)
            out_scr[:, c * CHUNK:(c + 1) * CHUNK] = within + carry
            if want_starts:
                starts = jnp.where(lane_c == c, carry, starts)
            carry = carry + jnp.sum(tile, axis=1, keepdims=True)
        if want_starts:
            starts = jnp.where(lane_c == nch, carry, starts)
        return starts

    excl_cumsum(jnp.where(eq, 1.0, 0.0), rank_scr, False)
    sel = gt | (eq & (rank_scr[...] < need))
    starts = excl_cumsum(jnp.where(sel, 1.0, 0.0), pos_scr, True)
    posf = jnp.where(sel, pos_scr[...], -1.0)
    pos_scr[...] = posf
    pos_ref[...] = posf.astype(I32)
    aff_ref[...] = jnp.where(sel, aff, 0.0)
    start_ref[...] = starts.astype(I32)

    tok = lax.broadcasted_iota(I32, (jb, lp), 1).astype(F32)
    lane_e = lax.broadcasted_iota(I32, (jb, ne), 1)
    for blk in range(capp // jb):
        jcol = (blk * jb + lax.broadcasted_iota(I32, (jb, 1), 0)).astype(F32)

        def per_expert(e, acc):
            prow = pos_scr[pl.ds(e, 1), :]
            hit = jnp.sum(jnp.where(prow == jcol, tok, 0.0), axis=1, keepdims=True)
            return jnp.where(lane_e == e, hit, acc)

        acc = lax.fori_loop(0, ne, per_expert, jnp.zeros((jb, ne), F32))
        idx_ref[blk * jb:(blk + 1) * jb, :] = acc.astype(I32)


def topk_route(logits_t):
    ne, lp = logits_t.shape
    cap, capp = _cap_sizes(lp)
    jb = 80 if capp % 80 == 0 else 16
    su = jnp.asarray(np.triu(np.ones((CHUNK, CHUNK), np.float32), 1), BF16)
    return pl.pallas_call(
        functools.partial(_topk_kernel, cap=cap, capp=capp, jb=jb),
        out_shape=[jax.ShapeDtypeStruct((ne, lp), I32), jax.ShapeDtypeStruct((ne, lp), F32),
                   jax.ShapeDtypeStruct((ne, CHUNK), I32), jax.ShapeDtypeStruct((capp, ne), I32)],
        scratch_shapes=[pltpu.VMEM((ne, lp), F32), pltpu.VMEM((ne, lp), F32)],
        compiler_params=pltpu.CompilerParams(vmem_limit_bytes=VMEM_LIMIT),
        name="topk_route",
    )(logits_t, su)


GATHER_UNROLL = 8


def _gather_kernel(idx_ref, u_hbm, o_ref, stage_ref, sem):
    e = pl.program_id(0)
    ne = pl.num_programs(0)
    capp = stage_ref.shape[1]

    def issue(eq, slot):
        def body(jo, carry):
            for ji in range(GATHER_UNROLL):
                j = jo * GATHER_UNROLL + ji
                t = idx_ref[eq, j]
                pltpu.make_async_copy(u_hbm.at[pl.ds(t, 1), :], stage_ref.at[slot, pl.ds(j, 1), :],
                                      sem.at[slot]).start()
            return carry

        lax.fori_loop(0, capp // GATHER_UNROLL, body, 0)

    @pl.when(e == 0)
    def _():
        issue(e, 0)

    @pl.when(e + 1 < ne)
    def _():
        issue(e + 1, (e + 1) % 2)

    slot = e % 2
    pltpu.make_async_copy(u_hbm.at[pl.ds(0, capp), :], stage_ref.at[slot], sem.at[slot]).wait()
    o_ref[0] = stage_ref[slot].astype(o_ref.dtype)


def gather_tokens(idx, u):
    ne, capp = idx.shape
    lp, d = u.shape
    assert capp % GATHER_UNROLL == 0
    return pl.pallas_call(
        _gather_kernel,
        grid_spec=pltpu.PrefetchScalarGridSpec(
            num_scalar_prefetch=1,
            grid=(ne,),
            in_specs=[pl.BlockSpec(memory_space=pl.ANY)],
            out_specs=pl.BlockSpec((1, capp, d), lambda e, *_: (e, 0, 0)),
            scratch_shapes=[pltpu.VMEM((2, capp, d), F32), pltpu.SemaphoreType.DMA((2,))],
        ),
        out_shape=jax.ShapeDtypeStruct((ne, capp, d), BF16),
        compiler_params=_cparams(("arbitrary",)),
        name="moe_gather",
    )(idx, u)


def _ffn_kernel(xs_ref, wg_ref, wu_ref, wd_ref, o_ref, hdn_ref, *, n_ff):
    s = pl.program_id(1)

    @pl.when(s < n_ff)
    def _():
        x = xs_ref[0]
        g = jnp.dot(x, wg_ref[0, 0].astype(BF16), preferred_element_type=F32)
        u = jnp.dot(x, wu_ref[0, 0].astype(BF16), preferred_element_type=F32)
        hdn_ref[s] = (g * _sigmoid(g) * u).astype(BF16)

    @pl.when(s >= n_ff)
    def _():
        wd = wd_ref[0, 0].astype(BF16)
        acc = None
        for j in range(n_ff):
            part = jnp.dot(hdn_ref[j], wd[j * FF_TILE:(j + 1) * FF_TILE, :], preferred_element_type=F32)
            acc = part if acc is None else acc + part
        o_ref[0] = acc


def expert_ffn(xs, w_gate, w_up, w_down, layer):
    ne, capp, d = xs.shape
    n_ff = EXPERT_FF // FF_TILE
    n_out = d // OUT_TILE
    return pl.pallas_call(
        functools.partial(_ffn_kernel, n_ff=n_ff),
        grid=(ne, n_ff + n_out),
        in_specs=[
            pl.BlockSpec((1, capp, d), lambda e, s: (e, 0, 0)),
            pl.BlockSpec((1, 1, d, FF_TILE), lambda e, s: (layer, e, 0, jnp.minimum(s, n_ff - 1))),
            pl.BlockSpec((1, 1, d, FF_TILE), lambda e, s: (layer, e, 0, jnp.minimum(s, n_ff - 1))),
            pl.BlockSpec((1, 1, EXPERT_FF, OUT_TILE), lambda e, s: (layer, e, 0, jnp.maximum(s - n_ff, 0))),
        ],
        out_specs=pl.BlockSpec((1, capp, OUT_TILE), lambda e, s: (e, 0, jnp.maximum(s - n_ff, 0))),
        out_shape=jax.ShapeDtypeStruct((ne, capp, d), F32),
        scratch_shapes=[pltpu.VMEM((n_ff, capp, FF_TILE), BF16)],
        compiler_params=_cparams(("arbitrary", "arbitrary")),
        name="expert_ffn",
    )(xs, w_gate, w_up, w_down)


def _combine_kernel(start_ref, h_ref, pos_ref, aff_ref, nw_ref, out_hbm, *refs, capp, rows, n_half, emit_h):
    if emit_h:
        o_ref, y_ref, acc_ref, g_ref, sem = refs
    else:
        y_ref, acc_ref, g_ref, sem = refs
    b = pl.program_id(0)
    hh = pl.program_id(1)
    nb = pl.num_programs(0)
    ne = pos_ref.shape[0]
    eh = ne // n_half

    def row_copy(src, dst, slot, n_rows=1):
        return pltpu.make_async_copy(out_hbm.at[pl.ds(src, n_rows), :], g_ref.at[slot, pl.ds(dst, n_rows), :],
                                     sem.at[slot])

    def block_rows(bq, hq):
        s0 = [start_ref[hq * eh + i, bq] for i in range(eh)]
        n = [start_ref[hq * eh + i, bq + 1] - s0[i] for i in range(eh)]
        return s0, n

    def issue(bq, hq, slot):
        s0, n = block_rows(bq, hq)
        off = jnp.int32(0)
        for i in range(eh):
            base = (hq * eh + i) * capp + s0[i]

            def body(j, carry, base=base, off=off):
                row_copy(base + j, off + j, slot).start()
                return carry

            lax.fori_loop(0, n[i], body, 0)
            off = off + n[i]

    @pl.when((b == 0) & (hh == 0))
    def _():
        g_ref[...] = jnp.zeros(g_ref.shape, F32)
        issue(b, hh, 0)

    @pl.when(hh + 1 < n_half)
    def _():
        issue(b, hh + 1, 1 - hh % 2)

    if n_half % 2 == 0:
        @pl.when((hh + 1 == n_half) & (b + 1 < nb))
        def _():
            issue(b + 1, 0, 0)

    slot = hh % 2
    s0, n = block_rows(b, hh)
    offs = []
    off = jnp.int32(0)
    for i in range(eh):
        offs.append(off)
        off = off + n[i]
    total = off

    stage_rows = g_ref.shape[1]
    bit = 1
    while bit <= stage_rows:
        @pl.when((total & bit) != 0)
        def _(bit=bit):
            row_copy(0, 0, slot, n_rows=bit).wait()
        bit *= 2

    @pl.when(hh == 0)
    def _():
        acc_ref[...] = h_ref[...]

    r_iota = lax.broadcasted_iota(I32, (rows, CHUNK), 0)
    d = acc_ref.shape[1]

    def chunk(ci, carry):
        r0 = pl.multiple_of(ci * rows, rows)
        rr = r_iota + r0
        onehot = jnp.zeros((rows, CHUNK), F32)
        gate = jnp.zeros((rows, CHUNK), F32)
        for i in range(eh):
            prow = pos_ref[pl.ds(hh * eh + i, 1), :]
            arow = aff_ref[pl.ds(hh * eh + i, 1), :]
            tgt = jnp.where(prow >= 0, prow - s0[i] + offs[i], -1)
            hit = rr == tgt
            onehot = jnp.where(hit, 1.0, onehot)
            gate = jnp.where(hit, arow, gate)
        gcol = jnp.sum(gate, axis=1, keepdims=True)
        oh = onehot.T.astype(BF16)
        for c0 in range(0, d, COMBINE_SLAB):
            gs = (g_ref[slot, pl.ds(r0, rows), c0:c0 + COMBINE_SLAB] * gcol).astype(BF16)
            acc_ref[:, c0:c0 + COMBINE_SLAB] += jnp.dot(oh, gs, preferred_element_type=F32)
        return carry

    lax.fori_loop(0, (total + rows - 1) // rows, chunk, 0)

    @pl.when(hh == n_half - 1)
    def _():
        x = acc_ref[...]
        if emit_h:
            o_ref[...] = x
        y = x * lax.rsqrt(jnp.mean(x * x, axis=-1, keepdims=True) + NORM_EPS) * nw_ref[...]
        y_ref[...] = y.astype(y_ref.dtype)


def moe_combine(h, pos, aff, starts, ffn_out, next_norm_w, last):
    lp, d = h.shape
    ne, capp, _ = ffn_out.shape
    rows = 2 * CHUNK
    n_half = 2
    stage_rows = (ne // n_half) * CHUNK
    blk = pl.BlockSpec((CHUNK, d), lambda b, hh, *_: (b, 0))
    if last:
        out_specs = pl.BlockSpec((CHUNK, d), lambda b, hh, *_: (jnp.maximum(b - 1, 0), 0))
        out_shape = jax.ShapeDtypeStruct((lp - CHUNK, d), F32)
    else:
        out_specs = [blk, blk]
        out_shape = [jax.ShapeDtypeStruct((lp, d), F32), jax.ShapeDtypeStruct((lp, d), BF16)]
    return pl.pallas_call(
        functools.partial(_combine_kernel, capp=capp, rows=rows, n_half=n_half, emit_h=not last),
        grid_spec=pltpu.PrefetchScalarGridSpec(
            num_scalar_prefetch=1,
            grid=(lp // CHUNK, n_half),
            in_specs=[
                blk,
                pl.BlockSpec((ne, CHUNK), lambda b, hh, *_: (0, b)),
                pl.BlockSpec((ne, CHUNK), lambda b, hh, *_: (0, b)),
                pl.BlockSpec((1, d), lambda b, hh, *_: (0, 0)),
                pl.BlockSpec(memory_space=pl.ANY),
            ],
            out_specs=out_specs,
            scratch_shapes=[pltpu.VMEM((CHUNK, d), F32), pltpu.VMEM((2, stage_rows, d), F32),
                            pltpu.SemaphoreType.DMA((2,))],
        ),
        out_shape=out_shape,
        compiler_params=_cparams(("arbitrary", "arbitrary")),
        name="moe_combine",
    )(starts, h, pos, aff, next_norm_w.reshape(1, d), ffn_out.reshape(ne * capp, d))


def moe_layer(h, norm_w, w_router, w_gate, w_up, w_down, layer, next_norm_w, last):
    u, logits_t = rmsnorm_router(h, norm_w, w_router)
    pos, aff, starts, idx_t = topk_route(logits_t)
    xs = gather_tokens(idx_t.T, u)
    ffn_out = expert_ffn(xs, w_gate, w_up, w_down, layer)
    return moe_combine(h, pos, aff, starts, ffn_out, next_norm_w, last)


def _scan_masks():
    lower = np.tril(np.ones((CHUNK, CHUNK), np.float32))
    return jnp.asarray(np.stack([lower, lower.T]))


def _pad_cols(w, width):
    return jnp.pad(w, ((0, 0), (0, width - w.shape[1])))


def _pad_rows(w, height):
    return jnp.pad(w, ((0, height - w.shape[0]), (0, 0)))


def ab_mixer(h, u, w_in, conv_w, conv_b, a_log, dt_bias, d_skip, ssd_norm, sink, w_out):
    zx_w = SSD_INNER + SSD_CONV_CH
    dt0 = zx_w
    qkv0 = zx_w + 2 * SSD_HEADS
    w_t = w_in.T
    z = matmul([u], w_t, 512, col0=0, n=SSD_INNER, w_t=True)
    xbc = matmul([u], w_t, 512, col0=SSD_INNER, n=SSD_CONV_CH, w_t=True)
    w_dt = jnp.concatenate([_pad_rows(w_t[dt0:dt0 + SSD_HEADS], CHUNK),
                            _pad_rows(w_t[dt0 + SSD_HEADS:qkv0], CHUNK)], axis=0)
    dt_raw = matmul([u], w_dt, 2 * CHUNK, w_t=True)
    qkv = matmul([u], w_t[qkv0:], 512, w_t=True)

    xbc_act = ssd_conv(xbc, conv_w, conv_b)
    dt_bias_rows = _pad_cols(dt_bias, CHUNK).reshape(2, 1, CHUNK)
    acoef_rows = _pad_cols(-jnp.exp(a_log), CHUNK).reshape(2, 1, CHUNK)
    yf, yb = ssd_scan(xbc_act, dt_raw, dt_bias_rows, acoef_rows, _scan_masks())
    d_skip_row = jnp.repeat(d_skip, SSD_HEAD_DIM).reshape(1, SSD_INNER)
    y_ssd = ssd_final(yf, yb, xbc_act, z, d_skip_row, ssd_norm)

    slopes = 2.0 ** (-8.0 * jnp.arange(1, ATT_Q_HEADS + 1, dtype=F32) / ATT_Q_HEADS)
    y_att = window_attention(qkv, slopes, sink)
    return matmul([y_ssd, y_att], w_out, 512, res=h)


def cd_mixer(h, u, w_in, i_bias, f_bias, head_norm, w_out):
    qkvo_w = 2 * ML_HEADS * ML_QK_DIM + 2 * ML_WIDTH
    i0 = qkvo_w
    f0 = i0 + 2 * ML_HEADS
    fn0 = f0 + 2 * ML_HEADS
    w_t = w_in.T
    qkvo = matmul([u], w_t, 512, col0=0, n=qkvo_w, w_t=True)

    def gate_block(dirn):
        wi = w_t[i0 + dirn * ML_HEADS:i0 + (dirn + 1) * ML_HEADS]
        wf = w_t[f0 + dirn * ML_HEADS:f0 + (dirn + 1) * ML_HEADS]
        return _pad_rows(jnp.concatenate([_pad_rows(wi, ML_F_LANE), wf], axis=0), CHUNK)

    def bias_block(dirn):
        row = jnp.concatenate([jnp.pad(i_bias[dirn], (0, ML_F_LANE - ML_HEADS)), f_bias[dirn]])
        return jnp.pad(row, (0, CHUNK - row.shape[0]))

    gates = matmul([u], jnp.concatenate([gate_block(0), gate_block(1)], axis=0), 2 * CHUNK, w_t=True)
    gate_bias_rows = jnp.stack([bias_block(0), bias_block(1)]).reshape(2, 1, CHUNK)
    hf, hb = mlstm_scan(qkvo, gates, gate_bias_rows, _scan_masks())
    y_ml = mlstm_final(hf, hb, qkvo, head_norm)

    u_fn = matmul([u], w_t[fn0:], 512, w_t=True)
    y_fn = fourier_mixer(u_fn[PAD:])
    y_fn = jnp.pad(y_fn, ((PAD, 0), (0, 0))).astype(BF16)
    return matmul([y_ml, y_fn], w_out, 512, res=h)


def kernel(x, meta_tokens, norm_mix, ab_w_in, ab_conv_w, ab_conv_b, ab_a_log, ab_dt_bias, ab_d_skip, ab_ssd_norm,
           ab_sink, ab_w_out, cd_w_in, cd_i_bias, cd_f_bias, cd_head_norm, cd_w_out, norm_ffn, moe_router,
           moe_w_gate, moe_w_up, moe_w_down, final_norm):
    bsz, seq, d = x.shape
    assert bsz == 1 and d == D_MODEL and seq % CHUNK == 0
    depth = norm_mix.shape[0]
    h = jnp.concatenate([jnp.zeros((PAD, d), F32), meta_tokens.astype(F32), x[0]], axis=0)
    u = rmsnorm(h, norm_mix[0], BF16)
    for layer in range(depth):
        j = layer // 2
        if layer % 2 == 0:
            h = ab_mixer(h, u, ab_w_in[j], ab_conv_w[j], ab_conv_b[j], ab_a_log[j], ab_dt_bias[j],
                         ab_d_skip[j], ab_ssd_norm[j], ab_sink[j], ab_w_out[j])
        else:
            h = cd_mixer(h, u, cd_w_in[j], cd_i_bias[j], cd_f_bias[j], cd_head_norm[j], cd_w_out[j])
        last = layer == depth - 1
        next_norm_w = final_norm if last else norm_mix[layer + 1]
        res = moe_layer(h, norm_ffn[layer], moe_router[layer], moe_w_gate, moe_w_up, moe_w_down, layer,
                        next_norm_w, last)
        if last:
            return res[None]
        h, u = res
```

```python
import functools
import math

import numpy as np
import jax
import jax.numpy as jnp
from jax import lax
from jax.experimental import pallas as pl
from jax.experimental.pallas import tpu as pltpu

F32 = jnp.float32
BF16 = jnp.bfloat16
I32 = jnp.int32
HIGHEST = lax.Precision.HIGHEST

D_MODEL = 4096
N_META = 16
CHUNK = 128
PAD = CHUNK - N_META
NORM_EPS = 1e-6
NEG = -1e30

SSD_HEADS = 32
SSD_HEAD_DIM = 64
SSD_INNER = SSD_HEADS * SSD_HEAD_DIM
SSD_GROUPS = 4
SSD_STATE = 128
SSD_CONV = 5
SSD_CONV_CH = SSD_INNER + 2 * SSD_GROUPS * SSD_STATE

ATT_Q_HEADS = 16
ATT_KV_HEADS = 4
ATT_HEAD_DIM = 128
ATT_WINDOW = 128
ATT_GRP = ATT_Q_HEADS // ATT_KV_HEADS

ML_HEADS = 6
ML_QK_DIM = 256
ML_V_DIM = 512
ML_WIDTH = ML_HEADS * ML_V_DIM
ML_F_LANE = 8

FN_GROUPS = 4
FN_GROUP_DIM = 256
FN_WIDTH = FN_GROUPS * FN_GROUP_DIM

N_EXPERTS = 16
EC_FACTOR = 2
EXPERT_FF = 1536
FF_TILE = 256
OUT_TILE = 512
COMBINE_SLAB = 512

VMEM_LIMIT = 56 * 1024 * 1024


def _cparams(sem):
    return pltpu.CompilerParams(dimension_semantics=sem, vmem_limit_bytes=VMEM_LIMIT)


def _row_tile(lp):
    return 640 if lp % 640 == 0 else CHUNK


def _sigmoid(x):
    return 1.0 / (1.0 + jnp.exp(-x))


def _softplus(x):
    return jnp.maximum(x, 0.0) + jnp.log1p(jnp.exp(-jnp.abs(x)))


NT = (((1,), (1,)), ((), ()))


def _bdot(a, b, dims=None):
    if dims is None:
        return jnp.dot(a, b, preferred_element_type=F32)
    return lax.dot_general(a, b, dims, preferred_element_type=F32)


def _split_bf16(a):
    hi = a.astype(BF16)
    return hi, (a - hi.astype(F32)).astype(BF16)


def _rmsnorm_kernel(h_ref, w_ref, o_ref):
    x = h_ref[...]
    y = x * lax.rsqrt(jnp.mean(x * x, axis=-1, keepdims=True) + NORM_EPS) * w_ref[...]
    o_ref[...] = y.astype(o_ref.dtype)


def rmsnorm(h, w, out_dtype):
    lp, d = h.shape
    tm = _row_tile(lp)
    return pl.pallas_call(
        _rmsnorm_kernel,
        grid=(lp // tm,),
        in_specs=[pl.BlockSpec((tm, d), lambda i: (i, 0)), pl.BlockSpec((1, d), lambda i: (0, 0))],
        out_specs=pl.BlockSpec((tm, d), lambda i: (i, 0)),
        out_shape=jax.ShapeDtypeStruct((lp, d), out_dtype),
        compiler_params=_cparams(("arbitrary",)),
        name="rmsnorm",
    )(h, w.reshape(1, d))


def _rmsnorm_router_kernel(h_ref, w_ref, wrt_ref, u_ref, lg_ref):
    x = h_ref[...]
    y = x * lax.rsqrt(jnp.mean(x * x, axis=-1, keepdims=True) + NORM_EPS) * w_ref[...]
    u_ref[...] = y
    wh, wl = _split_bf16(wrt_ref[...])
    yh, yl = _split_bf16(y)
    lg_ref[...] = _bdot(wh, yh, NT) + _bdot(wl, yh, NT) + _bdot(wh, yl, NT)


def rmsnorm_router(h, w, w_router):
    lp, d = h.shape
    tm = _row_tile(lp)
    return pl.pallas_call(
        _rmsnorm_router_kernel,
        grid=(lp // tm,),
        in_specs=[pl.BlockSpec((tm, d), lambda i: (i, 0)), pl.BlockSpec((1, d), lambda i: (0, 0)),
                  pl.BlockSpec((N_EXPERTS, d), lambda i: (0, 0))],
        out_specs=[pl.BlockSpec((tm, d), lambda i: (i, 0)), pl.BlockSpec((N_EXPERTS, tm), lambda i: (0, i))],
        out_shape=[jax.ShapeDtypeStruct((lp, d), F32), jax.ShapeDtypeStruct((N_EXPERTS, lp), F32)],
        compiler_params=_cparams(("arbitrary",)),
        name="rmsnorm_router",
    )(h, w.reshape(1, d), w_router.T)


def _matmul_kernel(*refs, k_splits, has_res, w_t):
    n_x = len(k_splits)
    x_refs = refs[:n_x]
    w_ref = refs[n_x]
    r_ref = refs[n_x + 1] if has_res else None
    o_ref = refs[n_x + 1 + int(has_res)]
    wb_ref = refs[n_x + 2 + int(has_res)]

    @pl.when(pl.program_id(1) == 0)
    def _():
        w = w_ref[...].T if w_t else w_ref[...]
        wb_ref[...] = w.astype(BF16)

    acc = None
    k0 = 0
    for x_ref, kk in zip(x_refs, k_splits):
        part = jnp.dot(x_ref[...], wb_ref[k0:k0 + kk, :], preferred_element_type=F32)
        acc = part if acc is None else acc + part
        k0 += kk
    if has_res:
        acc = acc + r_ref[...]
    o_ref[...] = acc.astype(o_ref.dtype)


def matmul(xs, w, tn, res=None, out_dtype=F32, col0=0, n=None, w_t=False):
    lp = xs[0].shape[0]
    k_splits = tuple(int(x.shape[1]) for x in xs)
    k, n_all = (w.shape[1], w.shape[0]) if w_t else w.shape
    n = n_all - col0 if n is None else n
    assert sum(k_splits) == k and n % tn == 0 and col0 % tn == 0
    cb0 = col0 // tn
    tm = _row_tile(lp)
    in_specs = [pl.BlockSpec((tm, kk), lambda j, i: (i, 0)) for kk in k_splits]
    if w_t:
        in_specs.append(pl.BlockSpec((tn, k), lambda j, i: (cb0 + j, 0)))
    else:
        in_specs.append(pl.BlockSpec((k, tn), lambda j, i: (0, cb0 + j)))
    args = list(xs) + [w]
    if res is not None:
        in_specs.append(pl.BlockSpec((tm, tn), lambda j, i: (i, j)))
        args.append(res)
    return pl.pallas_call(
        functools.partial(_matmul_kernel, k_splits=k_splits, has_res=res is not None, w_t=w_t),
        grid=(n // tn, lp // tm),
        in_specs=in_specs,
        out_specs=pl.BlockSpec((tm, tn), lambda j, i: (i, j)),
        out_shape=jax.ShapeDtypeStruct((lp, n), out_dtype),
        scratch_shapes=[pltpu.VMEM((k, tn), BF16)],
        compiler_params=_cparams(("arbitrary", "arbitrary")),
        name="matmul",
    )(*args)


def _ssd_conv_kernel(cur_ref, prev_ref, next_ref, w_ref, b_ref, o_ref, ext_ref):
    c = pl.program_id(0)
    nch = pl.num_programs(0)
    halo = 8
    ext_ref[0:halo, :] = jnp.where(c > 0, prev_ref[...], 0.0)
    ext_ref[halo:halo + CHUNK, :] = cur_ref[...]
    ext_ref[halo + CHUNK:2 * halo + CHUNK, :] = jnp.where(c < nch - 1, next_ref[...], 0.0)
    w = w_ref[...]
    acc = jnp.zeros(cur_ref.shape, F32) + b_ref[...]
    centre = (SSD_CONV - 1) // 2
    for j in range(SSD_CONV):
        acc = acc + w[j:j + 1, :] * ext_ref[pl.ds(halo - centre + j, CHUNK), :]
    y = acc * _sigmoid(acc)
    row = c * CHUNK + lax.broadcasted_iota(I32, y.shape, 0)
    o_ref[...] = jnp.where(row >= PAD, y, 0.0)


def ssd_conv(xbc, conv_w, conv_b):
    lp, ch = xbc.shape
    nch = lp // CHUNK
    sub = CHUNK // 8
    return pl.pallas_call(
        _ssd_conv_kernel,
        grid=(nch,),
        in_specs=[
            pl.BlockSpec((CHUNK, ch), lambda c: (c, 0)),
            pl.BlockSpec((8, ch), lambda c: (jnp.maximum(c * sub - 1, 0), 0)),
            pl.BlockSpec((8, ch), lambda c: (jnp.minimum((c + 1) * sub, nch * sub - 1), 0)),
            pl.BlockSpec((SSD_CONV, ch), lambda c: (0, 0)),
            pl.BlockSpec((1, ch), lambda c: (0, 0)),
        ],
        out_specs=pl.BlockSpec((CHUNK, ch), lambda c: (c, 0)),
        out_shape=jax.ShapeDtypeStruct((lp, ch), F32),
        scratch_shapes=[pltpu.VMEM((CHUNK + 16, ch), F32)],
        compiler_params=_cparams(("arbitrary",)),
        name="ssd_conv",
    )(xbc, xbc, xbc, conv_w, conv_b.reshape(1, ch))


def _ssd_direction(d, ce, x_ref, b_ref, c_ref, dt_ref, dtb_ref, acoef_ref, tri_ref, y_ref, state_ref):
    tri = tri_ref[d]
    trib = tri > 0.5
    row = ce * CHUNK + lax.broadcasted_iota(I32, (CHUNK, CHUNK), 0)
    dt = jnp.where(row >= PAD, _softplus(dt_ref[...] + dtb_ref[d]), 0.0)
    a = dt * acoef_ref[d]
    a_cum = jnp.dot(tri, a, precision=HIGHEST, preferred_element_type=F32)
    a_cum_t = a_cum.T
    a_tot = jnp.sum(a, axis=0, keepdims=True)
    e_cum = jnp.exp(a_cum)
    d_out = jnp.exp(a_tot - a_cum)
    e_tot = jnp.exp(a_tot)
    lane = lax.broadcasted_iota(I32, (CHUNK, CHUNK), 1)
    first = lane < SSD_HEAD_DIM
    first_row = first[0:1, :]

    def pair_cols(m, h0):
        return jnp.where(first, m[:, h0:h0 + 1], m[:, h0 + 1:h0 + 2])

    hpg = SSD_HEADS // SSD_GROUPS
    for g in range(SSD_GROUPS):
        bg = b_ref[:, g * SSD_STATE:(g + 1) * SSD_STATE]
        cg = c_ref[:, g * SSD_STATE:(g + 1) * SSD_STATE].astype(BF16)
        cb = _bdot(cg, bg.astype(BF16), NT)
        bt = bg.T.astype(BF16)
        for pp in range(hpg // 2):
            h0 = g * hpg + 2 * pp
            p_idx = d * (SSD_HEADS // 2) + h0 // 2
            lo = h0 * SSD_HEAD_DIM
            x_p = x_ref[:, lo:lo + 2 * SSD_HEAD_DIM]
            xdt = x_p * pair_cols(dt, h0)
            xw = (xdt * pair_cols(d_out, h0)).astype(BF16)
            xdt_b = xdt.astype(BF16)
            yd = []
            for hh in (h0, h0 + 1):
                diff = a_cum[:, hh:hh + 1] - a_cum_t[hh:hh + 1, :]
                dec = jnp.exp(jnp.where(trib, diff, NEG))
                yd.append(_bdot((cb * dec).astype(BF16), xdt_b))
            y_diag = jnp.where(first, yd[0], yd[1])
            st_prev = state_ref[p_idx]
            y_off = _bdot(cg, st_prev.astype(BF16)) * pair_cols(e_cum, h0)
            y_ref[:, lo:lo + 2 * SSD_HEAD_DIM] = y_diag + y_off
            st_new = _bdot(bt, xw)
            cdec = jnp.where(first_row, e_tot[:, h0:h0 + 1], e_tot[:, h0 + 1:h0 + 2])
            state_ref[p_idx] = st_prev * cdec + st_new


def _ssd_scan_kernel(xf_ref, bf_ref, cf_ref, dtf_ref, xb_ref, bb_ref, cb_ref, dtbk_ref, dtb_ref, acoef_ref, tri_ref,
                     yf_ref, yb_ref, state_ref):
    c = pl.program_id(0)
    nch = pl.num_programs(0)

    @pl.when(c == 0)
    def _():
        state_ref[...] = jnp.zeros(state_ref.shape, F32)

    _ssd_direction(0, c, xf_ref, bf_ref, cf_ref, dtf_ref, dtb_ref, acoef_ref, tri_ref, yf_ref, state_ref)
    _ssd_direction(1, nch - 1 - c, xb_ref, bb_ref, cb_ref, dtbk_ref, dtb_ref, acoef_ref, tri_ref, yb_ref, state_ref)


def ssd_scan(xbc_act, dt_raw, dt_bias_rows, acoef_rows, tri2):
    lp = xbc_act.shape[0]
    nch = lp // CHUNK
    gs = SSD_GROUPS * SSD_STATE

    def specs(chunk, d):
        return [
            pl.BlockSpec((CHUNK, SSD_INNER), lambda c: (chunk(c), 0)),
            pl.BlockSpec((CHUNK, gs), lambda c: (chunk(c), SSD_INNER // gs)),
            pl.BlockSpec((CHUNK, gs), lambda c: (chunk(c), SSD_INNER // gs + 1)),
            pl.BlockSpec((CHUNK, CHUNK), lambda c: (chunk(c), d)),
        ]

    fwd = lambda c: c
    bwd = lambda c: nch - 1 - c
    return pl.pallas_call(
        _ssd_scan_kernel,
        grid=(nch,),
        in_specs=specs(fwd, 0) + specs(bwd, 1) + [
            pl.BlockSpec((2, 1, CHUNK), lambda c: (0, 0, 0)),
            pl.BlockSpec((2, 1, CHUNK), lambda c: (0, 0, 0)),
            pl.BlockSpec((2, CHUNK, CHUNK), lambda c: (0, 0, 0)),
        ],
        out_specs=[pl.BlockSpec((CHUNK, SSD_INNER), lambda c: (c, 0)),
                   pl.BlockSpec((CHUNK, SSD_INNER), lambda c: (nch - 1 - c, 0))],
        out_shape=[jax.ShapeDtypeStruct((lp, SSD_INNER), F32), jax.ShapeDtypeStruct((lp, SSD_INNER), F32)],
        scratch_shapes=[pltpu.VMEM((SSD_HEADS, SSD_STATE, 2 * SSD_HEAD_DIM), F32)],
        compiler_params=_cparams(("arbitrary",)),
        name="ssd_scan",
    )(xbc_act, xbc_act, xbc_act, dt_raw, xbc_act, xbc_act, xbc_act, dt_raw, dt_bias_rows, acoef_rows, tri2)


def _ssd_final_kernel(yf_ref, yb_ref, x_ref, z_ref, dskip_ref, nw_ref, o_ref):
    y = yf_ref[...] + yb_ref[...] + dskip_ref[...] * x_ref[...]
    z = z_ref[...]
    y = y * (z * _sigmoid(z))
    gw = SSD_INNER // SSD_GROUPS
    for g in range(SSD_GROUPS):
        yg = y[:, g * gw:(g + 1) * gw]
        yg = yg * lax.rsqrt(jnp.mean(yg * yg, axis=-1, keepdims=True) + NORM_EPS)
        o_ref[:, g * gw:(g + 1) * gw] = (yg * nw_ref[:, g * gw:(g + 1) * gw]).astype(o_ref.dtype)


def ssd_final(yf, yb, xbc_act, z, d_skip_row, norm_w):
    lp = xbc_act.shape[0]
    blk = pl.BlockSpec((CHUNK, SSD_INNER), lambda c: (c, 0))
    row = pl.BlockSpec((1, SSD_INNER), lambda c: (0, 0))
    return pl.pallas_call(
        _ssd_final_kernel,
        grid=(lp // CHUNK,),
        in_specs=[blk, blk, blk, blk, row, row],
        out_specs=blk,
        out_shape=jax.ShapeDtypeStruct((lp, SSD_INNER), BF16),
        compiler_params=_cparams(("arbitrary",)),
        name="ssd_final",
    )(yf, yb, xbc_act, z, d_skip_row, norm_w.reshape(1, SSD_INNER))


def _attn_kernel(slope_ref, sink_ref, q_ref, km_ref, kp_ref, ks_ref, kn_ref, vm_ref, vp_ref, vs_ref, vn_ref, o_ref):
    b = pl.program_id(0)
    nb = pl.num_programs(0)
    scale = ATT_HEAD_DIM ** -0.5

    qpos = b * CHUNK - PAD + lax.broadcasted_iota(I32, (CHUNK, 4 * CHUNK), 0)
    col = lax.broadcasted_iota(I32, (CHUNK, 4 * CHUNK), 1)
    slot = col // CHUNK
    j = col - slot * CHUNK
    kblock = jnp.where(slot == 0, 0, b - 2 + slot)
    kpos = kblock * CHUNK - PAD + j
    dist = jnp.abs(qpos - kpos)
    meta_ok = (slot == 0) & (kpos >= 0)
    real_ok = (slot > 0) & (kblock >= 1) & (kblock < nb) & (kpos >= N_META) & (dist <= ATT_WINDOW)
    valid = meta_ok | real_ok
    alibi = jnp.where(real_ok & (qpos >= N_META), dist.astype(F32), 0.0)
    q_real = qpos[:, 0:ATT_HEAD_DIM] >= 0

    for hk in range(ATT_KV_HEADS):
        ksl = slice(hk * ATT_HEAD_DIM, (hk + 1) * ATT_HEAD_DIM)
        kcat = jnp.concatenate([km_ref[:, ksl], kp_ref[:, ksl], ks_ref[:, ksl], kn_ref[:, ksl]],
                               axis=0).astype(BF16)
        vcat = jnp.concatenate([vm_ref[:, ksl], vp_ref[:, ksl], vs_ref[:, ksl], vn_ref[:, ksl]],
                               axis=0).astype(BF16)
        for g in range(ATT_GRP):
            h = hk * ATT_GRP + g
            hsl = slice(h * ATT_HEAD_DIM, (h + 1) * ATT_HEAD_DIM)
            qg = q_ref[:, hsl].astype(BF16)
            s = _bdot(qg, kcat, NT) * scale
            s = jnp.where(valid, s - slope_ref[h] * alibi, NEG)
            sink = sink_ref[h]
            m = jnp.maximum(jnp.max(s, axis=-1, keepdims=True), sink)
            p = jnp.exp(s - m)
            denom = jnp.sum(p, axis=-1, keepdims=True) + jnp.exp(sink - m)
            o = _bdot(p.astype(BF16), vcat) / denom
            o_ref[:, hsl] = jnp.where(q_real, o, 0.0).astype(o_ref.dtype)


def window_attention(qkv, slopes, sink):
    lp = qkv.shape[0]
    nb = lp // CHUNK
    qw = ATT_Q_HEADS * ATT_HEAD_DIM
    kvw = ATT_KV_HEADS * ATT_HEAD_DIM
    kcol = qw // kvw
    vcol = kcol + 1

    def kv_spec(col, which):
        def imap(b, *_):
            if which == 0:
                blk = 0
            else:
                blk = jnp.clip(b - 2 + which, 0, nb - 1)
            return (blk, col)
        return pl.BlockSpec((CHUNK, kvw), imap)

    return pl.pallas_call(
        _attn_kernel,
        grid_spec=pltpu.PrefetchScalarGridSpec(
            num_scalar_prefetch=2,
            grid=(nb,),
            in_specs=[pl.BlockSpec((CHUNK, qw), lambda b, *_: (b, 0))]
            + [kv_spec(kcol, w) for w in range(4)] + [kv_spec(vcol, w) for w in range(4)],
            out_specs=pl.BlockSpec((CHUNK, qw), lambda b, *_: (b, 0)),
        ),
        out_shape=jax.ShapeDtypeStruct((lp, qw), BF16),
        compiler_params=_cparams(("arbitrary",)),
        name="window_attention",
    )(slopes, sink, *([qkv] * 9))


def _mlstm_direction(d, ce, q_ref, k_ref, v_ref, g_ref, gb_ref, tri_ref, h_ref, c_ref, n_ref, m_ref):
    tri = tri_ref[d]
    trib = tri > 0.5
    row = ce * CHUNK + lax.broadcasted_iota(I32, (CHUNK, CHUNK), 0)
    pre = g_ref[...] + gb_ref[d]
    real = row >= PAD
    real_col = real[:, 0:1]
    li_all = jnp.where(real, pre, 0.0)
    lf_all = jnp.where(real, -_softplus(-pre), 0.0)
    bl_all = jnp.dot(tri, lf_all, precision=HIGHEST, preferred_element_type=F32)
    li_t = li_all.T
    bl_t = bl_all.T
    g_all = jnp.sum(lf_all, axis=0, keepdims=True)

    for hd in range(ML_HEADS):
        fl = hd + ML_F_LANE
        li_col = li_all[:, hd:hd + 1]
        li_row = li_t[hd:hd + 1, :]
        bl_col = bl_all[:, fl:fl + 1]
        bl_row = bl_t[fl:fl + 1, :]
        g_tot = g_all[:, fl:fl + 1]
        st = d * ML_HEADS + hd
        ksl = slice(hd * ML_QK_DIM, (hd + 1) * ML_QK_DIM)
        vsl = slice(hd * ML_V_DIM, (hd + 1) * ML_V_DIM)

        qf = jnp.where(real_col, q_ref[:, ksl], 0.0)
        q = qf.astype(BF16)
        kf = jnp.where(real_col, k_ref[:, ksl], 0.0) * (ML_QK_DIM ** -0.5)
        kt = kf.T.astype(BF16)
        v = jnp.where(real_col, v_ref[:, vsl], 0.0)
        vb = v.astype(BF16)

        m_prev = m_ref[st, 0:1, 0:1]
        n_prev = n_ref[st]
        c_prev = c_ref[st]

        dmat = jnp.where(trib, bl_col - bl_row + li_row, NEG)
        m_inter = bl_col + m_prev
        m_t = jnp.maximum(jnp.max(dmat, axis=1, keepdims=True), m_inter)
        qk = jnp.dot(q, kt, preferred_element_type=F32)
        p = jnp.exp(dmat - m_t) * qk
        w_inter = jnp.exp(m_inter - m_t)
        num = (jnp.dot(p.astype(BF16), vb, preferred_element_type=F32)
               + w_inter * jnp.dot(q, c_prev.astype(BF16), preferred_element_type=F32))
        qn = jnp.sum(qf * n_prev, axis=1, keepdims=True)
        den = jnp.sum(p, axis=1, keepdims=True) + w_inter * qn
        h_ref[:, vsl] = num / jnp.maximum(jnp.abs(den), jnp.exp(-m_t))

        a_col = g_tot - bl_col + li_col
        m_loc = jnp.max(a_col, axis=0, keepdims=True)
        m_new = jnp.maximum(g_tot + m_prev, m_loc)
        s_old = jnp.exp(g_tot + m_prev - m_new)
        s_new = jnp.exp(m_loc - m_new)
        wa = jnp.exp(a_col - m_loc) * s_new
        c_loc = jnp.dot(kt, (wa * v).astype(BF16), preferred_element_type=F32)
        n_loc = jnp.sum(wa * kf, axis=0, keepdims=True)
        c_ref[st] = s_old * c_prev + c_loc
        n_ref[st] = s_old * n_prev + n_loc
        m_ref[st] = jnp.broadcast_to(m_new, m_ref.shape[1:])


def _mlstm_scan_kernel(qf_ref, kf_ref, vf_ref, gf_ref, qb_ref, kb_ref, vb_ref, gbk_ref, gb_ref, tri_ref,
                       hf_ref, hb_ref, c_ref, n_ref, m_ref):
    c = pl.program_id(0)
    nch = pl.num_programs(0)

    @pl.when(c == 0)
    def _():
        c_ref[...] = jnp.zeros(c_ref.shape, F32)
        n_ref[...] = jnp.zeros(n_ref.shape, F32)
        m_ref[...] = jnp.zeros(m_ref.shape, F32)

    _mlstm_direction(0, c, qf_ref, kf_ref, vf_ref, gf_ref, gb_ref, tri_ref, hf_ref, c_ref, n_ref, m_ref)
    _mlstm_direction(1, nch - 1 - c, qb_ref, kb_ref, vb_ref, gbk_ref, gb_ref, tri_ref, hb_ref, c_ref, n_ref, m_ref)


def mlstm_scan(qkvo, gates, gate_bias_rows, tri2):
    lp = qkvo.shape[0]
    nch = lp // CHUNK
    qk_w = ML_HEADS * ML_QK_DIM

    def specs(chunk, d):
        return [
            pl.BlockSpec((CHUNK, qk_w), lambda c: (chunk(c), 0)),
            pl.BlockSpec((CHUNK, qk_w), lambda c: (chunk(c), 1)),
            pl.BlockSpec((CHUNK, ML_WIDTH), lambda c: (chunk(c), 2 * qk_w // ML_WIDTH)),
            pl.BlockSpec((CHUNK, CHUNK), lambda c: (chunk(c), d)),
        ]

    fwd = lambda c: c
    bwd = lambda c: nch - 1 - c
    n_state = 2 * ML_HEADS
    return pl.pallas_call(
        _mlstm_scan_kernel,
        grid=(nch,),
        in_specs=specs(fwd, 0) + specs(bwd, 1) + [
            pl.BlockSpec((2, 1, CHUNK), lambda c: (0, 0, 0)),
            pl.BlockSpec((2, CHUNK, CHUNK), lambda c: (0, 0, 0)),
        ],
        out_specs=[pl.BlockSpec((CHUNK, ML_WIDTH), lambda c: (c, 0)),
                   pl.BlockSpec((CHUNK, ML_WIDTH), lambda c: (nch - 1 - c, 0))],
        out_shape=[jax.ShapeDtypeStruct((lp, ML_WIDTH), F32), jax.ShapeDtypeStruct((lp, ML_WIDTH), F32)],
        scratch_shapes=[pltpu.VMEM((n_state, ML_QK_DIM, ML_V_DIM), F32), pltpu.VMEM((n_state, 1, ML_QK_DIM), F32),
                        pltpu.VMEM((n_state, 8, CHUNK), F32)],
        compiler_params=_cparams(("arbitrary",)),
        name="mlstm_scan",
    )(qkvo, qkvo, qkvo, gates, qkvo, qkvo, qkvo, gates, gate_bias_rows, tri2)


def _mlstm_final_kernel(hf_ref, hb_ref, o_ref_in, nw_ref, y_ref):
    h = hf_ref[...] + hb_ref[...]
    o = o_ref_in[...]
    for hd in range(ML_HEADS):
        sl = slice(hd * ML_V_DIM, (hd + 1) * ML_V_DIM)
        hh = h[:, sl]
        hh = hh * lax.rsqrt(jnp.mean(hh * hh, axis=-1, keepdims=True) + NORM_EPS) * nw_ref[:, sl]
        y_ref[:, sl] = (_sigmoid(o[:, sl]) * hh).astype(y_ref.dtype)


def mlstm_final(hf, hb, qkvo, head_norm):
    lp = qkvo.shape[0]
    ocol = (2 * ML_HEADS * ML_QK_DIM + ML_WIDTH) // ML_WIDTH
    return pl.pallas_call(
        _mlstm_final_kernel,
        grid=(lp // CHUNK,),
        in_specs=[
            pl.BlockSpec((CHUNK, ML_WIDTH), lambda c: (c, 0)),
            pl.BlockSpec((CHUNK, ML_WIDTH), lambda c: (c, 0)),
            pl.BlockSpec((CHUNK, ML_WIDTH), lambda c: (c, ocol)),
            pl.BlockSpec((1, ML_WIDTH), lambda c: (0, 0)),
        ],
        out_specs=pl.BlockSpec((CHUNK, ML_WIDTH), lambda c: (c, 0)),
        out_shape=jax.ShapeDtypeStruct((lp, ML_WIDTH), BF16),
        compiler_params=_cparams(("arbitrary",)),
        name="mlstm_final",
    )(hf, hb, qkvo, head_norm.reshape(1, ML_WIDTH))


def _dft_factors(n):
    best = None
    for n1 in range(8, n + 1, 8):
        if n % n1 == 0:
            n2 = n // n1
            if best is None or n1 + n2 < best[0] + best[1]:
                best = (n1, n2)
    assert best is not None
    return best


def _dot3(a, b):
    return _bdot(a[0], b[0]) + _bdot(a[1], b[0]) + _bdot(a[0], b[1])


def _fnet_stage1_kernel(x_ref, c2h_ref, c2l_ref, s2h_ref, s2l_ref, twc_ref, tws_ref, tr_ref, ti_ref, *, n2):
    x = x_ref[...]
    rows = lax.broadcasted_iota(I32, x.shape, 0)
    xs = _split_bf16(jnp.where(rows < n2, x, 0.0))
    cx = _dot3((c2h_ref[...], c2l_ref[...]), xs)
    sx = _dot3((s2h_ref[...], s2l_ref[...]), xs)
    twc = twc_ref[0]
    tws = tws_ref[0]
    tr_ref[...] = cx * twc - sx * tws
    ti_ref[...] = -(cx * tws + sx * twc)


def _fnet_stage2_kernel(tr_ref, ti_ref, c1h_ref, c1l_ref, s1h_ref, s1l_ref, cch_ref, ccl_ref, sch_ref, scl_ref, o_ref,
                        *, scale):
    tr = _split_bf16(tr_ref[0])
    ti = _split_bf16(ti_ref[0])
    c1 = (c1h_ref[...], c1l_ref[...])
    s1 = (s1h_ref[...], s1l_ref[...])
    yr = _dot3(c1, tr) + _dot3(s1, ti)
    yi = _dot3(c1, ti) - _dot3(s1, tr)
    cc = (cch_ref[...], ccl_ref[...])
    sc = (sch_ref[...], scl_ref[...])
    for g in range(FN_GROUPS):
        sl = slice(g * FN_GROUP_DIM, (g + 1) * FN_GROUP_DIM)
        out = _dot3(_split_bf16(yr[:, sl]), cc) + _dot3(_split_bf16(yi[:, sl]), sc)
        o_ref[:, sl] = out * scale


def fourier_mixer(u_fn):
    n, ch = u_fn.shape
    n1, n2 = _dft_factors(n)
    n2p = -(-n2 // 8) * 8
    two_pi = 2.0 * np.pi

    def cs(size, pad):
        idx = np.arange(size)
        ang = two_pi * ((idx[:, None] * idx[None, :]) % size) / size
        out = []
        for fn in (np.cos, np.sin):
            m = np.zeros((pad, pad), np.float32)
            m[:size, :size] = fn(ang)
            hi = jnp.asarray(m).astype(BF16)
            out += [hi, (jnp.asarray(m) - hi.astype(F32)).astype(BF16)]
        return out

    dft2 = cs(n2, n2p)
    dft1 = cs(n1, n1)
    dftc = cs(FN_GROUP_DIM, FN_GROUP_DIM)
    tw_ang = two_pi * ((np.arange(n1)[:, None] * np.arange(n2)[None, :]) % n) / n
    twc = np.zeros((n1, n2p, 1), np.float32)
    tws = np.zeros((n1, n2p, 1), np.float32)
    twc[:, :n2, 0] = np.cos(tw_ang)
    tws[:, :n2, 0] = np.sin(tw_ang)

    def full(shape):
        return pl.BlockSpec(shape, lambda i: (0,) * len(shape))

    x2 = u_fn.reshape(n2, n1 * ch)
    tr, ti = pl.pallas_call(
        functools.partial(_fnet_stage1_kernel, n2=n2),
        grid=(n1,),
        in_specs=[pl.BlockSpec((n2p, ch), lambda i: (0, i))] + [full((n2p, n2p))] * 4 + [
            pl.BlockSpec((1, n2p, 1), lambda i: (i, 0, 0)),
            pl.BlockSpec((1, n2p, 1), lambda i: (i, 0, 0)),
        ],
        out_specs=[pl.BlockSpec((n2p, ch), lambda i: (0, i)), pl.BlockSpec((n2p, ch), lambda i: (0, i))],
        out_shape=[jax.ShapeDtypeStruct((n2p, n1 * ch), F32), jax.ShapeDtypeStruct((n2p, n1 * ch), F32)],
        compiler_params=_cparams(("arbitrary",)),
        name="fnet_stage1",
    )(x2, *dft2, jnp.asarray(twc), jnp.asarray(tws))

    tr3 = tr.reshape(n2p, n1, ch)
    ti3 = ti.reshape(n2p, n1, ch)
    scale = 1.0 / math.sqrt(float(n) * FN_GROUP_DIM)
    out = pl.pallas_call(
        functools.partial(_fnet_stage2_kernel, scale=scale),
        grid=(n2,),
        in_specs=[
            pl.BlockSpec((1, n1, ch), lambda i: (i, 0, 0)),
            pl.BlockSpec((1, n1, ch), lambda i: (i, 0, 0)),
        ] + [full((n1, n1))] * 4 + [full((FN_GROUP_DIM, FN_GROUP_DIM))] * 4,
        out_specs=pl.BlockSpec((n1, ch), lambda i: (0, i)),
        out_shape=jax.ShapeDtypeStruct((n1, n2 * ch), F32),
        compiler_params=_cparams(("arbitrary",)),
        name="fnet_stage2",
    )(tr3, ti3, *dft1, *dftc)
    return out.reshape(n, ch)


def _cap_sizes(lp):
    n_tok = lp - PAD
    cap = EC_FACTOR * n_tok // N_EXPERTS
    capp = -(-cap // 16) * 16
    return cap, capp


def _topk_kernel(lg_ref, su_ref, pos_ref, aff_ref, start_ref, idx_ref, rank_scr, pos_scr, *, cap, capp, jb):
    ne, lp = lg_ref.shape
    nch = lp // CHUNK
    lg = lg_ref[...]
    mx = jnp.max(lg, axis=0, keepdims=True)
    ex = jnp.exp(lg - mx)
    aff = ex / jnp.sum(ex, axis=0, keepdims=True)
    lane = lax.broadcasted_iota(I32, (ne, lp), 1)
    real = lane >= PAD
    bits = jnp.where(real, pltpu.bitcast(aff, I32), -1)
    capf = float(cap)

    def bisect(i, thr):
        cand = thr | jnp.left_shift(jnp.int32(1), 30 - i)
        cnt = jnp.sum(jnp.where(bits >= cand, 1.0, 0.0), axis=1, keepdims=True)
        return jnp.where(cnt >= capf, cand, thr)

    thr = lax.fori_loop(0, 31, bisect, jnp.zeros((ne, 1), I32))
    gt = bits > thr
    eq = bits == thr
    need = capf - jnp.sum(jnp.where(gt, 1.0, 0.0), axis=1, keepdims=True)

    su = su_ref[...]

    def excl_cumsum(mask_f32, out_scr, want_starts):
        carry = jnp.zeros((ne, 1), F32)
        starts = jnp.zeros((ne, CHUNK), F32)
        lane_c = lax.broadcasted_iota(I32, (ne, CHUNK), 1)
        for c in range(nch):
            tile = mask_f32[:, c * CHUNK:(c + 1) * CHUNK]
            within = jnp.dot(tile.astype(BF16), su, preferred_element_type=F32)
            out_scr[:, c * CHUNK:(c + 1) * CHUNK] = within + carry
            if want_starts:
                starts = jnp.where(lane_c == c, carry, starts)
            carry = carry + jnp.sum(tile, axis=1, keepdims=True)
        if want_starts:
            starts = jnp.where(lane_c == nch, carry, starts)
        return starts

    excl_cumsum(jnp.where(eq, 1.0, 0.0), rank_scr, False)
    sel = gt | (eq & (rank_scr[...] < need))
    starts = excl_cumsum(jnp.where(sel, 1.0, 0.0), pos_scr, True)
    posf = jnp.where(sel, pos_scr[...], -1.0)
    pos_scr[...] = posf
    pos_ref[...] = posf.astype(I32)
    aff_ref[...] = jnp.where(sel, aff, 0.0)
    start_ref[...] = starts.astype(I32)

    tok = lax.broadcasted_iota(I32, (jb, lp), 1).astype(F32)
    lane_e = lax.broadcasted_iota(I32, (jb, ne), 1)
    for blk in range(capp // jb):
        jcol = (blk * jb + lax.broadcasted_iota(I32, (jb, 1), 0)).astype(F32)

        def per_expert(e, acc):
            prow = pos_scr[pl.ds(e, 1), :]
            hit = jnp.sum(jnp.where(prow == jcol, tok, 0.0), axis=1, keepdims=True)
            return jnp.where(lane_e == e, hit, acc)

        acc = lax.fori_loop(0, ne, per_expert, jnp.zeros((jb, ne), F32))
        idx_ref[blk * jb:(blk + 1) * jb, :] = acc.astype(I32)


def topk_route(logits_t):
    ne, lp = logits_t.shape
    cap, capp = _cap_sizes(lp)
    jb = 80 if capp % 80 == 0 else 16
    su = jnp.asarray(np.triu(np.ones((CHUNK, CHUNK), np.float32), 1), BF16)
    return pl.pallas_call(
        functools.partial(_topk_kernel, cap=cap, capp=capp, jb=jb),
        out_shape=[jax.ShapeDtypeStruct((ne, lp), I32), jax.ShapeDtypeStruct((ne, lp), F32),
                   jax.ShapeDtypeStruct((ne, CHUNK), I32), jax.ShapeDtypeStruct((capp, ne), I32)],
        scratch_shapes=[pltpu.VMEM((ne, lp), F32), pltpu.VMEM((ne, lp), F32)],
        compiler_params=pltpu.CompilerParams(vmem_limit_bytes=VMEM_LIMIT),
        name="topk_route",
    )(logits_t, su)


GATHER_UNROLL = 8


def _gather_kernel(idx_ref, u_hbm, o_ref, stage_ref, sem):
    e = pl.program_id(0)
    ne = pl.num_programs(0)
    capp = stage_ref.shape[1]

    def issue(eq, slot):
        def body(jo, carry):
            for ji in range(GATHER_UNROLL):
                j = jo * GATHER_UNROLL + ji
                t = idx_ref[eq, j]
                pltpu.make_async_copy(u_hbm.at[pl.ds(t, 1), :], stage_ref.at[slot, pl.ds(j, 1), :],
                                      sem.at[slot]).start()
            return carry

        lax.fori_loop(0, capp // GATHER_UNROLL, body, 0)

    @pl.when(e == 0)
    def _():
        issue(e, 0)

    @pl.when(e + 1 < ne)
    def _():
        issue(e + 1, (e + 1) % 2)

    slot = e % 2
    pltpu.make_async_copy(u_hbm.at[pl.ds(0, capp), :], stage_ref.at[slot], sem.at[slot]).wait()
    o_ref[0] = stage_ref[slot].astype(o_ref.dtype)


def gather_tokens(idx, u):
    ne, capp = idx.shape
    lp, d = u.shape
    assert capp % GATHER_UNROLL == 0
    return pl.pallas_call(
        _gather_kernel,
        grid_spec=pltpu.PrefetchScalarGridSpec(
            num_scalar_prefetch=1,
            grid=(ne,),
            in_specs=[pl.BlockSpec(memory_space=pl.ANY)],
            out_specs=pl.BlockSpec((1, capp, d), lambda e, *_: (e, 0, 0)),
            scratch_shapes=[pltpu.VMEM((2, capp, d), F32), pltpu.SemaphoreType.DMA((2,))],
        ),
        out_shape=jax.ShapeDtypeStruct((ne, capp, d), BF16),
        compiler_params=_cparams(("arbitrary",)),
        name="moe_gather",
    )(idx, u)


def _ffn_kernel(xs_ref, wg_ref, wu_ref, wd_ref, o_ref, hdn_ref, *, n_ff):
    s = pl.program_id(1)

    @pl.when(s < n_ff)
    def _():
        x = xs_ref[0]
        g = jnp.dot(x, wg_ref[0, 0].astype(BF16), preferred_element_type=F32)
        u = jnp.dot(x, wu_ref[0, 0].astype(BF16), preferred_element_type=F32)
        hdn_ref[s] = (g * _sigmoid(g) * u).astype(BF16)

    @pl.when(s >= n_ff)
    def _():
        wd = wd_ref[0, 0].astype(BF16)
        acc = None
        for j in range(n_ff):
            part = jnp.dot(hdn_ref[j], wd[j * FF_TILE:(j + 1) * FF_TILE, :], preferred_element_type=F32)
            acc = part if acc is None else acc + part
        o_ref[0] = acc


def expert_ffn(xs, w_gate, w_up, w_down, layer):
    ne, capp, d = xs.shape
    n_ff = EXPERT_FF // FF_TILE
    n_out = d // OUT_TILE
    return pl.pallas_call(
        functools.partial(_ffn_kernel, n_ff=n_ff),
        grid=(ne, n_ff + n_out),
        in_specs=[
            pl.BlockSpec((1, capp, d), lambda e, s: (e, 0, 0)),
            pl.BlockSpec((1, 1, d, FF_TILE), lambda e, s: (layer, e, 0, jnp.minimum(s, n_ff - 1))),
            pl.BlockSpec((1, 1, d, FF_TILE), lambda e, s: (layer, e, 0, jnp.minimum(s, n_ff - 1))),
            pl.BlockSpec((1, 1, EXPERT_FF, OUT_TILE), lambda e, s: (layer, e, 0, jnp.maximum(s - n_ff, 0))),
        ],
        out_specs=pl.BlockSpec((1, capp, OUT_TILE), lambda e, s: (e, 0, jnp.maximum(s - n_ff, 0))),
        out_shape=jax.ShapeDtypeStruct((ne, capp, d), F32),
        scratch_shapes=[pltpu.VMEM((n_ff, capp, FF_TILE), BF16)],
        compiler_params=_cparams(("arbitrary", "arbitrary")),
        name="expert_ffn",
    )(xs, w_gate, w_up, w_down)


def _combine_kernel(start_ref, h_ref, pos_ref, aff_ref, nw_ref, out_hbm, *refs, capp, rows, n_half, emit_h):
    if emit_h:
        o_ref, y_ref, acc_ref, g_ref, sem = refs
    else:
        y_ref, acc_ref, g_ref, sem = refs
    b = pl.program_id(0)
    hh = pl.program_id(1)
    nb = pl.num_programs(0)
    ne = pos_ref.shape[0]
    eh = ne // n_half

    def row_copy(src, dst, slot, n_rows=1):
        return pltpu.make_async_copy(out_hbm.at[pl.ds(src, n_rows), :], g_ref.at[slot, pl.ds(dst, n_rows), :],
                                     sem.at[slot])

    def block_rows(bq, hq):
        s0 = [start_ref[hq * eh + i, bq] for i in range(eh)]
        n = [start_ref[hq * eh + i, bq + 1] - s0[i] for i in range(eh)]
        return s0, n

    def issue(bq, hq, slot):
        s0, n = block_rows(bq, hq)
        off = jnp.int32(0)
        for i in range(eh):
            base = (hq * eh + i) * capp + s0[i]

            def body(j, carry, base=base, off=off):
                row_copy(base + j, off + j, slot).start()
                return carry

            lax.fori_loop(0, n[i], body, 0)
            off = off + n[i]

    @pl.when((b == 0) & (hh == 0))
    def _():
        g_ref[...] = jnp.zeros(g_ref.shape, F32)
        issue(b, hh, 0)

    @pl.when(hh + 1 < n_half)
    def _():
        issue(b, hh + 1, 1 - hh % 2)

    if n_half % 2 == 0:
        @pl.when((hh + 1 == n_half) & (b + 1 < nb))
        def _():
            issue(b + 1, 0, 0)

    slot = hh % 2
    s0, n = block_rows(b, hh)
    offs = []
    off = jnp.int32(0)
    for i in range(eh):
        offs.append(off)
        off = off + n[i]
    total = off

    stage_rows = g_ref.shape[1]
    bit = 1
    while bit <= stage_rows:
        @pl.when((total & bit) != 0)
        def _(bit=bit):
            row_copy(0, 0, slot, n_rows=bit).wait()
        bit *= 2

    @pl.when(hh == 0)
    def _():
        acc_ref[...] = h_ref[...]

    r_iota = lax.broadcasted_iota(I32, (rows, CHUNK), 0)
    d = acc_ref.shape[1]

    def chunk(ci, carry):
        r0 = pl.multiple_of(ci * rows, rows)
        rr = r_iota + r0
        onehot = jnp.zeros((rows, CHUNK), F32)
        gate = jnp.zeros((rows, CHUNK), F32)
        for i in range(eh):
            prow = pos_ref[pl.ds(hh * eh + i, 1), :]
            arow = aff_ref[pl.ds(hh * eh + i, 1), :]
            tgt = jnp.where(prow >= 0, prow - s0[i] + offs[i], -1)
            hit = rr == tgt
            onehot = jnp.where(hit, 1.0, onehot)
            gate = jnp.where(hit, arow, gate)
        gcol = jnp.sum(gate, axis=1, keepdims=True)
        oh = onehot.T.astype(BF16)
        for c0 in range(0, d, COMBINE_SLAB):
            gs = (g_ref[slot, pl.ds(r0, rows), c0:c0 + COMBINE_SLAB] * gcol).astype(BF16)
            acc_ref[:, c0:c0 + COMBINE_SLAB] += jnp.dot(oh, gs, preferred_element_type=F32)
        return carry

    lax.fori_loop(0, (total + rows - 1) // rows, chunk, 0)

    @pl.when(hh == n_half - 1)
    def _():
        x = acc_ref[...]
        if emit_h:
            o_ref[...] = x
        y = x * lax.rsqrt(jnp.mean(x * x, axis=-1, keepdims=True) + NORM_EPS) * nw_ref[...]
        y_ref[...] = y.astype(y_ref.dtype)


def moe_combine(h, pos, aff, starts, ffn_out, next_norm_w, last):
    lp, d = h.shape
    ne, capp, _ = ffn_out.shape
    rows = 2 * CHUNK
    n_half = 2
    stage_rows = (ne // n_half) * CHUNK
    blk = pl.BlockSpec((CHUNK, d), lambda b, hh, *_: (b, 0))
    if last:
        out_specs = pl.BlockSpec((CHUNK, d), lambda b, hh, *_: (jnp.maximum(b - 1, 0), 0))
        out_shape = jax.ShapeDtypeStruct((lp - CHUNK, d), F32)
    else:
        out_specs = [blk, blk]
        out_shape = [jax.ShapeDtypeStruct((lp, d), F32), jax.ShapeDtypeStruct((lp, d), BF16)]
    return pl.pallas_call(
        functools.partial(_combine_kernel, capp=capp, rows=rows, n_half=n_half, emit_h=not last),
        grid_spec=pltpu.PrefetchScalarGridSpec(
            num_scalar_prefetch=1,
            grid=(lp // CHUNK, n_half),
            in_specs=[
                blk,
                pl.BlockSpec((ne, CHUNK), lambda b, hh, *_: (0, b)),
                pl.BlockSpec((ne, CHUNK), lambda b, hh, *_: (0, b)),
                pl.BlockSpec((1, d), lambda b, hh, *_: (0, 0)),
                pl.BlockSpec(memory_space=pl.ANY),
            ],
            out_specs=out_specs,
            scratch_shapes=[pltpu.VMEM((CHUNK, d), F32), pltpu.VMEM((2, stage_rows, d), F32),
                            pltpu.SemaphoreType.DMA((2,))],
        ),
        out_shape=out_shape,
        compiler_params=_cparams(("arbitrary", "arbitrary")),
        name="moe_combine",
    )(starts, h, pos, aff, next_norm_w.reshape(1, d), ffn_out.reshape(ne * capp, d))


def moe_layer(h, norm_w, w_router, w_gate, w_up, w_down, layer, next_norm_w, last):
    u, logits_t = rmsnorm_router(h, norm_w, w_router)
    pos, aff, starts, idx_t = topk_route(logits_t)
    xs = gather_tokens(idx_t.T, u)
    ffn_out = expert_ffn(xs, w_gate, w_up, w_down, layer)
    return moe_combine(h, pos, aff, starts, ffn_out, next_norm_w, last)


def _scan_masks():
    lower = np.tril(np.ones((CHUNK, CHUNK), np.float32))
    return jnp.asarray(np.stack([lower, lower.T]))


def _pad_cols(w, width):
    return jnp.pad(w, ((0, 0), (0, width - w.shape[1])))


def _pad_rows(w, height):
    return jnp.pad(w, ((0, height - w.shape[0]), (0, 0)))


def ab_mixer(h, u, w_in, conv_w, conv_b, a_log, dt_bias, d_skip, ssd_norm, sink, w_out):
    zx_w = SSD_INNER + SSD_CONV_CH
    dt0 = zx_w
    qkv0 = zx_w + 2 * SSD_HEADS
    w_t = w_in.T
    z = matmul([u], w_t, 512, col0=0, n=SSD_INNER, w_t=True)
    xbc = matmul([u], w_t, 512, col0=SSD_INNER, n=SSD_CONV_CH, w_t=True)
    w_dt = jnp.concatenate([_pad_rows(w_t[dt0:dt0 + SSD_HEADS], CHUNK),
                            _pad_rows(w_t[dt0 + SSD_HEADS:qkv0], CHUNK)], axis=0)
    dt_raw = matmul([u], w_dt, 2 * CHUNK, w_t=True)
    qkv = matmul([u], w_t[qkv0:], 512, w_t=True)

    xbc_act = ssd_conv(xbc, conv_w, conv_b)
    dt_bias_rows = _pad_cols(dt_bias, CHUNK).reshape(2, 1, CHUNK)
    acoef_rows = _pad_cols(-jnp.exp(a_log), CHUNK).reshape(2, 1, CHUNK)
    yf, yb = ssd_scan(xbc_act, dt_raw, dt_bias_rows, acoef_rows, _scan_masks())
    d_skip_row = jnp.repeat(d_skip, SSD_HEAD_DIM).reshape(1, SSD_INNER)
    y_ssd = ssd_final(yf, yb, xbc_act, z, d_skip_row, ssd_norm)

    slopes = 2.0 ** (-8.0 * jnp.arange(1, ATT_Q_HEADS + 1, dtype=F32) / ATT_Q_HEADS)
    y_att = window_attention(qkv, slopes, sink)
    return matmul([y_ssd, y_att], w_out, 512, res=h)


def cd_mixer(h, u, w_in, i_bias, f_bias, head_norm, w_out):
    qkvo_w = 2 * ML_HEADS * ML_QK_DIM + 2 * ML_WIDTH
    i0 = qkvo_w
    f0 = i0 + 2 * ML_HEADS
    fn0 = f0 + 2 * ML_HEADS
    w_t = w_in.T
    qkvo = matmul([u], w_t, 512, col0=0, n=qkvo_w, w_t=True)

    def gate_block(dirn):
        wi = w_t[i0 + dirn * ML_HEADS:i0 + (dirn + 1) * ML_HEADS]
        wf = w_t[f0 + dirn * ML_HEADS:f0 + (dirn + 1) * ML_HEADS]
        return _pad_rows(jnp.concatenate([_pad_rows(wi, ML_F_LANE), wf], axis=0), CHUNK)

    def bias_block(dirn):
        row = jnp.concatenate([jnp.pad(i_bias[dirn], (0, ML_F_LANE - ML_HEADS)), f_bias[dirn]])
        return jnp.pad(row, (0, CHUNK - row.shape[0]))

    gates = matmul([u], jnp.concatenate([gate_block(0), gate_block(1)], axis=0), 2 * CHUNK, w_t=True)
    gate_bias_rows = jnp.stack([bias_block(0), bias_block(1)]).reshape(2, 1, CHUNK)
    hf, hb = mlstm_scan(qkvo, gates, gate_bias_rows, _scan_masks())
    y_ml = mlstm_final(hf, hb, qkvo, head_norm)

    u_fn = matmul([u], w_t[fn0:], 512, w_t=True)
    y_fn = fourier_mixer(u_fn[PAD:])
    y_fn = jnp.pad(y_fn, ((PAD, 0), (0, 0))).astype(BF16)
    return matmul([y_ml, y_fn], w_out, 512, res=h)


def kernel(x, meta_tokens, norm_mix, ab_w_in, ab_conv_w, ab_conv_b, ab_a_log, ab_dt_bias, ab_d_skip, ab_ssd_norm,
           ab_sink, ab_w_out, cd_w_in, cd_i_bias, cd_f_bias, cd_head_norm, cd_w_out, norm_ffn, moe_router,
           moe_w_gate, moe_w_up, moe_w_down, final_norm):
    bsz, seq, d = x.shape
    assert bsz == 1 and d == D_MODEL and seq % CHUNK == 0
    depth = norm_mix.shape[0]
    h = jnp.concatenate([jnp.zeros((PAD, d), F32), meta_tokens.astype(F32), x[0]], axis=0)
    u = rmsnorm(h, norm_mix[0], BF16)
    for layer in range(depth):
        j = layer // 2
        if layer % 2 == 0:
            h = ab_mixer(h, u, ab_w_in[j], ab_conv_w[j], ab_conv_b[j], ab_a_log[j], ab_dt_bias[j],
                         ab_d_skip[j], ab_ssd_norm[j], ab_sink[j], ab_w_out[j])
        else:
            h = cd_mixer(h, u, cd_w_in[j], cd_i_bias[j], cd_f_bias[j], cd_head_norm[j], cd_w_out[j])
        last = layer == depth - 1
        next_norm_w = final_norm if last else norm_mix[layer + 1]
        res = moe_layer(h, norm_ffn[layer], moe_router[layer], moe_w_gate, moe_w_up, moe_w_down, layer,
                        next_norm_w, last)
        if last:
            return res[None]
        h, u = res
```

```python
import functools
import math

import numpy as np
import jax
import jax.numpy as jnp
from jax import lax
from jax.experimental import pallas as pl
from jax.experimental.pallas import tpu as pltpu

F32 = jnp.float32
BF16 = jnp.bfloat16
I32 = jnp.int32
HIGHEST = lax.Precision.HIGHEST

D_MODEL = 4096
N_META = 16
CHUNK = 128
PAD = CHUNK - N_META
NORM_EPS = 1e-6
NEG = -1e30

SSD_HEADS = 32
SSD_HEAD_DIM = 64
SSD_INNER = SSD_HEADS * SSD_HEAD_DIM
SSD_GROUPS = 4
SSD_STATE = 128
SSD_CONV = 5
SSD_CONV_CH = SSD_INNER + 2 * SSD_GROUPS * SSD_STATE

ATT_Q_HEADS = 16
ATT_KV_HEADS = 4
ATT_HEAD_DIM = 128
ATT_WINDOW = 128
ATT_GRP = ATT_Q_HEADS // ATT_KV_HEADS

ML_HEADS = 6
ML_QK_DIM = 256
ML_V_DIM = 512
ML_WIDTH = ML_HEADS * ML_V_DIM
ML_F_LANE = 8

FN_GROUPS = 4
FN_GROUP_DIM = 256
FN_WIDTH = FN_GROUPS * FN_GROUP_DIM

N_EXPERTS = 16
EC_FACTOR = 2
EXPERT_FF = 1536
FF_TILE = 256
OUT_TILE = 512
COMBINE_SLAB = 512

VMEM_LIMIT = 56 * 1024 * 1024


def _cparams(sem):
    return pltpu.CompilerParams(dimension_semantics=sem, vmem_limit_bytes=VMEM_LIMIT)


def _row_tile(lp):
    return 640 if lp % 640 == 0 else CHUNK


def _sigmoid(x):
    return 1.0 / (1.0 + jnp.exp(-x))


def _softplus(x):
    return jnp.maximum(x, 0.0) + jnp.log1p(jnp.exp(-jnp.abs(x)))


NT = (((1,), (1,)), ((), ()))


def _bdot(a, b, dims=None):
    if dims is None:
        return jnp.dot(a, b, preferred_element_type=F32)
    return lax.dot_general(a, b, dims, preferred_element_type=F32)


def _split_bf16(a):
    hi = a.astype(BF16)
    return hi, (a - hi.astype(F32)).astype(BF16)


def _rmsnorm_kernel(h_ref, w_ref, o_ref):
    x = h_ref[...]
    y = x * lax.rsqrt(jnp.mean(x * x, axis=-1, keepdims=True) + NORM_EPS) * w_ref[...]
    o_ref[...] = y.astype(o_ref.dtype)


def rmsnorm(h, w, out_dtype):
    lp, d = h.shape
    tm = _row_tile(lp)
    return pl.pallas_call(
        _rmsnorm_kernel,
        grid=(lp // tm,),
        in_specs=[pl.BlockSpec((tm, d), lambda i: (i, 0)), pl.BlockSpec((1, d), lambda i: (0, 0))],
        out_specs=pl.BlockSpec((tm, d), lambda i: (i, 0)),
        out_shape=jax.ShapeDtypeStruct((lp, d), out_dtype),
        compiler_params=_cparams(("arbitrary",)),
        name="rmsnorm",
    )(h, w.reshape(1, d))


def _rmsnorm_router_kernel(h_ref, w_ref, wrt_ref, u_ref, lg_ref):
    x = h_ref[...]
    y = x * lax.rsqrt(jnp.mean(x * x, axis=-1, keepdims=True) + NORM_EPS) * w_ref[...]
    u_ref[...] = y
    wh, wl = _split_bf16(wrt_ref[...])
    yh, yl = _split_bf16(y)
    lg_ref[...] = _bdot(wh, yh, NT) + _bdot(wl, yh, NT) + _bdot(wh, yl, NT)


def rmsnorm_router(h, w, w_router):
    lp, d = h.shape
    tm = _row_tile(lp)
    return pl.pallas_call(
        _rmsnorm_router_kernel,
        grid=(lp // tm,),
        in_specs=[pl.BlockSpec((tm, d), lambda i: (i, 0)), pl.BlockSpec((1, d), lambda i: (0, 0)),
                  pl.BlockSpec((N_EXPERTS, d), lambda i: (0, 0))],
        out_specs=[pl.BlockSpec((tm, d), lambda i: (i, 0)), pl.BlockSpec((N_EXPERTS, tm), lambda i: (0, i))],
        out_shape=[jax.ShapeDtypeStruct((lp, d), F32), jax.ShapeDtypeStruct((N_EXPERTS, lp), F32)],
        compiler_params=_cparams(("arbitrary",)),
        name="rmsnorm_router",
    )(h, w.reshape(1, d), w_router.T)


def _matmul_kernel(*refs, k_splits, has_res, w_t):
    n_x = len(k_splits)
    x_refs = refs[:n_x]
    w_ref = refs[n_x]
    r_ref = refs[n_x + 1] if has_res else None
    o_ref = refs[n_x + 1 + int(has_res)]
    wb_ref = refs[n_x + 2 + int(has_res)]

    @pl.when(pl.program_id(1) == 0)
    def _():
        w = w_ref[...].T if w_t else w_ref[...]
        wb_ref[...] = w.astype(BF16)

    acc = None
    k0 = 0
    for x_ref, kk in zip(x_refs, k_splits):
        part = jnp.dot(x_ref[...], wb_ref[k0:k0 + kk, :], preferred_element_type=F32)
        acc = part if acc is None else acc + part
        k0 += kk
    if has_res:
        acc = acc + r_ref[...]
    o_ref[...] = acc.astype(o_ref.dtype)


def matmul(xs, w, tn, res=None, out_dtype=F32, col0=0, n=None, w_t=False):
    lp = xs[0].shape[0]
    k_splits = tuple(int(x.shape[1]) for x in xs)
    k, n_all = (w.shape[1], w.shape[0]) if w_t else w.shape
    n = n_all - col0 if n is None else n
    assert sum(k_splits) == k and n % tn == 0 and col0 % tn == 0
    cb0 = col0 // tn
    tm = _row_tile(lp)
    in_specs = [pl.BlockSpec((tm, kk), lambda j, i: (i, 0)) for kk in k_splits]
    if w_t:
        in_specs.append(pl.BlockSpec((tn, k), lambda j, i: (cb0 + j, 0)))
    else:
        in_specs.append(pl.BlockSpec((k, tn), lambda j, i: (0, cb0 + j)))
    args = list(xs) + [w]
    if res is not None:
        in_specs.append(pl.BlockSpec((tm, tn), lambda j, i: (i, j)))
        args.append(res)
    return pl.pallas_call(
        functools.partial(_matmul_kernel, k_splits=k_splits, has_res=res is not None, w_t=w_t),
        grid=(n // tn, lp // tm),
        in_specs=in_specs,
        out_specs=pl.BlockSpec((tm, tn), lambda j, i: (i, j)),
        out_shape=jax.ShapeDtypeStruct((lp, n), out_dtype),
        scratch_shapes=[pltpu.VMEM((k, tn), BF16)],
        compiler_params=_cparams(("arbitrary", "arbitrary")),
        name="matmul",
    )(*args)


def _ssd_conv_kernel(cur_ref, prev_ref, next_ref, w_ref, b_ref, o_ref, ext_ref):
    c = pl.program_id(0)
    nch = pl.num_programs(0)
    halo = 8
    ext_ref[0:halo, :] = jnp.where(c > 0, prev_ref[...], 0.0)
    ext_ref[halo:halo + CHUNK, :] = cur_ref[...]
    ext_ref[halo + CHUNK:2 * halo + CHUNK, :] = jnp.where(c < nch - 1, next_ref[...], 0.0)
    w = w_ref[...]
    acc = jnp.zeros(cur_ref.shape, F32) + b_ref[...]
    centre = (SSD_CONV - 1) // 2
    for j in range(SSD_CONV):
        acc = acc + w[j:j + 1, :] * ext_ref[pl.ds(halo - centre + j, CHUNK), :]
    y = acc * _sigmoid(acc)
    row = c * CHUNK + lax.broadcasted_iota(I32, y.shape, 0)
    o_ref[...] = jnp.where(row >= PAD, y, 0.0)


def ssd_conv(xbc, conv_w, conv_b):
    lp, ch = xbc.shape
    nch = lp // CHUNK
    sub = CHUNK // 8
    return pl.pallas_call(
        _ssd_conv_kernel,
        grid=(nch,),
        in_specs=[
            pl.BlockSpec((CHUNK, ch), lambda c: (c, 0)),
            pl.BlockSpec((8, ch), lambda c: (jnp.maximum(c * sub - 1, 0), 0)),
            pl.BlockSpec((8, ch), lambda c: (jnp.minimum((c + 1) * sub, nch * sub - 1), 0)),
            pl.BlockSpec((SSD_CONV, ch), lambda c: (0, 0)),
            pl.BlockSpec((1, ch), lambda c: (0, 0)),
        ],
        out_specs=pl.BlockSpec((CHUNK, ch), lambda c: (c, 0)),
        out_shape=jax.ShapeDtypeStruct((lp, ch), F32),
        scratch_shapes=[pltpu.VMEM((CHUNK + 16, ch), F32)],
        compiler_params=_cparams(("arbitrary",)),
        name="ssd_conv",
    )(xbc, xbc, xbc, conv_w, conv_b.reshape(1, ch))


def _ssd_direction(d, ce, x_ref, b_ref, c_ref, dt_ref, dtb_ref, acoef_ref, tri_ref, y_ref, state_ref):
    tri = tri_ref[d]
    trib = tri > 0.5
    row = ce * CHUNK + lax.broadcasted_iota(I32, (CHUNK, CHUNK), 0)
    dt = jnp.where(row >= PAD, _softplus(dt_ref[...] + dtb_ref[d]), 0.0)
    a = dt * acoef_ref[d]
    a_cum = jnp.dot(tri, a, precision=HIGHEST, preferred_element_type=F32)
    a_cum_t = a_cum.T
    a_tot = jnp.sum(a, axis=0, keepdims=True)
    e_cum = jnp.exp(a_cum)
    d_out = jnp.exp(a_tot - a_cum)
    e_tot = jnp.exp(a_tot)
    lane = lax.broadcasted_iota(I32, (CHUNK, CHUNK), 1)
    first = lane < SSD_HEAD_DIM
    first_row = first[0:1, :]

    def pair_cols(m, h0):
        return jnp.where(first, m[:, h0:h0 + 1], m[:, h0 + 1:h0 + 2])

    hpg = SSD_HEADS // SSD_GROUPS
    for g in range(SSD_GROUPS):
        bg = b_ref[:, g * SSD_STATE:(g + 1) * SSD_STATE]
        cg = c_ref[:, g * SSD_STATE:(g + 1) * SSD_STATE].astype(BF16)
        cb = _bdot(cg, bg.astype(BF16), NT)
        bt = bg.T.astype(BF16)
        for pp in range(hpg // 2):
            h0 = g * hpg + 2 * pp
            p_idx = d * (SSD_HEADS // 2) + h0 // 2
            lo = h0 * SSD_HEAD_DIM
            x_p = x_ref[:, lo:lo + 2 * SSD_HEAD_DIM]
            xdt = x_p * pair_cols(dt, h0)
            xw = (xdt * pair_cols(d_out, h0)).astype(BF16)
            xdt_b = xdt.astype(BF16)
            yd = []
            for hh in (h0, h0 + 1):
                diff = a_cum[:, hh:hh + 1] - a_cum_t[hh:hh + 1, :]
                dec = jnp.exp(jnp.where(trib, diff, NEG))
                yd.append(_bdot((cb * dec).astype(BF16), xdt_b))
            y_diag = jnp.where(first, yd[0], yd[1])
            st_prev = state_ref[p_idx]
            y_off = _bdot(cg, st_prev.astype(BF16)) * pair_cols(e_cum, h0)
            y_ref[:, lo:lo + 2 * SSD_HEAD_DIM] = y_diag + y_off
            st_new = _bdot(bt, xw)
            cdec = jnp.where(first_row, e_tot[:, h0:h0 + 1], e_tot[:, h0 + 1:h0 + 2])
            state_ref[p_idx] = st_prev * cdec + st_new


def _ssd_scan_kernel(xf_ref, bf_ref, cf_ref, dtf_ref, xb_ref, bb_ref, cb_ref, dtbk_ref, dtb_ref, acoef_ref, tri_ref,
                     yf_ref, yb_ref, state_ref):
    c = pl.program_id(0)
    nch = pl.num_programs(0)

    @pl.when(c == 0)
    def _():
        state_ref[...] = jnp.zeros(state_ref.shape, F32)

    _ssd_direction(0, c, xf_ref, bf_ref, cf_ref, dtf_ref, dtb_ref, acoef_ref, tri_ref, yf_ref, state_ref)
    _ssd_direction(1, nch - 1 - c, xb_ref, bb_ref, cb_ref, dtbk_ref, dtb_ref, acoef_ref, tri_ref, yb_ref, state_ref)


def ssd_scan(xbc_act, dt_raw, dt_bias_rows, acoef_rows, tri2):
    lp = xbc_act.shape[0]
    nch = lp // CHUNK
    gs = SSD_GROUPS * SSD_STATE

    def specs(chunk, d):
        return [
            pl.BlockSpec((CHUNK, SSD_INNER), lambda c: (chunk(c), 0)),
            pl.BlockSpec((CHUNK, gs), lambda c: (chunk(c), SSD_INNER // gs)),
            pl.BlockSpec((CHUNK, gs), lambda c: (chunk(c), SSD_INNER // gs + 1)),
            pl.BlockSpec((CHUNK, CHUNK), lambda c: (chunk(c), d)),
        ]

    fwd = lambda c: c
    bwd = lambda c: nch - 1 - c
    return pl.pallas_call(
        _ssd_scan_kernel,
        grid=(nch,),
        in_specs=specs(fwd, 0) + specs(bwd, 1) + [
            pl.BlockSpec((2, 1, CHUNK), lambda c: (0, 0, 0)),
            pl.BlockSpec((2, 1, CHUNK), lambda c: (0, 0, 0)),
            pl.BlockSpec((2, CHUNK, CHUNK), lambda c: (0, 0, 0)),
        ],
        out_specs=[pl.BlockSpec((CHUNK, SSD_INNER), lambda c: (c, 0)),
                   pl.BlockSpec((CHUNK, SSD_INNER), lambda c: (nch - 1 - c, 0))],
        out_shape=[jax.ShapeDtypeStruct((lp, SSD_INNER), F32), jax.ShapeDtypeStruct((lp, SSD_INNER), F32)],
        scratch_shapes=[pltpu.VMEM((SSD_HEADS, SSD_STATE, 2 * SSD_HEAD_DIM), F32)],
        compiler_params=_cparams(("arbitrary",)),
        name="ssd_scan",
    )(xbc_act, xbc_act, xbc_act, dt_raw, xbc_act, xbc_act, xbc_act, dt_raw, dt_bias_rows, acoef_rows, tri2)


def _ssd_final_kernel(yf_ref, yb_ref, x_ref, z_ref, dskip_ref, nw_ref, o_ref):
    y = yf_ref[...] + yb_ref[...] + dskip_ref[...] * x_ref[...]
    z = z_ref[...]
    y = y * (z * _sigmoid(z))
    gw = SSD_INNER // SSD_GROUPS
    for g in range(SSD_GROUPS):
        yg = y[:, g * gw:(g + 1) * gw]
        yg = yg * lax.rsqrt(jnp.mean(yg * yg, axis=-1, keepdims=True) + NORM_EPS)
        o_ref[:, g * gw:(g + 1) * gw] = (yg * nw_ref[:, g * gw:(g + 1) * gw]).astype(o_ref.dtype)


def ssd_final(yf, yb, xbc_act, z, d_skip_row, norm_w):
    lp = xbc_act.shape[0]
    blk = pl.BlockSpec((CHUNK, SSD_INNER), lambda c: (c, 0))
    row = pl.BlockSpec((1, SSD_INNER), lambda c: (0, 0))
    return pl.pallas_call(
        _ssd_final_kernel,
        grid=(lp // CHUNK,),
        in_specs=[blk, blk, blk, blk, row, row],
        out_specs=blk,
        out_shape=jax.ShapeDtypeStruct((lp, SSD_INNER), BF16),
        compiler_params=_cparams(("arbitrary",)),
        name="ssd_final",
    )(yf, yb, xbc_act, z, d_skip_row, norm_w.reshape(1, SSD_INNER))


def _attn_kernel(slope_ref, sink_ref, q_ref, km_ref, kp_ref, ks_ref, kn_ref, vm_ref, vp_ref, vs_ref, vn_ref, o_ref):
    b = pl.program_id(0)
    nb = pl.num_programs(0)
    scale = ATT_HEAD_DIM ** -0.5

    qpos = b * CHUNK - PAD + lax.broadcasted_iota(I32, (CHUNK, 4 * CHUNK), 0)
    col = lax.broadcasted_iota(I32, (CHUNK, 4 * CHUNK), 1)
    slot = col // CHUNK
    j = col - slot * CHUNK
    kblock = jnp.where(slot == 0, 0, b - 2 + slot)
    kpos = kblock * CHUNK - PAD + j
    dist = jnp.abs(qpos - kpos)
    meta_ok = (slot == 0) & (kpos >= 0)
    real_ok = (slot > 0) & (kblock >= 1) & (kblock < nb) & (kpos >= N_META) & (dist <= ATT_WINDOW)
    valid = meta_ok | real_ok
    alibi = jnp.where(real_ok & (qpos >= N_META), dist.astype(F32), 0.0)
    q_real = qpos[:, 0:ATT_HEAD_DIM] >= 0

    for hk in range(ATT_KV_HEADS):
        ksl = slice(hk * ATT_HEAD_DIM, (hk + 1) * ATT_HEAD_DIM)
        kcat = jnp.concatenate([km_ref[:, ksl], kp_ref[:, ksl], ks_ref[:, ksl], kn_ref[:, ksl]],
                               axis=0).astype(BF16)
        vcat = jnp.concatenate([vm_ref[:, ksl], vp_ref[:, ksl], vs_ref[:, ksl], vn_ref[:, ksl]],
                               axis=0).astype(BF16)
        for g in range(ATT_GRP):
            h = hk * ATT_GRP + g
            hsl = slice(h * ATT_HEAD_DIM, (h + 1) * ATT_HEAD_DIM)
            qg = (q_ref[:, hsl] * scale).astype(BF16)
            s = _bdot(qg, kcat, NT)
            s = jnp.where(valid, s - slope_ref[h] * alibi, NEG)
            sink = sink_ref[h]
            m = jnp.maximum(jnp.max(s, axis=-1, keepdims=True), sink)
            p = jnp.exp(s - m)
            denom = jnp.sum(p, axis=-1, keepdims=True) + jnp.exp(sink - m)
            o = _bdot(p.astype(BF16), vcat) / denom
            o_ref[:, hsl] = jnp.where(q_real, o, 0.0).astype(o_ref.dtype)


def window_attention(qkv, slopes, sink):
    lp = qkv.shape[0]
    nb = lp // CHUNK
    qw = ATT_Q_HEADS * ATT_HEAD_DIM
    kvw = ATT_KV_HEADS * ATT_HEAD_DIM
    kcol = qw // kvw
    vcol = kcol + 1

    def kv_spec(col, which):
        def imap(b, *_):
            if which == 0:
                blk = 0
            else:
                blk = jnp.clip(b - 2 + which, 0, nb - 1)
            return (blk, col)
        return pl.BlockSpec((CHUNK, kvw), imap)

    return pl.pallas_call(
        _attn_kernel,
        grid_spec=pltpu.PrefetchScalarGridSpec(
            num_scalar_prefetch=2,
            grid=(nb,),
            in_specs=[pl.BlockSpec((CHUNK, qw), lambda b, *_: (b, 0))]
            + [kv_spec(kcol, w) for w in range(4)] + [kv_spec(vcol, w) for w in range(4)],
            out_specs=pl.BlockSpec((CHUNK, qw), lambda b, *_: (b, 0)),
        ),
        out_shape=jax.ShapeDtypeStruct((lp, qw), BF16),
        compiler_params=_cparams(("arbitrary",)),
        name="window_attention",
    )(slopes, sink, *([qkv] * 9))


def _mlstm_direction(d, ce, q_ref, k_ref, v_ref, g_ref, gb_ref, tri_ref, h_ref, c_ref, n_ref, m_ref):
    tri = tri_ref[d]
    trib = tri > 0.5
    row = ce * CHUNK + lax.broadcasted_iota(I32, (CHUNK, CHUNK), 0)
    pre = g_ref[...] + gb_ref[d]
    real = row >= PAD
    real_col = real[:, 0:1]
    li_all = jnp.where(real, pre, 0.0)
    lf_all = jnp.where(real, -_softplus(-pre), 0.0)
    bl_all = jnp.dot(tri, lf_all, precision=HIGHEST, preferred_element_type=F32)
    li_t = li_all.T
    bl_t = bl_all.T
    g_all = jnp.sum(lf_all, axis=0, keepdims=True)

    for hd in range(ML_HEADS):
        fl = hd + ML_F_LANE
        li_col = li_all[:, hd:hd + 1]
        li_row = li_t[hd:hd + 1, :]
        bl_col = bl_all[:, fl:fl + 1]
        bl_row = bl_t[fl:fl + 1, :]
        g_tot = g_all[:, fl:fl + 1]
        st = d * ML_HEADS + hd
        ksl = slice(hd * ML_QK_DIM, (hd + 1) * ML_QK_DIM)
        vsl = slice(hd * ML_V_DIM, (hd + 1) * ML_V_DIM)

        qf = jnp.where(real_col, q_ref[:, ksl], 0.0)
        q = qf.astype(BF16)
        kf = jnp.where(real_col, k_ref[:, ksl], 0.0) * (ML_QK_DIM ** -0.5)
        kt = kf.T.astype(BF16)
        v = jnp.where(real_col, v_ref[:, vsl], 0.0)
        vb = v.astype(BF16)

        m_prev = m_ref[st, 0:1, 0:1]
        n_prev = n_ref[st]
        c_prev = c_ref[st]

        dmat = jnp.where(trib, bl_col - bl_row + li_row, NEG)
        m_inter = bl_col + m_prev
        m_t = jnp.maximum(jnp.max(dmat, axis=1, keepdims=True), m_inter)
        qk = jnp.dot(q, kt, preferred_element_type=F32)
        p = jnp.exp(dmat - m_t) * qk
        w_inter = jnp.exp(m_inter - m_t)
        num = (jnp.dot(p.astype(BF16), vb, preferred_element_type=F32)
               + w_inter * jnp.dot(q, c_prev.astype(BF16), preferred_element_type=F32))
        qn = jnp.sum(qf * n_prev, axis=1, keepdims=True)
        den = jnp.sum(p, axis=1, keepdims=True) + w_inter * qn
        h_ref[:, vsl] = num / jnp.maximum(jnp.abs(den), jnp.exp(-m_t))

        a_col = g_tot - bl_col + li_col
        m_loc = jnp.max(a_col, axis=0, keepdims=True)
        m_new = jnp.maximum(g_tot + m_prev, m_loc)
        s_old = jnp.exp(g_tot + m_prev - m_new)
        s_new = jnp.exp(m_loc - m_new)
        wa = jnp.exp(a_col - m_loc) * s_new
        c_loc = jnp.dot(kt, (wa * v).astype(BF16), preferred_element_type=F32)
        n_loc = jnp.sum(wa * kf, axis=0, keepdims=True)
        c_ref[st] = s_old * c_prev + c_loc
        n_ref[st] = s_old * n_prev + n_loc
        m_ref[st] = jnp.broadcast_to(m_new, m_ref.shape[1:])


def _mlstm_scan_kernel(qf_ref, kf_ref, vf_ref, gf_ref, qb_ref, kb_ref, vb_ref, gbk_ref, gb_ref, tri_ref,
                       hf_ref, hb_ref, c_ref, n_ref, m_ref):
    c = pl.program_id(0)
    nch = pl.num_programs(0)

    @pl.when(c == 0)
    def _():
        c_ref[...] = jnp.zeros(c_ref.shape, F32)
        n_ref[...] = jnp.zeros(n_ref.shape, F32)
        m_ref[...] = jnp.zeros(m_ref.shape, F32)

    _mlstm_direction(0, c, qf_ref, kf_ref, vf_ref, gf_ref, gb_ref, tri_ref, hf_ref, c_ref, n_ref, m_ref)
    _mlstm_direction(1, nch - 1 - c, qb_ref, kb_ref, vb_ref, gbk_ref, gb_ref, tri_ref, hb_ref, c_ref, n_ref, m_ref)


def mlstm_scan(qkvo, gates, gate_bias_rows, tri2):
    lp = qkvo.shape[0]
    nch = lp // CHUNK
    qk_w = ML_HEADS * ML_QK_DIM

    def specs(chunk, d):
        return [
            pl.BlockSpec((CHUNK, qk_w), lambda c: (chunk(c), 0)),
            pl.BlockSpec((CHUNK, qk_w), lambda c: (chunk(c), 1)),
            pl.BlockSpec((CHUNK, ML_WIDTH), lambda c: (chunk(c), 2 * qk_w // ML_WIDTH)),
            pl.BlockSpec((CHUNK, CHUNK), lambda c: (chunk(c), d)),
        ]

    fwd = lambda c: c
    bwd = lambda c: nch - 1 - c
    n_state = 2 * ML_HEADS
    return pl.pallas_call(
        _mlstm_scan_kernel,
        grid=(nch,),
        in_specs=specs(fwd, 0) + specs(bwd, 1) + [
            pl.BlockSpec((2, 1, CHUNK), lambda c: (0, 0, 0)),
            pl.BlockSpec((2, CHUNK, CHUNK), lambda c: (0, 0, 0)),
        ],
        out_specs=[pl.BlockSpec((CHUNK, ML_WIDTH), lambda c: (c, 0)),
                   pl.BlockSpec((CHUNK, ML_WIDTH), lambda c: (nch - 1 - c, 0))],
        out_shape=[jax.ShapeDtypeStruct((lp, ML_WIDTH), F32), jax.ShapeDtypeStruct((lp, ML_WIDTH), F32)],
        scratch_shapes=[pltpu.VMEM((n_state, ML_QK_DIM, ML_V_DIM), F32), pltpu.VMEM((n_state, 1, ML_QK_DIM), F32),
                        pltpu.VMEM((n_state, 8, CHUNK), F32)],
        compiler_params=_cparams(("arbitrary",)),
        name="mlstm_scan",
    )(qkvo, qkvo, qkvo, gates, qkvo, qkvo, qkvo, gates, gate_bias_rows, tri2)


def _mlstm_final_kernel(hf_ref, hb_ref, o_ref_in, nw_ref, y_ref):
    h = hf_ref[...] + hb_ref[...]
    o = o_ref_in[...]
    for hd in range(ML_HEADS):
        sl = slice(hd * ML_V_DIM, (hd + 1) * ML_V_DIM)
        hh = h[:, sl]
        hh = hh * lax.rsqrt(jnp.mean(hh * hh, axis=-1, keepdims=True) + NORM_EPS) * nw_ref[:, sl]
        y_ref[:, sl] = (_sigmoid(o[:, sl]) * hh).astype(y_ref.dtype)


def mlstm_final(hf, hb, qkvo, head_norm):
    lp = qkvo.shape[0]
    ocol = (2 * ML_HEADS * ML_QK_DIM + ML_WIDTH) // ML_WIDTH
    return pl.pallas_call(
        _mlstm_final_kernel,
        grid=(lp // CHUNK,),
        in_specs=[
            pl.BlockSpec((CHUNK, ML_WIDTH), lambda c: (c, 0)),
            pl.BlockSpec((CHUNK, ML_WIDTH), lambda c: (c, 0)),
            pl.BlockSpec((CHUNK, ML_WIDTH), lambda c: (c, ocol)),
            pl.BlockSpec((1, ML_WIDTH), lambda c: (0, 0)),
        ],
        out_specs=pl.BlockSpec((CHUNK, ML_WIDTH), lambda c: (c, 0)),
        out_shape=jax.ShapeDtypeStruct((lp, ML_WIDTH), BF16),
        compiler_params=_cparams(("arbitrary",)),
        name="mlstm_final",
    )(hf, hb, qkvo, head_norm.reshape(1, ML_WIDTH))


def _dft_factors(n):
    best = None
    for n1 in range(8, n + 1, 8):
        if n % n1 == 0:
            n2 = n // n1
            if best is None or n1 + n2 < best[0] + best[1]:
                best = (n1, n2)
    assert best is not None
    return best


def _dot3(a, b):
    return _bdot(a[0], b[0]) + _bdot(a[1], b[0]) + _bdot(a[0], b[1])


def _fnet_stage1_kernel(x_ref, c2h_ref, c2l_ref, s2h_ref, s2l_ref, twc_ref, tws_ref, tr_ref, ti_ref, *, n2):
    x = x_ref[...]
    rows = lax.broadcasted_iota(I32, x.shape, 0)
    xs = _split_bf16(jnp.where(rows < n2, x, 0.0))
    cx = _dot3((c2h_ref[...], c2l_ref[...]), xs)
    sx = _dot3((s2h_ref[...], s2l_ref[...]), xs)
    twc = twc_ref[0]
    tws = tws_ref[0]
    tr_ref[...] = cx * twc - sx * tws
    ti_ref[...] = -(cx * tws + sx * twc)


def _fnet_stage2_kernel(tr_ref, ti_ref, c1h_ref, c1l_ref, s1h_ref, s1l_ref, cch_ref, ccl_ref, sch_ref, scl_ref, o_ref,
                        *, scale):
    tr = _split_bf16(tr_ref[0])
    ti = _split_bf16(ti_ref[0])
    c1 = (c1h_ref[...], c1l_ref[...])
    s1 = (s1h_ref[...], s1l_ref[...])
    yr = _dot3(c1, tr) + _dot3(s1, ti)
    yi = _dot3(c1, ti) - _dot3(s1, tr)
    cc = (cch_ref[...], ccl_ref[...])
    sc = (sch_ref[...], scl_ref[...])
    for g in range(FN_GROUPS):
        sl = slice(g * FN_GROUP_DIM, (g + 1) * FN_GROUP_DIM)
        out = _dot3(_split_bf16(yr[:, sl]), cc) + _dot3(_split_bf16(yi[:, sl]), sc)
        o_ref[:, sl] = out * scale


def fourier_mixer(u_fn):
    n, ch = u_fn.shape
    n1, n2 = _dft_factors(n)
    n2p = -(-n2 // 8) * 8
    two_pi = 2.0 * np.pi

    def cs(size, pad):
        idx = np.arange(size)
        ang = two_pi * ((idx[:, None] * idx[None, :]) % size) / size
        out = []
        for fn in (np.cos, np.sin):
            m = np.zeros((pad, pad), np.float32)
            m[:size, :size] = fn(ang)
            hi = jnp.asarray(m).astype(BF16)
            out += [hi, (jnp.asarray(m) - hi.astype(F32)).astype(BF16)]
        return out

    dft2 = cs(n2, n2p)
    dft1 = cs(n1, n1)
    dftc = cs(FN_GROUP_DIM, FN_GROUP_DIM)
    tw_ang = two_pi * ((np.arange(n1)[:, None] * np.arange(n2)[None, :]) % n) / n
    twc = np.zeros((n1, n2p, 1), np.float32)
    tws = np.zeros((n1, n2p, 1), np.float32)
    twc[:, :n2, 0] = np.cos(tw_ang)
    tws[:, :n2, 0] = np.sin(tw_ang)

    def full(shape):
        return pl.BlockSpec(shape, lambda i: (0,) * len(shape))

    x2 = u_fn.reshape(n2, n1 * ch)
    tr, ti = pl.pallas_call(
        functools.partial(_fnet_stage1_kernel, n2=n2),
        grid=(n1,),
        in_specs=[pl.BlockSpec((n2p, ch), lambda i: (0, i))] + [full((n2p, n2p))] * 4 + [
            pl.BlockSpec((1, n2p, 1), lambda i: (i, 0, 0)),
            pl.BlockSpec((1, n2p, 1), lambda i: (i, 0, 0)),
        ],
        out_specs=[pl.BlockSpec((n2p, ch), lambda i: (0, i)), pl.BlockSpec((n2p, ch), lambda i: (0, i))],
        out_shape=[jax.ShapeDtypeStruct((n2p, n1 * ch), F32), jax.ShapeDtypeStruct((n2p, n1 * ch), F32)],
        compiler_params=_cparams(("arbitrary",)),
        name="fnet_stage1",
    )(x2, *dft2, jnp.asarray(twc), jnp.asarray(tws))

    tr3 = tr.reshape(n2p, n1, ch)
    ti3 = ti.reshape(n2p, n1, ch)
    scale = 1.0 / math.sqrt(float(n) * FN_GROUP_DIM)
    out = pl.pallas_call(
        functools.partial(_fnet_stage2_kernel, scale=scale),
        grid=(n2,),
        in_specs=[
            pl.BlockSpec((1, n1, ch), lambda i: (i, 0, 0)),
            pl.BlockSpec((1, n1, ch), lambda i: (i, 0, 0)),
        ] + [full((n1, n1))] * 4 + [full((FN_GROUP_DIM, FN_GROUP_DIM))] * 4,
        out_specs=pl.BlockSpec((n1, ch), lambda i: (0, i)),
        out_shape=jax.ShapeDtypeStruct((n1, n2 * ch), F32),
        compiler_params=_cparams(("arbitrary",)),
        name="fnet_stage2",
    )(tr3, ti3, *dft1, *dftc)
    return out.reshape(n, ch)


def _cap_sizes(lp):
    n_tok = lp - PAD
    cap = EC_FACTOR * n_tok // N_EXPERTS
    capp = -(-cap // 16) * 16
    return cap, capp


def _topk_kernel(lg_ref, su_ref, pos_ref, aff_ref, start_ref, idx_ref, rank_scr, pos_scr, *, cap, capp, jb):
    ne, lp = lg_ref.shape
    nch = lp // CHUNK
    lg = lg_ref[...]
    mx = jnp.max(lg, axis=0, keepdims=True)
    ex = jnp.exp(lg - mx)
    aff = ex / jnp.sum(ex, axis=0, keepdims=True)
    lane = lax.broadcasted_iota(I32, (ne, lp), 1)
    real = lane >= PAD
    bits = jnp.where(real, pltpu.bitcast(aff, I32), -1)
    capf = float(cap)

    def bisect(i, thr):
        cand = thr | jnp.left_shift(jnp.int32(1), 30 - i)
        cnt = jnp.sum(jnp.where(bits >= cand, 1.0, 0.0), axis=1, keepdims=True)
        return jnp.where(cnt >= capf, cand, thr)

    thr = lax.fori_loop(0, 31, bisect, jnp.zeros((ne, 1), I32))
    gt = bits > thr
    eq = bits == thr
    need = capf - jnp.sum(jnp.where(gt, 1.0, 0.0), axis=1, keepdims=True)

    su = su_ref[...]

    def excl_cumsum(mask_f32, out_scr, want_starts):
        carry = jnp.zeros((ne, 1), F32)
        starts = jnp.zeros((ne, CHUNK), F32)
        lane_c = lax.broadcasted_iota(I32, (ne, CHUNK), 1)
        for c in range(nch):
            tile = mask_f32[:, c * CHUNK:(c + 1) * CHUNK]
            within = jnp.dot(tile.astype(BF16), su, preferred_element_type=F32)
            out_scr[:, c * CHUNK:(c + 1) * CHUNK] = within + carry
            if want_starts:
                starts = jnp.where(lane_c == c, carry, starts)
            carry = carry + jnp.sum(tile, axis=1, keepdims=True)
        if want_starts:
            starts = jnp.where(lane_c == nch, carry, starts)
        return starts

    excl_cumsum(jnp.where(eq, 1.0, 0.0), rank_scr, False)
    sel = gt | (eq & (rank_scr[...] < need))
    starts = excl_cumsum(jnp.where(sel, 1.0, 0.0), pos_scr, True)
    posf = jnp.where(sel, pos_scr[...], -1.0)
    pos_scr[...] = posf
    pos_ref[...] = posf.astype(I32)
    aff_ref[...] = jnp.where(sel, aff, 0.0)
    start_ref[...] = starts.astype(I32)

    tok = lax.broadcasted_iota(I32, (jb, lp), 1).astype(F32)
    lane_e = lax.broadcasted_iota(I32, (jb, ne), 1)
    for blk in range(capp // jb):
        jcol = (blk * jb + lax.broadcasted_iota(I32, (jb, 1), 0)).astype(F32)

        def per_expert(e, acc):
            prow = pos_scr[pl.ds(e, 1), :]
            hit = jnp.sum(jnp.where(prow == jcol, tok, 0.0), axis=1, keepdims=True)
            return jnp.where(lane_e == e, hit, acc)

        acc = lax.fori_loop(0, ne, per_expert, jnp.zeros((jb, ne), F32))
        idx_ref[blk * jb:(blk + 1) * jb, :] = acc.astype(I32)


def topk_route(logits_t):
    ne, lp = logits_t.shape
    cap, capp = _cap_sizes(lp)
    jb = 80 if capp % 80 == 0 else 16
    su = jnp.asarray(np.triu(np.ones((CHUNK, CHUNK), np.float32), 1), BF16)
    return pl.pallas_call(
        functools.partial(_topk_kernel, cap=cap, capp=capp, jb=jb),
        out_shape=[jax.ShapeDtypeStruct((ne, lp), I32), jax.ShapeDtypeStruct((ne, lp), F32),
                   jax.ShapeDtypeStruct((ne, CHUNK), I32), jax.ShapeDtypeStruct((capp, ne), I32)],
        scratch_shapes=[pltpu.VMEM((ne, lp), F32), pltpu.VMEM((ne, lp), F32)],
        compiler_params=pltpu.CompilerParams(vmem_limit_bytes=VMEM_LIMIT),
        name="topk_route",
    )(logits_t, su)


GATHER_UNROLL = 8


def _gather_kernel(idx_ref, u_hbm, o_ref, stage_ref, sem):
    e = pl.program_id(0)
    ne = pl.num_programs(0)
    capp = stage_ref.shape[1]

    def issue(eq, slot):
        def body(jo, carry):
            for ji in range(GATHER_UNROLL):
                j = jo * GATHER_UNROLL + ji
                t = idx_ref[eq, j]
                pltpu.make_async_copy(u_hbm.at[pl.ds(t, 1), :], stage_ref.at[slot, pl.ds(j, 1), :],
                                      sem.at[slot]).start()
            return carry

        lax.fori_loop(0, capp // GATHER_UNROLL, body, 0)

    @pl.when(e == 0)
    def _():
        issue(e, 0)

    @pl.when(e + 1 < ne)
    def _():
        issue(e + 1, (e + 1) % 2)

    slot = e % 2
    pltpu.make_async_copy(u_hbm.at[pl.ds(0, capp), :], stage_ref.at[slot], sem.at[slot]).wait()
    o_ref[0] = stage_ref[slot].astype(o_ref.dtype)


def gather_tokens(idx, u):
    ne, capp = idx.shape
    lp, d = u.shape
    assert capp % GATHER_UNROLL == 0
    return pl.pallas_call(
        _gather_kernel,
        grid_spec=pltpu.PrefetchScalarGridSpec(
            num_scalar_prefetch=1,
            grid=(ne,),
            in_specs=[pl.BlockSpec(memory_space=pl.ANY)],
            out_specs=pl.BlockSpec((1, capp, d), lambda e, *_: (e, 0, 0)),
            scratch_shapes=[pltpu.VMEM((2, capp, d), F32), pltpu.SemaphoreType.DMA((2,))],
        ),
        out_shape=jax.ShapeDtypeStruct((ne, capp, d), BF16),
        compiler_params=_cparams(("arbitrary",)),
        name="moe_gather",
    )(idx, u)


def _ffn_kernel(xs_ref, wg_ref, wu_ref, wd_ref, o_ref, hdn_ref, *, n_ff):
    s = pl.program_id(1)

    @pl.when(s < n_ff)
    def _():
        x = xs_ref[0]
        g = jnp.dot(x, wg_ref[0, 0].astype(BF16), preferred_element_type=F32)
        u = jnp.dot(x, wu_ref[0, 0].astype(BF16), preferred_element_type=F32)
        hdn_ref[s] = (g * _sigmoid(g) * u).astype(BF16)

    @pl.when(s >= n_ff)
    def _():
        wd = wd_ref[0, 0].astype(BF16)
        acc = None
        for j in range(n_ff):
            part = jnp.dot(hdn_ref[j], wd[j * FF_TILE:(j + 1) * FF_TILE, :], preferred_element_type=F32)
            acc = part if acc is None else acc + part
        o_ref[0] = acc


def expert_ffn(xs, w_gate, w_up, w_down, layer):
    ne, capp, d = xs.shape
    n_ff = EXPERT_FF // FF_TILE
    n_out = d // OUT_TILE
    return pl.pallas_call(
        functools.partial(_ffn_kernel, n_ff=n_ff),
        grid=(ne, n_ff + n_out),
        in_specs=[
            pl.BlockSpec((1, capp, d), lambda e, s: (e, 0, 0)),
            pl.BlockSpec((1, 1, d, FF_TILE), lambda e, s: (layer, e, 0, jnp.minimum(s, n_ff - 1))),
            pl.BlockSpec((1, 1, d, FF_TILE), lambda e, s: (layer, e, 0, jnp.minimum(s, n_ff - 1))),
            pl.BlockSpec((1, 1, EXPERT_FF, OUT_TILE), lambda e, s: (layer, e, 0, jnp.maximum(s - n_ff, 0))),
        ],
        out_specs=pl.BlockSpec((1, capp, OUT_TILE), lambda e, s: (e, 0, jnp.maximum(s - n_ff, 0))),
        out_shape=jax.ShapeDtypeStruct((ne, capp, d), F32),
        scratch_shapes=[pltpu.VMEM((n_ff, capp, FF_TILE), BF16)],
        compiler_params=_cparams(("arbitrary", "arbitrary")),
        name="expert_ffn",
    )(xs, w_gate, w_up, w_down)


def _combine_kernel(start_ref, h_ref, pos_ref, aff_ref, nw_ref, out_hbm, *refs, capp, rows, n_half, emit_h):
    if emit_h:
        o_ref, y_ref, acc_ref, g_ref, sem = refs
    else:
        y_ref, acc_ref, g_ref, sem = refs
    b = pl.program_id(0)
    hh = pl.program_id(1)
    nb = pl.num_programs(0)
    ne = pos_ref.shape[0]
    eh = ne // n_half

    def row_copy(src, dst, slot, n_rows=1):
        return pltpu.make_async_copy(out_hbm.at[pl.ds(src, n_rows), :], g_ref.at[slot, pl.ds(dst, n_rows), :],
                                     sem.at[slot])

    def block_rows(bq, hq):
        s0 = [start_ref[hq * eh + i, bq] for i in range(eh)]
        n = [start_ref[hq * eh + i, bq + 1] - s0[i] for i in range(eh)]
        return s0, n

    def issue(bq, hq, slot):
        s0, n = block_rows(bq, hq)
        off = jnp.int32(0)
        for i in range(eh):
            base = (hq * eh + i) * capp + s0[i]

            def body(j, carry, base=base, off=off):
                row_copy(base + j, off + j, slot).start()
                return carry

            lax.fori_loop(0, n[i], body, 0)
            off = off + n[i]

    @pl.when((b == 0) & (hh == 0))
    def _():
        g_ref[...] = jnp.zeros(g_ref.shape, F32)
        issue(b, hh, 0)

    @pl.when(hh + 1 < n_half)
    def _():
        issue(b, hh + 1, 1 - hh % 2)

    if n_half % 2 == 0:
        @pl.when((hh + 1 == n_half) & (b + 1 < nb))
        def _():
            issue(b + 1, 0, 0)

    slot = hh % 2
    s0, n = block_rows(b, hh)
    offs = []
    off = jnp.int32(0)
    for i in range(eh):
        offs.append(off)
        off = off + n[i]
    total = off

    stage_rows = g_ref.shape[1]
    bit = 1
    while bit <= stage_rows:
        @pl.when((total & bit) != 0)
        def _(bit=bit):
            row_copy(0, 0, slot, n_rows=bit).wait()
        bit *= 2

    @pl.when(hh == 0)
    def _():
        acc_ref[...] = h_ref[...]

    r_iota = lax.broadcasted_iota(I32, (rows, CHUNK), 0)
    d = acc_ref.shape[1]

    def chunk(ci, carry):
        r0 = pl.multiple_of(ci * rows, rows)
        rr = r_iota + r0
        onehot = jnp.zeros((rows, CHUNK), F32)
        gate = jnp.zeros((rows, CHUNK), F32)
        for i in range(eh):
            prow = pos_ref[pl.ds(hh * eh + i, 1), :]
            arow = aff_ref[pl.ds(hh * eh + i, 1), :]
            tgt = jnp.where(prow >= 0, prow - s0[i] + offs[i], -1)
            hit = rr == tgt
            onehot = jnp.where(hit, 1.0, onehot)
            gate = jnp.where(hit, arow, gate)
        gcol = jnp.sum(gate, axis=1, keepdims=True)
        oh = onehot.T.astype(BF16)
        for c0 in range(0, d, COMBINE_SLAB):
            gs = (g_ref[slot, pl.ds(r0, rows), c0:c0 + COMBINE_SLAB] * gcol).astype(BF16)
            acc_ref[:, c0:c0 + COMBINE_SLAB] += jnp.dot(oh, gs, preferred_element_type=F32)
        return carry

    lax.fori_loop(0, (total + rows - 1) // rows, chunk, 0)

    @pl.when(hh == n_half - 1)
    def _():
        x = acc_ref[...]
        if emit_h:
            o_ref[...] = x
        y = x * lax.rsqrt(jnp.mean(x * x, axis=-1, keepdims=True) + NORM_EPS) * nw_ref[...]
        y_ref[...] = y.astype(y_ref.dtype)


def moe_combine(h, pos, aff, starts, ffn_out, next_norm_w, last):
    lp, d = h.shape
    ne, capp, _ = ffn_out.shape
    rows = CHUNK
    n_half = 2
    stage_rows = (ne // n_half) * CHUNK
    blk = pl.BlockSpec((CHUNK, d), lambda b, hh, *_: (b, 0))
    if last:
        out_specs = pl.BlockSpec((CHUNK, d), lambda b, hh, *_: (jnp.maximum(b - 1, 0), 0))
        out_shape = jax.ShapeDtypeStruct((lp - CHUNK, d), F32)
    else:
        out_specs = [blk, blk]
        out_shape = [jax.ShapeDtypeStruct((lp, d), F32), jax.ShapeDtypeStruct((lp, d), BF16)]
    return pl.pallas_call(
        functools.partial(_combine_kernel, capp=capp, rows=rows, n_half=n_half, emit_h=not last),
        grid_spec=pltpu.PrefetchScalarGridSpec(
            num_scalar_prefetch=1,
            grid=(lp // CHUNK, n_half),
            in_specs=[
                blk,
                pl.BlockSpec((ne, CHUNK), lambda b, hh, *_: (0, b)),
                pl.BlockSpec((ne, CHUNK), lambda b, hh, *_: (0, b)),
                pl.BlockSpec((1, d), lambda b, hh, *_: (0, 0)),
                pl.BlockSpec(memory_space=pl.ANY),
            ],
            out_specs=out_specs,
            scratch_shapes=[pltpu.VMEM((CHUNK, d), F32), pltpu.VMEM((2, stage_rows, d), F32),
                            pltpu.SemaphoreType.DMA((2,))],
        ),
        out_shape=out_shape,
        compiler_params=_cparams(("arbitrary", "arbitrary")),
        name="moe_combine",
    )(starts, h, pos, aff, next_norm_w.reshape(1, d), ffn_out.reshape(ne * capp, d))


def moe_layer(h, norm_w, w_router, w_gate, w_up, w_down, layer, next_norm_w, last):
    u, logits_t = rmsnorm_router(h, norm_w, w_router)
    pos, aff, starts, idx_t = topk_route(logits_t)
    xs = gather_tokens(idx_t.T, u)
    ffn_out = expert_ffn(xs, w_gate, w_up, w_down, layer)
    return moe_combine(h, pos, aff, starts, ffn_out, next_norm_w, last)


def _scan_masks():
    lower = np.tril(np.ones((CHUNK, CHUNK), np.float32))
    return jnp.asarray(np.stack([lower, lower.T]))


def _pad_cols(w, width):
    return jnp.pad(w, ((0, 0), (0, width - w.shape[1])))


def _pad_rows(w, height):
    return jnp.pad(w, ((0, height - w.shape[0]), (0, 0)))


def ab_mixer(h, u, w_in, conv_w, conv_b, a_log, dt_bias, d_skip, ssd_norm, sink, w_out):
    zx_w = SSD_INNER + SSD_CONV_CH
    dt0 = zx_w
    qkv0 = zx_w + 2 * SSD_HEADS
    w_t = w_in.T
    z = matmul([u], w_t, 512, col0=0, n=SSD_INNER, w_t=True)
    xbc = matmul([u], w_t, 512, col0=SSD_INNER, n=SSD_CONV_CH, w_t=True)
    w_dt = jnp.concatenate([_pad_rows(w_t[dt0:dt0 + SSD_HEADS], CHUNK),
                            _pad_rows(w_t[dt0 + SSD_HEADS:qkv0], CHUNK)], axis=0)
    dt_raw = matmul([u], w_dt, 2 * CHUNK, w_t=True)
    qkv = matmul([u], w_t[qkv0:], 512, w_t=True)

    xbc_act = ssd_conv(xbc, conv_w, conv_b)
    dt_bias_rows = _pad_cols(dt_bias, CHUNK).reshape(2, 1, CHUNK)
    acoef_rows = _pad_cols(-jnp.exp(a_log), CHUNK).reshape(2, 1, CHUNK)
    yf, yb = ssd_scan(xbc_act, dt_raw, dt_bias_rows, acoef_rows, _scan_masks())
    d_skip_row = jnp.repeat(d_skip, SSD_HEAD_DIM).reshape(1, SSD_INNER)
    y_ssd = ssd_final(yf, yb, xbc_act, z, d_skip_row, ssd_norm)

    slopes = 2.0 ** (-8.0 * jnp.arange(1, ATT_Q_HEADS + 1, dtype=F32) / ATT_Q_HEADS)
    y_att = window_attention(qkv, slopes, sink)
    return matmul([y_ssd, y_att], w_out, 512, res=h)


def cd_mixer(h, u, w_in, i_bias, f_bias, head_norm, w_out):
    qkvo_w = 2 * ML_HEADS * ML_QK_DIM + 2 * ML_WIDTH
    i0 = qkvo_w
    f0 = i0 + 2 * ML_HEADS
    fn0 = f0 + 2 * ML_HEADS
    w_t = w_in.T
    qkvo = matmul([u], w_t, 512, col0=0, n=qkvo_w, w_t=True)

    def gate_block(dirn):
        wi = w_t[i0 + dirn * ML_HEADS:i0 + (dirn + 1) * ML_HEADS]
        wf = w_t[f0 + dirn * ML_HEADS:f0 + (dirn + 1) * ML_HEADS]
        return _pad_rows(jnp.concatenate([_pad_rows(wi, ML_F_LANE), wf], axis=0), CHUNK)

    def bias_block(dirn):
        row = jnp.concatenate([jnp.pad(i_bias[dirn], (0, ML_F_LANE - ML_HEADS)), f_bias[dirn]])
        return jnp.pad(row, (0, CHUNK - row.shape[0]))

    gates = matmul([u], jnp.concatenate([gate_block(0), gate_block(1)], axis=0), 2 * CHUNK, w_t=True)
    gate_bias_rows = jnp.stack([bias_block(0), bias_block(1)]).reshape(2, 1, CHUNK)
    hf, hb = mlstm_scan(qkvo, gates, gate_bias_rows, _scan_masks())
    y_ml = mlstm_final(hf, hb, qkvo, head_norm)

    u_fn = matmul([u], w_t[fn0:], 512, w_t=True)
    y_fn = fourier_mixer(u_fn[PAD:])
    y_fn = jnp.pad(y_fn, ((PAD, 0), (0, 0))).astype(BF16)
    return matmul([y_ml, y_fn], w_out, 512, res=h)


def kernel(x, meta_tokens, norm_mix, ab_w_in, ab_conv_w, ab_conv_b, ab_a_log, ab_dt_bias, ab_d_skip, ab_ssd_norm,
           ab_sink, ab_w_out, cd_w_in, cd_i_bias, cd_f_bias, cd_head_norm, cd_w_out, norm_ffn, moe_router,
           moe_w_gate, moe_w_up, moe_w_down, final_norm):
    bsz, seq, d = x.shape
    assert bsz == 1 and d == D_MODEL and seq % CHUNK == 0
    depth = norm_mix.shape[0]
    h = jnp.concatenate([jnp.zeros((PAD, d), F32), meta_tokens.astype(F32), x[0]], axis=0)
    u = rmsnorm(h, norm_mix[0], BF16)
    for layer in range(depth):
        j = layer // 2
        if layer % 2 == 0:
            h = ab_mixer(h, u, ab_w_in[j], ab_conv_w[j], ab_conv_b[j], ab_a_log[j], ab_dt_bias[j],
                         ab_d_skip[j], ab_ssd_norm[j], ab_sink[j], ab_w_out[j])
        else:
            h = cd_mixer(h, u, cd_w_in[j], cd_i_bias[j], cd_f_bias[j], cd_head_norm[j], cd_w_out[j])
        last = layer == depth - 1
        next_norm_w = final_norm if last else norm_mix[layer + 1]
        res = moe_layer(h, norm_ffn[layer], moe_router[layer], moe_w_gate, moe_w_up, moe_w_down, layer,
                        next_norm_w, last)
        if last:
            return res[None]
        h, u = res
```
